```python
import jax, jax.numpy as jnp
from jax import lax
import numpy as np

D_MODEL = 1024
BATCH = 4
SEQ = 4096
DEPTH = 4

N_EVEN = (DEPTH + 1) // 2
N_ODD = DEPTH // 2
RMS_EPS = 1e-5
LN_EPS = 1e-5

D_CONV = D_MODEL // 2
D_POOL = D_MODEL // 2
CONV_K = 31
POOL_WINDOWS = (2, 4, 8, 16)
N_POOL_GROUPS = len(POOL_WINDOWS)
POOL_GROUP = D_POOL // N_POOL_GROUPS
D_IN_EVEN = 2 * D_CONV + D_POOL
D_FF_DENSE = ((8 * D_MODEL // 3 + 127) // 128) * 128

HEAD_DIM = 64
N_Q_HEADS = D_MODEL // HEAD_DIM
N_KV_HEADS = 4
GQA_GROUP = N_Q_HEADS // N_KV_HEADS
WINDOW = 128
ATTN_BLOCK = 128
D_QKV = (N_Q_HEADS + 2 * N_KV_HEADS) * HEAD_DIM

N_EXPERTS = 8
TOP_K = 2
D_FF_EXPERT = 7 * D_MODEL // 2

kernel_name = "hybrid_conv_pool_swa_moe_trunk"


def rmsnorm(x, g):
    xf = x.astype(jnp.float32)
    y = xf * lax.rsqrt(jnp.mean(xf * xf, axis=-1, keepdims=True) + RMS_EPS)
    return (y * g.astype(jnp.float32)).astype(x.dtype)


def layernorm(x, g, b):
    xf = x.astype(jnp.float32)
    mu = jnp.mean(xf, axis=-1, keepdims=True)
    var = jnp.mean(jnp.square(xf - mu), axis=-1, keepdims=True)
    y = (xf - mu) * lax.rsqrt(var + LN_EPS)
    return (y * g.astype(jnp.float32) + b.astype(jnp.float32)).astype(x.dtype)


def swiglu(t, w1, w3, w2):
    return (jax.nn.silu(t @ w1) * (t @ w3)) @ w2


def causal_depthwise_conv(u, w, b):
    up = jnp.pad(u, ((0, 0), (CONV_K - 1, 0), (0, 0)))
    y = lax.conv_general_dilated(
        up, w.astype(u.dtype)[:, None, :], window_strides=(1,), padding='VALID',
        dimension_numbers=('NWC', 'WIO', 'NWC'), feature_group_count=u.shape[-1])
    return y + b.astype(u.dtype)


def multiscale_causal_pool(u):
    bsz, s, _ = u.shape
    ug = u.reshape(bsz, s, N_POOL_GROUPS, POOL_GROUP).astype(jnp.float32)
    cs = jnp.cumsum(ug, axis=1)
    pos = jnp.arange(s)
    pooled = []
    for g, w in enumerate(POOL_WINDOWS):
        c = cs[:, :, g]
        lag = jnp.pad(c, ((0, 0), (w, 0), (0, 0)))[:, :s]
        cnt = jnp.minimum(pos + 1, w).astype(jnp.float32)[None, :, None]
        pooled.append((c - lag) / cnt)
    pooled = jnp.stack(pooled, axis=2)
    return (pooled - ug).astype(u.dtype)


def conv_pool_mixer(hn, w_in, conv_w, conv_b, ln_g, ln_b, pool_w, pool_scale, w_out):
    bsz, s, _ = hn.shape
    u = hn @ w_in
    a_val = u[..., :D_CONV]
    a_gate = u[..., D_CONV:2 * D_CONV]
    b_in = u[..., 2 * D_CONV:]
    a = a_val * jax.nn.sigmoid(a_gate)
    a = causal_depthwise_conv(a, conv_w, conv_b)
    a = jax.nn.silu(layernorm(a, ln_g, ln_b))
    p = multiscale_causal_pool(b_in)
    p = jnp.einsum('bsgc,gcd->bsgd', p, pool_w).reshape(bsz, s, D_POOL)
    p = p * pool_scale
    return jnp.concatenate([a, p], axis=-1) @ w_out


def sliding_window_sink_attention(q, k, v, sinks):
    bsz, s = q.shape[:2]
    nb = s // ATTN_BLOCK
    qb = q.reshape(bsz, nb, ATTN_BLOCK, N_KV_HEADS, GQA_GROUP, HEAD_DIM)

    def with_prev_block(t):
        tb = t.reshape(bsz, nb, ATTN_BLOCK, N_KV_HEADS, HEAD_DIM)
        prev = jnp.pad(tb, ((0, 0), (1, 0), (0, 0), (0, 0), (0, 0)))[:, :-1]
        return jnp.concatenate([prev, tb], axis=2)

    kc = with_prev_block(k)
    vc = with_prev_block(v)
    scores = jnp.einsum('bnqkgd,bnskd->bnkgqs', qb, kc).astype(jnp.float32)
    scores = scores * (HEAD_DIM ** -0.5)
    blk = jnp.arange(nb)[:, None, None]
    qpos = blk * ATTN_BLOCK + jnp.arange(ATTN_BLOCK)[None, :, None]
    kpos = (blk - 1) * ATTN_BLOCK + jnp.arange(2 * ATTN_BLOCK)[None, None, :]
    diff = qpos - kpos
    mask = (diff >= 0) & (diff < WINDOW) & (kpos >= 0)
    scores = jnp.where(mask[None, :, None, None], scores, -jnp.inf)
    sink = sinks.astype(jnp.float32).reshape(N_KV_HEADS, GQA_GROUP)[None, None, :, :, None, None]
    m = jnp.maximum(jnp.max(scores, axis=-1, keepdims=True), sink)
    e = jnp.exp(scores - m)
    denom = jnp.sum(e, axis=-1, keepdims=True) + jnp.exp(sink - m)
    probs = (e / denom).astype(v.dtype)
    out = jnp.einsum('bnkgqs,bnskd->bnqkgd', probs, vc)
    return out.reshape(bsz, s, N_Q_HEADS * HEAD_DIM)


def swa_mixer(hn, w_qkv, b_qkv, sinks, w_o, b_o):
    bsz, s, _ = hn.shape
    u = hn @ w_qkv + b_qkv
    dq = N_Q_HEADS * HEAD_DIM
    dk = N_KV_HEADS * HEAD_DIM
    q = u[..., :dq].reshape(bsz, s, N_Q_HEADS, HEAD_DIM)
    k = u[..., dq:dq + dk].reshape(bsz, s, N_KV_HEADS, HEAD_DIM)
    v = u[..., dq + dk:].reshape(bsz, s, N_KV_HEADS, HEAD_DIM)
    o = sliding_window_sink_attention(q, k, v, sinks)
    return o @ w_o + b_o


def moe_swiglu(hn, router_w, w1, w3, w2):
    bsz, s, d = hn.shape
    t = hn.reshape(-1, d)
    logits = (t @ router_w).astype(jnp.float32)
    top_v, top_i = lax.top_k(logits, TOP_K)
    gates = jax.nn.softmax(top_v, axis=-1)
    gate_te = jnp.sum(jax.nn.one_hot(top_i, N_EXPERTS, dtype=jnp.float32) * gates[..., None], axis=1)
    gate_te = gate_te.astype(t.dtype)
    out = jnp.zeros_like(t)
    for e in range(N_EXPERTS):
        out = out + gate_te[:, e:e + 1] * swiglu(t, w1[e], w3[e], w2[e])
    return out.reshape(bsz, s, d)


def _normal(k, shape, scale):
    return jax.random.normal(k, shape, jnp.float32) * scale


def setup_inputs(seed: int = 0) -> dict:
    key = jax.random.key(seed)
    ks = jax.random.split(key, 27)
    D = D_MODEL
    return {
        "x": _normal(ks[0], (BATCH, SEQ, D), 1.0),
        "e_norm1": 1.0 + _normal(ks[1], (N_EVEN, D), 0.05),
        "e_w_in": _normal(ks[2], (N_EVEN, D, D_IN_EVEN), D ** -0.5),
        "e_conv_w": _normal(ks[3], (N_EVEN, CONV_K, D_CONV), CONV_K ** -0.5),
        "e_conv_b": _normal(ks[4], (N_EVEN, D_CONV), 0.02),
        "e_ln_g": 1.0 + _normal(ks[5], (N_EVEN, D_CONV), 0.05),
        "e_ln_b": _normal(ks[6], (N_EVEN, D_CONV), 0.02),
        "e_pool_w": _normal(ks[7], (N_EVEN, N_POOL_GROUPS, POOL_GROUP, POOL_GROUP), POOL_GROUP ** -0.5),
        "e_pool_scale": 1.0 + _normal(ks[8], (N_EVEN, D_POOL), 0.05),
        "e_w_out": _normal(ks[9], (N_EVEN, D, D), D ** -0.5),
        "e_norm2": 1.0 + _normal(ks[10], (N_EVEN, D), 0.05),
        "e_ff_w1": _normal(ks[11], (N_EVEN, D, D_FF_DENSE), D ** -0.5),
        "e_ff_w3": _normal(ks[12], (N_EVEN, D, D_FF_DENSE), D ** -0.5),
        "e_ff_w2": _normal(ks[13], (N_EVEN, D_FF_DENSE, D), D_FF_DENSE ** -0.5),
        "o_norm1": 1.0 + _normal(ks[14], (N_ODD, D), 0.05),
        "o_w_qkv": _normal(ks[15], (N_ODD, D, D_QKV), D ** -0.5),
        "o_b_qkv": _normal(ks[16], (N_ODD, D_QKV), 0.02),
        "o_sinks": _normal(ks[17], (N_ODD, N_Q_HEADS), 0.5),
        "o_w_o": _normal(ks[18], (N_ODD, N_Q_HEADS * HEAD_DIM, D), (N_Q_HEADS * HEAD_DIM) ** -0.5),
        "o_b_o": _normal(ks[19], (N_ODD, D), 0.02),
        "o_norm2": 1.0 + _normal(ks[20], (N_ODD, D), 0.05),
        "o_router": _normal(ks[21], (N_ODD, D, N_EXPERTS), D ** -0.5),
        "o_exp_w1": _normal(ks[22], (N_ODD, N_EXPERTS, D, D_FF_EXPERT), D ** -0.5),
        "o_exp_w3": _normal(ks[23], (N_ODD, N_EXPERTS, D, D_FF_EXPERT), D ** -0.5),
        "o_exp_w2": _normal(ks[24], (N_ODD, N_EXPERTS, D_FF_EXPERT, D), D_FF_EXPERT ** -0.5),
        "final_norm": 1.0 + _normal(ks[25], (D,), 0.05),
    }


def reference(x, e_norm1, e_w_in, e_conv_w, e_conv_b, e_ln_g, e_ln_b, e_pool_w, e_pool_scale,
              e_w_out, e_norm2, e_ff_w1, e_ff_w3, e_ff_w2, o_norm1, o_w_qkv, o_b_qkv, o_sinks,
              o_w_o, o_b_o, o_norm2, o_router, o_exp_w1, o_exp_w3, o_exp_w2, final_norm):
    h = x
    for layer in range(DEPTH):
        i = layer // 2
        if layer % 2 == 0:
            h = h + conv_pool_mixer(rmsnorm(h, e_norm1[i]), e_w_in[i], e_conv_w[i], e_conv_b[i],
                                    e_ln_g[i], e_ln_b[i], e_pool_w[i], e_pool_scale[i], e_w_out[i])
            h = h + swiglu(rmsnorm(h, e_norm2[i]), e_ff_w1[i], e_ff_w3[i], e_ff_w2[i])
        else:
            h = h + swa_mixer(rmsnorm(h, o_norm1[i]), o_w_qkv[i], o_b_qkv[i], o_sinks[i],
                              o_w_o[i], o_b_o[i])
            h = h + moe_swiglu(rmsnorm(h, o_norm2[i]), o_router[i], o_exp_w1[i], o_exp_w3[i],
                               o_exp_w2[i])
    return rmsnorm(h, final_norm)
```

```python
import functools

import jax
import jax.numpy as jnp
from jax import lax
from jax.experimental import pallas as pl
from jax.experimental.pallas import tpu as pltpu

F32 = jnp.float32
BF16 = jnp.bfloat16

D_MODEL = 1024
DEPTH = 4
RMS_EPS = 1e-5
LN_EPS = 1e-5

D_CONV = 512
D_POOL = 512
CONV_K = 31
POOL_WINDOWS = (2, 4, 8, 16)
POOL_GROUP = 128
D_IN_EVEN = 2 * D_CONV + D_POOL

HEAD_DIM = 64
N_Q_HEADS = 16
N_KV_HEADS = 4
ATTN_BLOCK = 128
D_Q = N_Q_HEADS * HEAD_DIM
D_KV = N_KV_HEADS * HEAD_DIM
D_QKV = D_Q + 2 * D_KV

N_EXPERTS = 8

LANES = 128
SEQ_TILE = 512
HALO = 32
CONV_ROWS = 32
VMEM_LIMIT = 56 * 1024 * 1024


def _rms(x, g):
    ms = jnp.mean(x * x, axis=-1, keepdims=True)
    return x * lax.rsqrt(ms + RMS_EPS) * g


def _silu(x):
    return x * jax.nn.sigmoid(x)


def _even_mixer_kernel(x_ref, g_ref, win_ref, cw_ref, cb_ref, lg_ref, lb_ref,
                       pw_ref, ps_ref, wout_ref, o_ref,
                       abuf, bbuf, p1, p2, p3, cat):
    s = pl.program_id(1)
    ts = SEQ_TILE
    rows = HALO + ts

    @pl.when(s == 0)
    def _():
        abuf[0:HALO, :] = jnp.zeros((HALO, D_CONV), F32)
        bbuf[0:HALO, :] = jnp.zeros((HALO, D_POOL), F32)

    x = x_ref[0]
    hn = _rms(x, g_ref[...]).astype(BF16)
    u = jnp.dot(hn, win_ref[...], preferred_element_type=F32)
    abuf[HALO:rows, :] = u[:, :D_CONV] * jax.nn.sigmoid(u[:, D_CONV:2 * D_CONV])
    bbuf[HALO:rows, :] = u[:, 2 * D_CONV:]

    for c in range(ts // CONV_ROWS):
        r0 = c * CONV_ROWS
        acc = jnp.broadcast_to(cb_ref[...], (CONV_ROWS, D_CONV))
        for k in range(CONV_K):
            start = r0 + HALO - (CONV_K - 1) + k
            acc = acc + cw_ref[k:k + 1, :] * abuf[start:start + CONV_ROWS, :]
        mu = jnp.mean(acc, axis=-1, keepdims=True)
        d = acc - mu
        var = jnp.mean(d * d, axis=-1, keepdims=True)
        y = d * lax.rsqrt(var + LN_EPS) * lg_ref[...] + lb_ref[...]
        cat[r0:r0 + CONV_ROWS, 0:D_CONV] = _silu(y).astype(BF16)

    p1[8:rows, :] = bbuf[8:rows, :] + bbuf[7:rows - 1, :]
    p2[16:rows, 0:384] = p1[16:rows, 128:512] + p1[14:rows - 2, 128:512]
    p3[24:rows, 0:256] = p2[24:rows, 128:384] + p2[20:rows - 4, 128:384]
    s16 = p3[HALO:rows, 128:256] + p3[HALO - 8:rows - 8, 128:256]
    sums = (p1[HALO:rows, 0:128], p2[HALO:rows, 0:128], p3[HALO:rows, 0:128], s16)
    pos1 = (s * ts + 1 + lax.broadcasted_iota(jnp.int32, (ts, 1), 0)).astype(F32)
    for g, w in enumerate(POOL_WINDOWS):
        cnt = jnp.minimum(pos1, float(w))
        pg = sums[g] / cnt - bbuf[HALO:rows, g * POOL_GROUP:(g + 1) * POOL_GROUP]
        pm = jnp.dot(pg.astype(BF16), pw_ref[g], preferred_element_type=F32)
        pm = pm * ps_ref[:, g * POOL_GROUP:(g + 1) * POOL_GROUP]
        cat[:, D_CONV + g * POOL_GROUP:D_CONV + (g + 1) * POOL_GROUP] = pm.astype(BF16)

    o_ref[0] = x + jnp.dot(cat[...], wout_ref[...], preferred_element_type=F32)

    abuf[0:HALO, :] = abuf[ts:rows, :]
    bbuf[0:HALO, :] = bbuf[ts:rows, :]


def _even_mixer(h, g, w_in, conv_w, conv_b, ln_g, ln_b, pool_w, pool_scale, w_out):
    b, s, d = h.shape
    ts = SEQ_TILE
    rows = HALO + ts
    const = lambda shape: pl.BlockSpec(shape, lambda i, j: (0,) * len(shape))
    return pl.pallas_call(
        _even_mixer_kernel,
        out_shape=jax.ShapeDtypeStruct(h.shape, F32),
        grid=(b, s // ts),
        in_specs=[
            pl.BlockSpec((1, ts, d), lambda i, j: (i, j, 0)),
            const((1, d)),
            const((d, D_IN_EVEN)),
            const((CONV_K, D_CONV)),
            const((1, D_CONV)),
            const((1, D_CONV)),
            const((1, D_CONV)),
            const((len(POOL_WINDOWS), POOL_GROUP, POOL_GROUP)),
            const((1, D_POOL)),
            const((d, d)),
        ],
        out_specs=pl.BlockSpec((1, ts, d), lambda i, j: (i, j, 0)),
        scratch_shapes=[
            pltpu.VMEM((rows, D_CONV), F32),
            pltpu.VMEM((rows, D_POOL), F32),
            pltpu.VMEM((rows, D_POOL), F32),
            pltpu.VMEM((rows, 384), F32),
            pltpu.VMEM((rows, 256), F32),
            pltpu.VMEM((ts, d), BF16),
        ],
        compiler_params=pltpu.CompilerParams(
            dimension_semantics=("arbitrary", "arbitrary"),
            vmem_limit_bytes=VMEM_LIMIT),
        name="even_mixer",
    )(h, g.reshape(1, d), w_in.astype(BF16), conv_w, conv_b.reshape(1, -1),
      ln_g.reshape(1, -1), ln_b.reshape(1, -1), pool_w.astype(BF16),
      pool_scale.reshape(1, -1), w_out.astype(BF16))


def _attn_kernel(sink_ref, x_ref, g_ref, wqkv_ref, bqkv_ref, wo_ref, bo_ref, o_ref,
                 kbuf, vbuf, obuf):
    s = pl.program_id(1)
    ts = SEQ_TILE
    blk = ATTN_BLOCK
    group = N_Q_HEADS // N_KV_HEADS

    @pl.when(s == 0)
    def _():
        kbuf[0:blk, :] = jnp.zeros((blk, N_KV_HEADS * LANES), BF16)
        vbuf[0:blk, :] = jnp.zeros((blk, N_KV_HEADS * LANES), BF16)

    x = x_ref[0]
    hn = _rms(x, g_ref[...]).astype(BF16)
    u = jnp.dot(hn, wqkv_ref[...], preferred_element_type=F32) + bqkv_ref[...]
    q = (u[:, :D_Q] * (HEAD_DIM ** -0.5)).astype(BF16)

    lo_t = lax.broadcasted_iota(jnp.int32, (ts, LANES), 1) < HEAD_DIM
    for buf, base in ((kbuf, D_Q), (vbuf, D_Q + D_KV)):
        for p in range(N_KV_HEADS // 2):
            t = u[:, base + p * LANES:base + (p + 1) * LANES]
            r = pltpu.roll(t, HEAD_DIM, axis=1)
            buf[blk:blk + ts, (2 * p) * LANES:(2 * p + 1) * LANES] = jnp.where(lo_t, t, r).astype(BF16)
            buf[blk:blk + ts, (2 * p + 1) * LANES:(2 * p + 2) * LANES] = jnp.where(lo_t, r, t).astype(BF16)

    lo_b = lax.broadcasted_iota(jnp.int32, (blk, LANES), 1) < HEAD_DIM
    rowi = lax.broadcasted_iota(jnp.int32, (blk, 2 * blk), 0)
    coli = lax.broadcasted_iota(jnp.int32, (blk, 2 * blk), 1)
    band = (coli > rowi) & (coli <= rowi + blk)
    band0 = band & ((coli >= blk) | (s > 0))
    zero_b = jnp.zeros((blk, LANES), BF16)

    for j in range(ts // blk):
        msk = band0 if j == 0 else band
        msk4 = jnp.concatenate([msk] * group, axis=0)
        for g in range(N_KV_HEADS):
            kk = kbuf[j * blk:(j + 2) * blk, g * LANES:(g + 1) * LANES]
            vv = vbuf[j * blk:(j + 2) * blk, g * LANES:(g + 1) * LANES]
            qs = []
            sk = []
            for i in range(group):
                h = g * group + i
                qt = q[j * blk:(j + 1) * blk, (h // 2) * LANES:(h // 2 + 1) * LANES]
                qs.append(jnp.where(lo_b, qt, zero_b) if h % 2 == 0 else jnp.where(lo_b, zero_b, qt))
                sk.append(jnp.full((blk, 1), sink_ref[h], F32))
            q4 = jnp.concatenate(qs, axis=0)
            sink = jnp.concatenate(sk, axis=0)
            sc = lax.dot_general(q4, kk, (((1,), (1,)), ((), ())), preferred_element_type=F32)
            sc = jnp.where(msk4, sc, -jnp.inf)
            m = jnp.maximum(jnp.max(sc, axis=-1, keepdims=True), sink)
            e = jnp.exp(sc - m)
            denom = jnp.sum(e, axis=-1, keepdims=True) + jnp.exp(sink - m)
            pv = jnp.dot(e.astype(BF16), vv, preferred_element_type=F32)
            pv = pv * (1.0 / denom)
            for pr in range(group // 2):
                ev = pv[(2 * pr) * blk:(2 * pr + 1) * blk]
                od = pv[(2 * pr + 1) * blk:(2 * pr + 2) * blk]
                tile = (g * group) // 2 + pr
                obuf[j * blk:(j + 1) * blk, tile * LANES:(tile + 1) * LANES] = (
                    jnp.where(lo_b, ev, od).astype(BF16))

    o_ref[0] = x + jnp.dot(obuf[...], wo_ref[...], preferred_element_type=F32) + bo_ref[...]

    kbuf[0:blk, :] = kbuf[ts:ts + blk, :]
    vbuf[0:blk, :] = vbuf[ts:ts + blk, :]


def _attn_mixer(h, g, w_qkv, b_qkv, sinks, w_o, b_o):
    b, s, d = h.shape
    ts = SEQ_TILE
    const = lambda shape: pl.BlockSpec(shape, lambda i, j, sk: (0,) * len(shape))
    grid_spec = pltpu.PrefetchScalarGridSpec(
        num_scalar_prefetch=1,
        grid=(b, s // ts),
        in_specs=[
            pl.BlockSpec((1, ts, d), lambda i, j, sk: (i, j, 0)),
            const((1, d)),
            const((d, D_QKV)),
            const((1, D_QKV)),
            const((D_Q, d)),
            const((1, d)),
        ],
        out_specs=pl.BlockSpec((1, ts, d), lambda i, j, sk: (i, j, 0)),
        scratch_shapes=[
            pltpu.VMEM((ATTN_BLOCK + ts, N_KV_HEADS * LANES), BF16),
            pltpu.VMEM((ATTN_BLOCK + ts, N_KV_HEADS * LANES), BF16),
            pltpu.VMEM((ts, D_Q), BF16),
        ],
    )
    return pl.pallas_call(
        _attn_kernel,
        out_shape=jax.ShapeDtypeStruct(h.shape, F32),
        grid_spec=grid_spec,
        compiler_params=pltpu.CompilerParams(
            dimension_semantics=("arbitrary", "arbitrary"),
            vmem_limit_bytes=VMEM_LIMIT),
        name="attn_mixer",
    )(sinks, h, g.reshape(1, d), w_qkv.astype(BF16), b_qkv.reshape(1, -1),
      w_o.astype(BF16), b_o.reshape(1, d))


def _top2_gates(logits):
    lane = lax.broadcasted_iota(jnp.int32, logits.shape, 1)
    lg = jnp.where(lane < N_EXPERTS, logits, -jnp.inf)
    m1 = jnp.max(lg, axis=-1, keepdims=True)
    i1 = jnp.min(jnp.where(lg == m1, lane, LANES), axis=-1, keepdims=True)
    lg2 = jnp.where(lane == i1, -jnp.inf, lg)
    m2 = jnp.max(lg2, axis=-1, keepdims=True)
    i2 = jnp.min(jnp.where(lg2 == m2, lane, LANES), axis=-1, keepdims=True)
    e2 = jnp.exp(m2 - m1)
    den = 1.0 + e2
    return jnp.where(lane == i1, 1.0 / den, 0.0) + jnp.where(lane == i2, e2 / den, 0.0)


def _ffn_kernel(x_ref, g_ref, rw_ref, w1_ref, w3_ref, w2_ref, o_ref, hn_s, acc_s, gate_s,
                *, gated):
    e = pl.program_id(1)
    f = pl.program_id(2)

    @pl.when((e == 0) & (f == 0))
    def _():
        x = x_ref[...]
        hn = _rms(x, g_ref[...])
        hn_s[...] = hn.astype(BF16)
        acc_s[...] = x
        if gated:
            logits = jnp.dot(hn, rw_ref[...], preferred_element_type=F32,
                             precision=lax.Precision.HIGHEST)
            gate_s[...] = _top2_gates(logits)

    hn = hn_s[...]
    t = _silu(jnp.dot(hn, w1_ref[0], preferred_element_type=F32))
    t = t * jnp.dot(hn, w3_ref[0], preferred_element_type=F32)
    if gated:
        lane = lax.broadcasted_iota(jnp.int32, gate_s.shape, 1)
        t = t * jnp.sum(jnp.where(lane == e, gate_s[...], 0.0), axis=-1, keepdims=True)
    acc_s[...] += jnp.dot(t.astype(BF16), w2_ref[0], preferred_element_type=F32)

    @pl.when((e == pl.num_programs(1) - 1) & (f == pl.num_programs(2) - 1))
    def _():
        o_ref[...] = acc_s[...]


def _ffn(h2d, g, router_w, w1, w3, w2, *, tm, tf, gated):
    t, d = h2d.shape
    n_exp, _, ff = w1.shape
    rw = jnp.zeros((d, LANES), F32).at[:, :router_w.shape[1]].set(router_w)
    return pl.pallas_call(
        functools.partial(_ffn_kernel, gated=gated),
        out_shape=jax.ShapeDtypeStruct((t, d), F32),
        grid=(t // tm, n_exp, ff // tf),
        in_specs=[
            pl.BlockSpec((tm, d), lambda i, e, f: (i, 0)),
            pl.BlockSpec((1, d), lambda i, e, f: (0, 0)),
            pl.BlockSpec((d, LANES), lambda i, e, f: (0, 0)),
            pl.BlockSpec((1, d, tf), lambda i, e, f: (e, 0, f)),
            pl.BlockSpec((1, d, tf), lambda i, e, f: (e, 0, f)),
            pl.BlockSpec((1, tf, d), lambda i, e, f: (e, f, 0)),
        ],
        out_specs=pl.BlockSpec((tm, d), lambda i, e, f: (i, 0)),
        scratch_shapes=[
            pltpu.VMEM((tm, d), BF16),
            pltpu.VMEM((tm, d), F32),
            pltpu.VMEM((tm, LANES), F32),
        ],
        compiler_params=pltpu.CompilerParams(
            dimension_semantics=("arbitrary", "arbitrary", "arbitrary"),
            vmem_limit_bytes=VMEM_LIMIT),
        name="moe_ffn" if gated else "dense_ffn",
    )(h2d, g.reshape(1, d), rw, w1.astype(BF16), w3.astype(BF16), w2.astype(BF16))


def _final_norm_kernel(x_ref, g_ref, o_ref):
    o_ref[...] = _rms(x_ref[...], g_ref[...])


def _final_norm(h2d, g):
    t, d = h2d.shape
    tm = 1024
    return pl.pallas_call(
        _final_norm_kernel,
        out_shape=jax.ShapeDtypeStruct((t, d), F32),
        grid=(t // tm,),
        in_specs=[pl.BlockSpec((tm, d), lambda i: (i, 0)),
                  pl.BlockSpec((1, d), lambda i: (0, 0))],
        out_specs=pl.BlockSpec((tm, d), lambda i: (i, 0)),
        name="final_norm",
    )(h2d, g.reshape(1, d))


def kernel(x, e_norm1, e_w_in, e_conv_w, e_conv_b, e_ln_g, e_ln_b, e_pool_w, e_pool_scale,
           e_w_out, e_norm2, e_ff_w1, e_ff_w3, e_ff_w2, o_norm1, o_w_qkv, o_b_qkv, o_sinks,
           o_w_o, o_b_o, o_norm2, o_router, o_exp_w1, o_exp_w3, o_exp_w2, final_norm):
    b, s, d = x.shape
    h = x
    for layer in range(DEPTH):
        i = layer // 2
        if layer % 2 == 0:
            h = _even_mixer(h, e_norm1[i], e_w_in[i], e_conv_w[i], e_conv_b[i], e_ln_g[i],
                            e_ln_b[i], e_pool_w[i], e_pool_scale[i], e_w_out[i])
            h = _ffn(h.reshape(b * s, d), e_norm2[i], jnp.zeros((d, 1), F32),
                     e_ff_w1[i][None], e_ff_w3[i][None], e_ff_w2[i][None],
                     tm=512, tf=1408, gated=False).reshape(b, s, d)
        else:
            h = _attn_mixer(h, o_norm1[i], o_w_qkv[i], o_b_qkv[i], o_sinks[i], o_w_o[i],
                            o_b_o[i])
            h = _ffn(h.reshape(b * s, d), o_norm2[i], o_router[i], o_exp_w1[i], o_exp_w3[i],
                     o_exp_w2[i], tm=1024, tf=512, gated=True).reshape(b, s, d)
    return _final_norm(h.reshape(b * s, d), final_norm).reshape(b, s, d)
```

```python
import functools

import jax
import jax.numpy as jnp
from jax import lax
from jax.experimental import pallas as pl
from jax.experimental.pallas import tpu as pltpu

F32 = jnp.float32
BF16 = jnp.bfloat16

D_MODEL = 1024
DEPTH = 4
RMS_EPS = 1e-5
LN_EPS = 1e-5

D_CONV = 512
D_POOL = 512
CONV_K = 31
POOL_WINDOWS = (2, 4, 8, 16)
POOL_GROUP = 128
D_IN_EVEN = 2 * D_CONV + D_POOL

HEAD_DIM = 64
N_Q_HEADS = 16
N_KV_HEADS = 4
ATTN_BLOCK = 128
D_Q = N_Q_HEADS * HEAD_DIM
D_KV = N_KV_HEADS * HEAD_DIM
D_QKV = D_Q + 2 * D_KV

N_EXPERTS = 8

LANES = 128
SEQ_TILE = 512
HALO = 32
CONV_ROWS = 32
VMEM_LIMIT = 56 * 1024 * 1024


def _rms(x, g):
    ms = jnp.mean(x * x, axis=-1, keepdims=True)
    return x * lax.rsqrt(ms + RMS_EPS) * g


def _silu(x):
    return x * jax.nn.sigmoid(x)


def _even_mixer_kernel(x_ref, g_ref, win_ref, cw_ref, cb_ref, lg_ref, lb_ref,
                       pw_ref, ps_ref, wout_ref, o_ref,
                       abuf, bbuf, p1, p2, p3, cat):
    s = pl.program_id(1)
    ts = SEQ_TILE
    rows = HALO + ts

    @pl.when(s == 0)
    def _():
        abuf[0:HALO, :] = jnp.zeros((HALO, D_CONV), F32)
        bbuf[0:HALO, :] = jnp.zeros((HALO, D_POOL), F32)

    x = x_ref[0]
    hn = _rms(x, g_ref[...]).astype(BF16)
    u = jnp.dot(hn, win_ref[...], preferred_element_type=F32)
    abuf[HALO:rows, :] = u[:, :D_CONV] * jax.nn.sigmoid(u[:, D_CONV:2 * D_CONV])
    bbuf[HALO:rows, :] = u[:, 2 * D_CONV:]

    for c in range(ts // CONV_ROWS):
        r0 = c * CONV_ROWS
        acc = jnp.broadcast_to(cb_ref[...], (CONV_ROWS, D_CONV))
        for k in range(CONV_K):
            start = r0 + HALO - (CONV_K - 1) + k
            acc = acc + cw_ref[k:k + 1, :] * abuf[start:start + CONV_ROWS, :]
        mu = jnp.mean(acc, axis=-1, keepdims=True)
        d = acc - mu
        var = jnp.mean(d * d, axis=-1, keepdims=True)
        y = d * lax.rsqrt(var + LN_EPS) * lg_ref[...] + lb_ref[...]
        cat[r0:r0 + CONV_ROWS, 0:D_CONV] = _silu(y).astype(BF16)

    p1[8:rows, :] = bbuf[8:rows, :] + bbuf[7:rows - 1, :]
    p2[16:rows, 0:384] = p1[16:rows, 128:512] + p1[14:rows - 2, 128:512]
    p3[24:rows, 0:256] = p2[24:rows, 128:384] + p2[20:rows - 4, 128:384]
    s16 = p3[HALO:rows, 128:256] + p3[HALO - 8:rows - 8, 128:256]
    sums = (p1[HALO:rows, 0:128], p2[HALO:rows, 0:128], p3[HALO:rows, 0:128], s16)
    pos1 = (s * ts + 1 + lax.broadcasted_iota(jnp.int32, (ts, 1), 0)).astype(F32)
    for g, w in enumerate(POOL_WINDOWS):
        cnt = jnp.minimum(pos1, float(w))
        pg = sums[g] / cnt - bbuf[HALO:rows, g * POOL_GROUP:(g + 1) * POOL_GROUP]
        pm = jnp.dot(pg.astype(BF16), pw_ref[g], preferred_element_type=F32)
        pm = pm * ps_ref[:, g * POOL_GROUP:(g + 1) * POOL_GROUP]
        cat[:, D_CONV + g * POOL_GROUP:D_CONV + (g + 1) * POOL_GROUP] = pm.astype(BF16)

    o_ref[0] = x + jnp.dot(cat[...], wout_ref[...], preferred_element_type=F32)

    abuf[0:HALO, :] = abuf[ts:rows, :]
    bbuf[0:HALO, :] = bbuf[ts:rows, :]


def _even_mixer(h, g, w_in, conv_w, conv_b, ln_g, ln_b, pool_w, pool_scale, w_out):
    b, s, d = h.shape
    ts = SEQ_TILE
    rows = HALO + ts
    const = lambda shape: pl.BlockSpec(shape, lambda i, j: (0,) * len(shape))
    return pl.pallas_call(
        _even_mixer_kernel,
        out_shape=jax.ShapeDtypeStruct(h.shape, F32),
        grid=(b, s // ts),
        in_specs=[
            pl.BlockSpec((1, ts, d), lambda i, j: (i, j, 0)),
            const((1, d)),
            const((d, D_IN_EVEN)),
            const((CONV_K, D_CONV)),
            const((1, D_CONV)),
            const((1, D_CONV)),
            const((1, D_CONV)),
            const((len(POOL_WINDOWS), POOL_GROUP, POOL_GROUP)),
            const((1, D_POOL)),
            const((d, d)),
        ],
        out_specs=pl.BlockSpec((1, ts, d), lambda i, j: (i, j, 0)),
        scratch_shapes=[
            pltpu.VMEM((rows, D_CONV), F32),
            pltpu.VMEM((rows, D_POOL), F32),
            pltpu.VMEM((rows, D_POOL), F32),
            pltpu.VMEM((rows, 384), F32),
            pltpu.VMEM((rows, 256), F32),
            pltpu.VMEM((ts, d), BF16),
        ],
        compiler_params=pltpu.CompilerParams(
            dimension_semantics=("arbitrary", "arbitrary"),
            vmem_limit_bytes=VMEM_LIMIT),
        name="even_mixer",
    )(h, g.reshape(1, d), w_in.astype(BF16), conv_w, conv_b.reshape(1, -1),
      ln_g.reshape(1, -1), ln_b.reshape(1, -1), pool_w.astype(BF16),
      pool_scale.reshape(1, -1), w_out.astype(BF16))


def _attn_kernel(sink_ref, x_ref, g_ref, wqkv_ref, bqkv_ref, wo_ref, bo_ref, o_ref,
                 kbuf, vbuf, obuf):
    s = pl.program_id(1)
    ts = SEQ_TILE
    blk = ATTN_BLOCK
    group = N_Q_HEADS // N_KV_HEADS

    @pl.when(s == 0)
    def _():
        kbuf[0:blk, :] = jnp.zeros((blk, N_KV_HEADS * LANES), BF16)
        vbuf[0:blk, :] = jnp.zeros((blk, N_KV_HEADS * LANES), BF16)

    x = x_ref[0]
    hn = _rms(x, g_ref[...]).astype(BF16)
    u = jnp.dot(hn, wqkv_ref[...], preferred_element_type=F32) + bqkv_ref[...]
    q = (u[:, :D_Q] * (HEAD_DIM ** -0.5)).astype(BF16)

    lo_t = lax.broadcasted_iota(jnp.int32, (ts, LANES), 1) < HEAD_DIM
    for buf, base in ((kbuf, D_Q), (vbuf, D_Q + D_KV)):
        for p in range(N_KV_HEADS // 2):
            t = u[:, base + p * LANES:base + (p + 1) * LANES]
            r = pltpu.roll(t, HEAD_DIM, axis=1)
            buf[blk:blk + ts, (2 * p) * LANES:(2 * p + 1) * LANES] = jnp.where(lo_t, t, r).astype(BF16)
            buf[blk:blk + ts, (2 * p + 1) * LANES:(2 * p + 2) * LANES] = jnp.where(lo_t, r, t).astype(BF16)

    lo_b = lax.broadcasted_iota(jnp.int32, (blk, LANES), 1) < HEAD_DIM
    rowi = lax.broadcasted_iota(jnp.int32, (blk, 2 * blk), 0)
    coli = lax.broadcasted_iota(jnp.int32, (blk, 2 * blk), 1)
    band = (coli > rowi) & (coli <= rowi + blk)
    band0 = band & ((coli >= blk) | (s > 0))
    zero_b = jnp.zeros((blk, LANES), BF16)

    for j in range(ts // blk):
        msk = band0 if j == 0 else band
        msk4 = jnp.concatenate([msk] * group, axis=0)
        for g in range(N_KV_HEADS):
            kk = kbuf[j * blk:(j + 2) * blk, g * LANES:(g + 1) * LANES]
            vv = vbuf[j * blk:(j + 2) * blk, g * LANES:(g + 1) * LANES]
            qs = []
            sk = []
            for i in range(group):
                h = g * group + i
                qt = q[j * blk:(j + 1) * blk, (h // 2) * LANES:(h // 2 + 1) * LANES]
                qs.append(jnp.where(lo_b, qt, zero_b) if h % 2 == 0 else jnp.where(lo_b, zero_b, qt))
                sk.append(jnp.full((blk, 1), sink_ref[h], F32))
            q4 = jnp.concatenate(qs, axis=0)
            sink = jnp.concatenate(sk, axis=0)
            sc = lax.dot_general(q4, kk, (((1,), (1,)), ((), ())), preferred_element_type=F32)
            sc = jnp.where(msk4, sc, -jnp.inf)
            m = jnp.maximum(jnp.max(sc, axis=-1, keepdims=True), sink)
            e = jnp.exp(sc - m)
            denom = jnp.sum(e, axis=-1, keepdims=True) + jnp.exp(sink - m)
            pv = jnp.dot(e.astype(BF16), vv, preferred_element_type=F32)
            pv = pv * (1.0 / denom)
            for pr in range(group // 2):
                ev = pv[(2 * pr) * blk:(2 * pr + 1) * blk]
                od = pv[(2 * pr + 1) * blk:(2 * pr + 2) * blk]
                tile = (g * group) // 2 + pr
                obuf[j * blk:(j + 1) * blk, tile * LANES:(tile + 1) * LANES] = (
                    jnp.where(lo_b, ev, od).astype(BF16))

    o_ref[0] = x + jnp.dot(obuf[...], wo_ref[...], preferred_element_type=F32) + bo_ref[...]

    kbuf[0:blk, :] = kbuf[ts:ts + blk, :]
    vbuf[0:blk, :] = vbuf[ts:ts + blk, :]


def _attn_mixer(h, g, w_qkv, b_qkv, sinks, w_o, b_o):
    b, s, d = h.shape
    ts = SEQ_TILE
    const = lambda shape: pl.BlockSpec(shape, lambda i, j, sk: (0,) * len(shape))
    grid_spec = pltpu.PrefetchScalarGridSpec(
        num_scalar_prefetch=1,
        grid=(b, s // ts),
        in_specs=[
            pl.BlockSpec((1, ts, d), lambda i, j, sk: (i, j, 0)),
            const((1, d)),
            const((d, D_QKV)),
            const((1, D_QKV)),
            const((D_Q, d)),
            const((1, d)),
        ],
        out_specs=pl.BlockSpec((1, ts, d), lambda i, j, sk: (i, j, 0)),
        scratch_shapes=[
            pltpu.VMEM((ATTN_BLOCK + ts, N_KV_HEADS * LANES), BF16),
            pltpu.VMEM((ATTN_BLOCK + ts, N_KV_HEADS * LANES), BF16),
            pltpu.VMEM((ts, D_Q), BF16),
        ],
    )
    return pl.pallas_call(
        _attn_kernel,
        out_shape=jax.ShapeDtypeStruct(h.shape, F32),
        grid_spec=grid_spec,
        compiler_params=pltpu.CompilerParams(
            dimension_semantics=("arbitrary", "arbitrary"),
            vmem_limit_bytes=VMEM_LIMIT),
        name="attn_mixer",
    )(sinks, h, g.reshape(1, d), w_qkv.astype(BF16), b_qkv.reshape(1, -1),
      w_o.astype(BF16), b_o.reshape(1, d))


def _ffn_kernel(x_ref, g_ref, w1_ref, w3_ref, w2_ref, o_ref, hn_s, acc_s):
    f = pl.program_id(1)

    @pl.when(f == 0)
    def _():
        x = x_ref[...]
        hn_s[...] = _rms(x, g_ref[...]).astype(BF16)
        acc_s[...] = x

    hn = hn_s[...]
    t = _silu(jnp.dot(hn, w1_ref[...], preferred_element_type=F32))
    t = t * jnp.dot(hn, w3_ref[...], preferred_element_type=F32)
    acc_s[...] += jnp.dot(t.astype(BF16), w2_ref[...], preferred_element_type=F32)

    @pl.when(f == pl.num_programs(1) - 1)
    def _():
        o_ref[...] = acc_s[...]


def _ffn(h2d, g, w1, w3, w2, *, tm, tf):
    t, d = h2d.shape
    ff = w1.shape[1]
    return pl.pallas_call(
        _ffn_kernel,
        out_shape=jax.ShapeDtypeStruct((t, d), F32),
        grid=(t // tm, ff // tf),
        in_specs=[
            pl.BlockSpec((tm, d), lambda i, f: (i, 0)),
            pl.BlockSpec((1, d), lambda i, f: (0, 0)),
            pl.BlockSpec((d, tf), lambda i, f: (0, f)),
            pl.BlockSpec((d, tf), lambda i, f: (0, f)),
            pl.BlockSpec((tf, d), lambda i, f: (f, 0)),
        ],
        out_specs=pl.BlockSpec((tm, d), lambda i, f: (i, 0)),
        scratch_shapes=[
            pltpu.VMEM((tm, d), BF16),
            pltpu.VMEM((tm, d), F32),
        ],
        compiler_params=pltpu.CompilerParams(
            dimension_semantics=("arbitrary", "arbitrary"),
            vmem_limit_bytes=VMEM_LIMIT),
        name="dense_ffn",
    )(h2d, g.reshape(1, d), w1.astype(BF16), w3.astype(BF16), w2.astype(BF16))


ROUTE_TILE = 512
EXPERT_TILE = 512
ROUTE_ROWS = 8


def _router_kernel(x_ref, g_ref, rw_ref, hn_ref, ri_ref, gcol_ref, cnt_ref, carry):
    i = pl.program_id(0)
    tb = ROUTE_TILE

    @pl.when(i == 0)
    def _():
        carry[...] = jnp.zeros(carry.shape, F32)

    hn = _rms(x_ref[...], g_ref[...])
    hn_ref[...] = hn
    logits = jnp.dot(hn, rw_ref[...], preferred_element_type=F32,
                     precision=lax.Precision.HIGHEST)
    lane = lax.broadcasted_iota(jnp.int32, (tb, LANES), 1)
    lg = jnp.where(lane < N_EXPERTS, logits, -jnp.inf)
    m1 = jnp.max(lg, axis=-1, keepdims=True)
    i1 = jnp.min(jnp.where(lg == m1, lane, LANES), axis=-1, keepdims=True)
    lg2 = jnp.where(lane == i1, -jnp.inf, lg)
    m2 = jnp.max(lg2, axis=-1, keepdims=True)
    i2 = jnp.min(jnp.where(lg2 == m2, lane, LANES), axis=-1, keepdims=True)
    e2 = jnp.exp(m2 - m1)
    den = 1.0 + e2
    gcol_ref[...] = jnp.where(lane == 0, 1.0 / den, jnp.where(lane == 1, e2 / den, 0.0))

    member = (lane == i1) | (lane == i2)
    mem_f = jnp.where(member, 1.0, 0.0)
    rr = lax.broadcasted_iota(jnp.int32, (tb, tb), 0)
    cc = lax.broadcasted_iota(jnp.int32, (tb, tb), 1)
    lower = jnp.where(cc < rr, 1.0, 0.0).astype(BF16)
    before = jnp.dot(lower, mem_f.astype(BF16), preferred_element_type=F32) + carry[0:1, :]
    r1 = jnp.sum(jnp.where(lane == i1, before, 0.0), axis=-1, keepdims=True)
    r2 = jnp.sum(jnp.where(lane == i2, before, 0.0), axis=-1, keepdims=True)
    new_carry = carry[0:1, :] + jnp.sum(mem_f, axis=0, keepdims=True)
    carry[...] = jnp.broadcast_to(new_carry, carry.shape)
    cnt_ref[...] = jnp.broadcast_to(new_carry, cnt_ref.shape)

    table = jnp.where(lane == 0, i1.astype(F32),
                      jnp.where(lane == 1, i2.astype(F32),
                                jnp.where(lane == 2, r1, jnp.where(lane == 3, r2, 0.0))))
    ri_ref[0] = jnp.transpose(table)[0:ROUTE_ROWS, :].astype(jnp.int32)


def _router(h2d, g, router_w):
    t, d = h2d.shape
    tb = ROUTE_TILE
    rw = jnp.zeros((d, LANES), F32).at[:, :N_EXPERTS].set(router_w)
    return pl.pallas_call(
        _router_kernel,
        out_shape=(
            jax.ShapeDtypeStruct((t, d), F32),
            jax.ShapeDtypeStruct((t // tb, ROUTE_ROWS, tb), jnp.int32),
            jax.ShapeDtypeStruct((t, LANES), F32),
            jax.ShapeDtypeStruct((8, LANES), F32),
        ),
        grid=(t // tb,),
        in_specs=[
            pl.BlockSpec((tb, d), lambda i: (i, 0)),
            pl.BlockSpec((1, d), lambda i: (0, 0)),
            pl.BlockSpec((d, LANES), lambda i: (0, 0)),
        ],
        out_specs=(
            pl.BlockSpec((tb, d), lambda i: (i, 0)),
            pl.BlockSpec((1, ROUTE_ROWS, tb), lambda i: (i, 0, 0)),
            pl.BlockSpec((tb, LANES), lambda i: (i, 0)),
            pl.BlockSpec((8, LANES), lambda i: (0, 0)),
        ),
        scratch_shapes=[pltpu.VMEM((8, LANES), F32)],
        compiler_params=pltpu.CompilerParams(
            dimension_semantics=("arbitrary",), vmem_limit_bytes=VMEM_LIMIT),
        name="moe_router",
    )(h2d, g.reshape(1, d), rw)


def _row_copy(src_ref, src_row, dst_ref, dst_row, sem):
    return pltpu.make_async_copy(src_ref.at[pl.ds(src_row, 1)], dst_ref.at[pl.ds(dst_row, 1)], sem)


def _dispatch_kernel(off_ref, pad_lo_ref, pad_hi_ref, na_ref, hn_ref, ri_ref, xs_ref, zblk, sem):
    tb = ROUTE_TILE
    tm = EXPERT_TILE

    @pl.when(pl.program_id(0) == 0)
    def _():
        zblk[...] = jnp.zeros(zblk.shape, F32)
        n_blocks = xs_ref.shape[0] // tm

        def fill(do_start):
            def pad_row(r, c):
                cp = _row_copy(zblk, 0, xs_ref, r, sem)
                cp.start() if do_start else cp.wait()
                return c

            def tail_blk(b, c):
                cp = pltpu.make_async_copy(
                    zblk, xs_ref.at[pl.ds(pl.multiple_of(b * tm, tm), tm)], sem)
                cp.start() if do_start else cp.wait()
                return c

            for e in range(N_EXPERTS):
                lax.fori_loop(pad_lo_ref[e], pad_hi_ref[e], pad_row, 0)
            lax.fori_loop(na_ref[0], n_blocks, tail_blk, 0)

        fill(True)
        fill(False)

    def start(t, c):
        for slot in range(2):
            dst = off_ref[ri_ref[0, slot, t]] + ri_ref[0, 2 + slot, t]
            _row_copy(hn_ref, t, xs_ref, dst, sem).start()
        return c

    lax.fori_loop(0, tb, start, 0, unroll=8)

    def wait(t, c):
        _row_copy(hn_ref, 0, xs_ref, 0, sem).wait()
        _row_copy(hn_ref, 0, xs_ref, 0, sem).wait()
        return c

    lax.fori_loop(0, tb, wait, 0, unroll=8)


def _dispatch(hn, ri, off, pad_lo, pad_hi, n_active, n_rows):
    t, d = hn.shape
    tb = ROUTE_TILE
    grid_spec = pltpu.PrefetchScalarGridSpec(
        num_scalar_prefetch=4,
        grid=(t // tb,),
        in_specs=[
            pl.BlockSpec((tb, d), lambda i, *_: (i, 0)),
            pl.BlockSpec((1, ROUTE_ROWS, tb), lambda i, *_: (i, 0, 0), memory_space=pltpu.SMEM),
        ],
        out_specs=pl.BlockSpec(memory_space=pl.ANY),
        scratch_shapes=[pltpu.VMEM((EXPERT_TILE, d), F32), pltpu.SemaphoreType.DMA(())],
    )
    return pl.pallas_call(
        _dispatch_kernel,
        out_shape=jax.ShapeDtypeStruct((n_rows, d), F32),
        grid_spec=grid_spec,
        compiler_params=pltpu.CompilerParams(
            dimension_semantics=("arbitrary",), has_side_effects=True),
        name="moe_dispatch",
    )(off, pad_lo, pad_hi, n_active, hn, ri)


def _expert_kernel(be_ref, na_ref, x_ref, w1_ref, w3_ref, w2_ref, y_ref):
    i = pl.program_id(0)
    f = pl.program_id(1)

    @pl.when(i < na_ref[0])
    def _():
        x = x_ref[...].astype(BF16)
        t = _silu(jnp.dot(x, w1_ref[0], preferred_element_type=F32))
        t = t * jnp.dot(x, w3_ref[0], preferred_element_type=F32)
        part = jnp.dot(t.astype(BF16), w2_ref[0], preferred_element_type=F32)

        @pl.when(f == 0)
        def _():
            y_ref[...] = part

        @pl.when(f > 0)
        def _():
            y_ref[...] += part

    @pl.when((i >= na_ref[0]) & (f == 0))
    def _():
        y_ref[...] = jnp.zeros(y_ref.shape, F32)


def _experts(xs, blk_expert, n_active, w1, w3, w2, *, tf):
    r, d = xs.shape
    tm = EXPERT_TILE
    ff = w1.shape[2]
    nf = ff // tf

    def row_blk(i, f, be, na):
        return (jnp.minimum(i, na[0] - 1), 0)

    def f_blk(i, f, na):
        return jnp.where(i < na[0], f, nf - 1)

    grid_spec = pltpu.PrefetchScalarGridSpec(
        num_scalar_prefetch=2,
        grid=(r // tm, nf),
        in_specs=[
            pl.BlockSpec((tm, d), row_blk),
            pl.BlockSpec((1, d, tf), lambda i, f, be, na: (be[i], 0, f_blk(i, f, na))),
            pl.BlockSpec((1, d, tf), lambda i, f, be, na: (be[i], 0, f_blk(i, f, na))),
            pl.BlockSpec((1, tf, d), lambda i, f, be, na: (be[i], f_blk(i, f, na), 0)),
        ],
        out_specs=pl.BlockSpec((tm, d), lambda i, f, be, na: (i, 0)),
    )
    return pl.pallas_call(
        _expert_kernel,
        out_shape=jax.ShapeDtypeStruct((r, d), F32),
        grid_spec=grid_spec,
        compiler_params=pltpu.CompilerParams(
            dimension_semantics=("arbitrary", "arbitrary"),
            vmem_limit_bytes=VMEM_LIMIT),
        name="moe_experts",
    )(blk_expert, n_active, xs, w1.astype(BF16), w3.astype(BF16), w2.astype(BF16))


def _combine_kernel(off_ref, h_ref, ri_ref, gcol_ref, y_ref, o_ref, ybuf, sem):
    tb = ROUTE_TILE

    def start(t, c):
        for slot in range(2):
            src = off_ref[ri_ref[0, slot, t]] + ri_ref[0, 2 + slot, t]
            _row_copy(y_ref, src, ybuf.at[slot], t, sem).start()
        return c

    lax.fori_loop(0, tb, start, 0, unroll=8)

    def wait(t, c):
        _row_copy(y_ref, 0, ybuf.at[0], 0, sem).wait()
        _row_copy(y_ref, 0, ybuf.at[0], 0, sem).wait()
        return c

    lax.fori_loop(0, tb, wait, 0, unroll=8)

    g = gcol_ref[...]
    o_ref[...] = h_ref[...] + (g[:, 0:1] * ybuf[0] + g[:, 1:2] * ybuf[1])


def _combine(h2d, ri, gcol, y, off):
    t, d = h2d.shape
    tb = ROUTE_TILE
    grid_spec = pltpu.PrefetchScalarGridSpec(
        num_scalar_prefetch=1,
        grid=(t // tb,),
        in_specs=[
            pl.BlockSpec((tb, d), lambda i, off: (i, 0)),
            pl.BlockSpec((1, ROUTE_ROWS, tb), lambda i, off: (i, 0, 0), memory_space=pltpu.SMEM),
            pl.BlockSpec((tb, LANES), lambda i, off: (i, 0)),
            pl.BlockSpec(memory_space=pl.ANY),
        ],
        out_specs=pl.BlockSpec((tb, d), lambda i, off: (i, 0)),
        scratch_shapes=[pltpu.VMEM((2, tb, d), F32), pltpu.SemaphoreType.DMA(())],
    )
    return pl.pallas_call(
        _combine_kernel,
        out_shape=jax.ShapeDtypeStruct((t, d), F32),
        grid_spec=grid_spec,
        compiler_params=pltpu.CompilerParams(
            dimension_semantics=("arbitrary",), vmem_limit_bytes=VMEM_LIMIT),
        name="moe_combine",
    )(off, h2d, ri, gcol, y)


def _moe(h2d, g, router_w, w1, w3, w2):
    t, d = h2d.shape
    tm = EXPERT_TILE
    n_rows = 2 * t + N_EXPERTS * tm
    hn, ri, gcol, cnt = _router(h2d, g, router_w)
    counts = cnt[0, :N_EXPERTS].astype(jnp.int32)
    blocks = (counts + tm - 1) // tm
    ends = jnp.cumsum(blocks)
    off = (ends - blocks) * tm
    n_active = ends[-1:]
    blk = jnp.arange(n_rows // tm, dtype=jnp.int32)
    blk_expert = jnp.sum(jnp.minimum(blk, n_active - 1)[:, None] >= ends[None, :], axis=1)
    blk_expert = blk_expert.astype(jnp.int32)
    xs = _dispatch(hn, ri, off, off + counts, off + blocks * tm, n_active, n_rows)
    y = _experts(xs, blk_expert, n_active, w1, w3, w2, tf=1792)
    return _combine(h2d, ri, gcol, y, off)


def _final_norm_kernel(x_ref, g_ref, o_ref):
    o_ref[...] = _rms(x_ref[...], g_ref[...])


def _final_norm(h2d, g):
    t, d = h2d.shape
    tm = 1024
    return pl.pallas_call(
        _final_norm_kernel,
        out_shape=jax.ShapeDtypeStruct((t, d), F32),
        grid=(t // tm,),
        in_specs=[pl.BlockSpec((tm, d), lambda i: (i, 0)),
                  pl.BlockSpec((1, d), lambda i: (0, 0))],
        out_specs=pl.BlockSpec((tm, d), lambda i: (i, 0)),
        name="final_norm",
    )(h2d, g.reshape(1, d))


def kernel(x, e_norm1, e_w_in, e_conv_w, e_conv_b, e_ln_g, e_ln_b, e_pool_w, e_pool_scale,
           e_w_out, e_norm2, e_ff_w1, e_ff_w3, e_ff_w2, o_norm1, o_w_qkv, o_b_qkv, o_sinks,
           o_w_o, o_b_o, o_norm2, o_router, o_exp_w1, o_exp_w3, o_exp_w2, final_norm):
    b, s, d = x.shape
    h = x
    for layer in range(DEPTH):
        i = layer // 2
        if layer % 2 == 0:
            h = _even_mixer(h, e_norm1[i], e_w_in[i], e_conv_w[i], e_conv_b[i], e_ln_g[i],
                            e_ln_b[i], e_pool_w[i], e_pool_scale[i], e_w_out[i])
            h = _ffn(h.reshape(b * s, d), e_norm2[i], e_ff_w1[i], e_ff_w3[i], e_ff_w2[i],
                     tm=512, tf=1408).reshape(b, s, d)
        else:
            h = _attn_mixer(h, o_norm1[i], o_w_qkv[i], o_b_qkv[i], o_sinks[i], o_w_o[i],
                            o_b_o[i])
            h = _moe(h.reshape(b * s, d), o_norm2[i], o_router[i], o_exp_w1[i], o_exp_w3[i],
                     o_exp_w2[i]).reshape(b, s, d)
    return _final_norm(h.reshape(b * s, d), final_norm).reshape(b, s, d)
```

```python
import functools

import jax
import jax.numpy as jnp
from jax import lax
from jax.experimental import pallas as pl
from jax.experimental.pallas import tpu as pltpu

F32 = jnp.float32
BF16 = jnp.bfloat16

D_MODEL = 1024
DEPTH = 4
RMS_EPS = 1e-5
LN_EPS = 1e-5

D_CONV = 512
D_POOL = 512
CONV_K = 31
POOL_WINDOWS = (2, 4, 8, 16)
POOL_GROUP = 128
D_IN_EVEN = 2 * D_CONV + D_POOL

HEAD_DIM = 64
N_Q_HEADS = 16
N_KV_HEADS = 4
ATTN_BLOCK = 128
D_Q = N_Q_HEADS * HEAD_DIM
D_KV = N_KV_HEADS * HEAD_DIM
D_QKV = D_Q + 2 * D_KV

N_EXPERTS = 8

LANES = 128
SEQ_TILE = 512
HALO = 32
CONV_ROWS = 32
VMEM_LIMIT = 56 * 1024 * 1024


def _rms(x, g):
    ms = jnp.mean(x * x, axis=-1, keepdims=True)
    return x * lax.rsqrt(ms + RMS_EPS) * g


def _silu(x):
    return x * jax.nn.sigmoid(x)


def _even_mixer_kernel(x_ref, g_ref, win_ref, cw_ref, cb_ref, lg_ref, lb_ref,
                       pw_ref, ps_ref, wout_ref, o_ref,
                       abuf, bbuf, p1, p2, p3, cat):
    s = pl.program_id(1)
    ts = SEQ_TILE
    rows = HALO + ts

    @pl.when(s == 0)
    def _():
        abuf[0:HALO, :] = jnp.zeros((HALO, D_CONV), F32)
        bbuf[0:HALO, :] = jnp.zeros((HALO, D_POOL), F32)

    x = x_ref[0]
    hn = _rms(x, g_ref[...]).astype(BF16)
    u = jnp.dot(hn, win_ref[...], preferred_element_type=F32)
    abuf[HALO:rows, :] = u[:, :D_CONV] * jax.nn.sigmoid(u[:, D_CONV:2 * D_CONV])
    bbuf[HALO:rows, :] = u[:, 2 * D_CONV:]

    for c in range(ts // CONV_ROWS):
        r0 = c * CONV_ROWS
        acc = jnp.broadcast_to(cb_ref[...], (CONV_ROWS, D_CONV))
        for k in range(CONV_K):
            start = r0 + HALO - (CONV_K - 1) + k
            acc = acc + cw_ref[k:k + 1, :] * abuf[start:start + CONV_ROWS, :]
        mu = jnp.mean(acc, axis=-1, keepdims=True)
        d = acc - mu
        var = jnp.mean(d * d, axis=-1, keepdims=True)
        y = d * lax.rsqrt(var + LN_EPS) * lg_ref[...] + lb_ref[...]
        cat[r0:r0 + CONV_ROWS, 0:D_CONV] = _silu(y).astype(BF16)

    p1[8:rows, :] = bbuf[8:rows, :] + bbuf[7:rows - 1, :]
    p2[16:rows, 0:384] = p1[16:rows, 128:512] + p1[14:rows - 2, 128:512]
    p3[24:rows, 0:256] = p2[24:rows, 128:384] + p2[20:rows - 4, 128:384]
    s16 = p3[HALO:rows, 128:256] + p3[HALO - 8:rows - 8, 128:256]
    sums = (p1[HALO:rows, 0:128], p2[HALO:rows, 0:128], p3[HALO:rows, 0:128], s16)
    pos1 = (s * ts + 1 + lax.broadcasted_iota(jnp.int32, (ts, 1), 0)).astype(F32)
    for g, w in enumerate(POOL_WINDOWS):
        cnt = jnp.minimum(pos1, float(w))
        pg = sums[g] / cnt - bbuf[HALO:rows, g * POOL_GROUP:(g + 1) * POOL_GROUP]
        pm = jnp.dot(pg.astype(BF16), pw_ref[g], preferred_element_type=F32)
        pm = pm * ps_ref[:, g * POOL_GROUP:(g + 1) * POOL_GROUP]
        cat[:, D_CONV + g * POOL_GROUP:D_CONV + (g + 1) * POOL_GROUP] = pm.astype(BF16)

    o_ref[0] = x + jnp.dot(cat[...], wout_ref[...], preferred_element_type=F32)

    abuf[0:HALO, :] = abuf[ts:rows, :]
    bbuf[0:HALO, :] = bbuf[ts:rows, :]


def _even_mixer(h, g, w_in, conv_w, conv_b, ln_g, ln_b, pool_w, pool_scale, w_out):
    b, s, d = h.shape
    ts = SEQ_TILE
    rows = HALO + ts
    const = lambda shape: pl.BlockSpec(shape, lambda i, j: (0,) * len(shape))
    return pl.pallas_call(
        _even_mixer_kernel,
        out_shape=jax.ShapeDtypeStruct(h.shape, F32),
        grid=(b, s // ts),
        in_specs=[
            pl.BlockSpec((1, ts, d), lambda i, j: (i, j, 0)),
            const((1, d)),
            const((d, D_IN_EVEN)),
            const((CONV_K, D_CONV)),
            const((1, D_CONV)),
            const((1, D_CONV)),
            const((1, D_CONV)),
            const((len(POOL_WINDOWS), POOL_GROUP, POOL_GROUP)),
            const((1, D_POOL)),
            const((d, d)),
        ],
        out_specs=pl.BlockSpec((1, ts, d), lambda i, j: (i, j, 0)),
        scratch_shapes=[
            pltpu.VMEM((rows, D_CONV), F32),
            pltpu.VMEM((rows, D_POOL), F32),
            pltpu.VMEM((rows, D_POOL), F32),
            pltpu.VMEM((rows, 384), F32),
            pltpu.VMEM((rows, 256), F32),
            pltpu.VMEM((ts, d), BF16),
        ],
        compiler_params=pltpu.CompilerParams(
            dimension_semantics=("arbitrary", "arbitrary"),
            vmem_limit_bytes=VMEM_LIMIT),
        name="even_mixer",
    )(h, g.reshape(1, d), w_in.astype(BF16), conv_w, conv_b.reshape(1, -1),
      ln_g.reshape(1, -1), ln_b.reshape(1, -1), pool_w.astype(BF16),
      pool_scale.reshape(1, -1), w_out.astype(BF16))


def _attn_kernel(sink_ref, x_ref, g_ref, wqkv_ref, bqkv_ref, wot_ref, bo_ref, o_ref,
                 kbuf, vtbuf, otbuf):
    s = pl.program_id(1)
    ts = SEQ_TILE
    blk = ATTN_BLOCK
    group = N_Q_HEADS // N_KV_HEADS

    @pl.when(s == 0)
    def _():
        kbuf[0:blk, :] = jnp.zeros((blk, N_KV_HEADS * LANES), BF16)
        vtbuf[:, 0:blk] = jnp.zeros((D_KV, blk), BF16)

    x = x_ref[0]
    hn = _rms(x, g_ref[...]).astype(BF16)
    u = jnp.dot(hn, wqkv_ref[...], preferred_element_type=F32) + bqkv_ref[...]
    q = (u[:, :D_Q] * (HEAD_DIM ** -0.5)).astype(BF16)

    lo_t = lax.broadcasted_iota(jnp.int32, (ts, LANES), 1) < HEAD_DIM
    for p in range(N_KV_HEADS // 2):
        t = u[:, D_Q + p * LANES:D_Q + (p + 1) * LANES]
        r = pltpu.roll(t, HEAD_DIM, axis=1)
        kbuf[blk:blk + ts, (2 * p) * LANES:(2 * p + 1) * LANES] = jnp.where(lo_t, t, r).astype(BF16)
        kbuf[blk:blk + ts, (2 * p + 1) * LANES:(2 * p + 2) * LANES] = jnp.where(lo_t, r, t).astype(BF16)
    vtbuf[:, blk:blk + ts] = jnp.transpose(u[:, D_Q + D_KV:]).astype(BF16)

    lo_b = lax.broadcasted_iota(jnp.int32, (blk, LANES), 1) < HEAD_DIM
    zero_b = jnp.zeros((blk, LANES), BF16)
    key = lax.broadcasted_iota(jnp.int32, (2 * blk, blk), 0)
    qry = lax.broadcasted_iota(jnp.int32, (2 * blk, blk), 1)
    band = (key > qry) & (key <= qry + blk)
    band0 = band & ((key >= blk) | (s > 0))
    neg = jnp.concatenate([jnp.where(band, 0.0, -jnp.inf)] * group, axis=1)
    neg0 = jnp.concatenate([jnp.where(band0, 0.0, -jnp.inf)] * group, axis=1)

    for j in range(ts // blk):
        mask_add = neg0 if j == 0 else neg
        for g in range(N_KV_HEADS):
            kk = kbuf[j * blk:(j + 2) * blk, g * LANES:(g + 1) * LANES]
            vt = vtbuf[g * HEAD_DIM:(g + 1) * HEAD_DIM, j * blk:(j + 2) * blk]
            qs = []
            sk = []
            for i in range(group):
                h = g * group + i
                qt = q[j * blk:(j + 1) * blk, (h // 2) * LANES:(h // 2 + 1) * LANES]
                qs.append(jnp.where(lo_b, qt, zero_b) if h % 2 == 0 else jnp.where(lo_b, zero_b, qt))
                sk.append(jnp.full((1, blk), sink_ref[h], F32))
            q4 = jnp.concatenate(qs, axis=0)
            sink = jnp.concatenate(sk, axis=1)
            st = lax.dot_general(kk, q4, (((1,), (1,)), ((), ())),
                                 preferred_element_type=F32) + mask_add
            m = jnp.maximum(jnp.max(st, axis=0, keepdims=True), sink)
            e = jnp.exp(st - m)
            denom = jnp.sum(e, axis=0, keepdims=True) + jnp.exp(sink - m)
            ot = jnp.dot(vt, e.astype(BF16), preferred_element_type=F32) * (1.0 / denom)
            for i in range(group):
                h = g * group + i
                otbuf[h * HEAD_DIM:(h + 1) * HEAD_DIM, j * blk:(j + 1) * blk] = (
                    ot[:, i * blk:(i + 1) * blk].astype(BF16))

    proj_t = jnp.dot(wot_ref[...], otbuf[...], preferred_element_type=F32)
    o_ref[0] = x + jnp.transpose(proj_t) + bo_ref[...]

    kbuf[0:blk, :] = kbuf[ts:ts + blk, :]
    vtbuf[:, 0:blk] = vtbuf[:, ts:ts + blk]


def _attn_mixer(h, g, w_qkv, b_qkv, sinks, w_o, b_o):
    b, s, d = h.shape
    ts = SEQ_TILE
    const = lambda shape: pl.BlockSpec(shape, lambda i, j, sk: (0,) * len(shape))
    grid_spec = pltpu.PrefetchScalarGridSpec(
        num_scalar_prefetch=1,
        grid=(b, s // ts),
        in_specs=[
            pl.BlockSpec((1, ts, d), lambda i, j, sk: (i, j, 0)),
            const((1, d)),
            const((d, D_QKV)),
            const((1, D_QKV)),
            const((d, D_Q)),
            const((1, d)),
        ],
        out_specs=pl.BlockSpec((1, ts, d), lambda i, j, sk: (i, j, 0)),
        scratch_shapes=[
            pltpu.VMEM((ATTN_BLOCK + ts, N_KV_HEADS * LANES), BF16),
            pltpu.VMEM((D_KV, ATTN_BLOCK + ts), BF16),
            pltpu.VMEM((D_Q, ts), BF16),
        ],
    )
    return pl.pallas_call(
        _attn_kernel,
        out_shape=jax.ShapeDtypeStruct(h.shape, F32),
        grid_spec=grid_spec,
        compiler_params=pltpu.CompilerParams(
            dimension_semantics=("arbitrary", "arbitrary"),
            vmem_limit_bytes=VMEM_LIMIT),
        name="attn_mixer",
    )(sinks, h, g.reshape(1, d), w_qkv.astype(BF16), b_qkv.reshape(1, -1),
      jnp.transpose(w_o).astype(BF16), b_o.reshape(1, d))


def _ffn_kernel(x_ref, g_ref, w1_ref, w3_ref, w2_ref, o_ref, hn_s, acc_s):
    f = pl.program_id(1)

    @pl.when(f == 0)
    def _():
        x = x_ref[...]
        hn_s[...] = _rms(x, g_ref[...]).astype(BF16)
        acc_s[...] = x

    hn = hn_s[...]
    t = _silu(jnp.dot(hn, w1_ref[...], preferred_element_type=F32))
    t = t * jnp.dot(hn, w3_ref[...], preferred_element_type=F32)
    acc_s[...] += jnp.dot(t.astype(BF16), w2_ref[...], preferred_element_type=F32)

    @pl.when(f == pl.num_programs(1) - 1)
    def _():
        o_ref[...] = acc_s[...]


def _ffn(h2d, g, w1, w3, w2, *, tm, tf):
    t, d = h2d.shape
    ff = w1.shape[1]
    return pl.pallas_call(
        _ffn_kernel,
        out_shape=jax.ShapeDtypeStruct((t, d), F32),
        grid=(t // tm, ff // tf),
        in_specs=[
            pl.BlockSpec((tm, d), lambda i, f: (i, 0)),
            pl.BlockSpec((1, d), lambda i, f: (0, 0)),
            pl.BlockSpec((d, tf), lambda i, f: (0, f)),
            pl.BlockSpec((d, tf), lambda i, f: (0, f)),
            pl.BlockSpec((tf, d), lambda i, f: (f, 0)),
        ],
        out_specs=pl.BlockSpec((tm, d), lambda i, f: (i, 0)),
        scratch_shapes=[
            pltpu.VMEM((tm, d), BF16),
            pltpu.VMEM((tm, d), F32),
        ],
        compiler_params=pltpu.CompilerParams(
            dimension_semantics=("arbitrary", "arbitrary"),
            vmem_limit_bytes=VMEM_LIMIT),
        name="dense_ffn",
    )(h2d, g.reshape(1, d), w1.astype(BF16), w3.astype(BF16), w2.astype(BF16))


ROUTE_TILE = 512
EXPERT_TILE = 512
ROUTE_ROWS = 8


def _router_kernel(x_ref, g_ref, rw_ref, hn_ref, ri_ref, gcol_ref, cnt_ref, carry):
    i = pl.program_id(0)
    tb = ROUTE_TILE

    @pl.when(i == 0)
    def _():
        carry[...] = jnp.zeros(carry.shape, F32)

    hn = _rms(x_ref[...], g_ref[...])
    hn_ref[...] = hn
    logits = jnp.dot(hn, rw_ref[...], preferred_element_type=F32,
                     precision=lax.Precision.HIGHEST)
    lane = lax.broadcasted_iota(jnp.int32, (tb, LANES), 1)
    lg = jnp.where(lane < N_EXPERTS, logits, -jnp.inf)
    m1 = jnp.max(lg, axis=-1, keepdims=True)
    i1 = jnp.min(jnp.where(lg == m1, lane, LANES), axis=-1, keepdims=True)
    lg2 = jnp.where(lane == i1, -jnp.inf, lg)
    m2 = jnp.max(lg2, axis=-1, keepdims=True)
    i2 = jnp.min(jnp.where(lg2 == m2, lane, LANES), axis=-1, keepdims=True)
    e2 = jnp.exp(m2 - m1)
    den = 1.0 + e2
    gcol_ref[...] = jnp.where(lane == 0, 1.0 / den, jnp.where(lane == 1, e2 / den, 0.0))

    member = (lane == i1) | (lane == i2)
    mem_f = jnp.where(member, 1.0, 0.0)
    rr = lax.broadcasted_iota(jnp.int32, (tb, tb), 0)
    cc = lax.broadcasted_iota(jnp.int32, (tb, tb), 1)
    lower = jnp.where(cc < rr, 1.0, 0.0).astype(BF16)
    before = jnp.dot(lower, mem_f.astype(BF16), preferred_element_type=F32) + carry[0:1, :]
    r1 = jnp.sum(jnp.where(lane == i1, before, 0.0), axis=-1, keepdims=True)
    r2 = jnp.sum(jnp.where(lane == i2, before, 0.0), axis=-1, keepdims=True)
    new_carry = carry[0:1, :] + jnp.sum(mem_f, axis=0, keepdims=True)
    carry[...] = jnp.broadcast_to(new_carry, carry.shape)
    cnt_ref[...] = jnp.broadcast_to(new_carry, cnt_ref.shape)

    table = jnp.where(lane == 0, i1.astype(F32),
                      jnp.where(lane == 1, i2.astype(F32),
                                jnp.where(lane == 2, r1, jnp.where(lane == 3, r2, 0.0))))
    ri_ref[0] = jnp.transpose(table)[0:ROUTE_ROWS, :].astype(jnp.int32)


def _router(h2d, g, router_w):
    t, d = h2d.shape
    tb = ROUTE_TILE
    rw = jnp.zeros((d, LANES), F32).at[:, :N_EXPERTS].set(router_w)
    return pl.pallas_call(
        _router_kernel,
        out_shape=(
            jax.ShapeDtypeStruct((t, d), F32),
            jax.ShapeDtypeStruct((t // tb, ROUTE_ROWS, tb), jnp.int32),
            jax.ShapeDtypeStruct((t, LANES), F32),
            jax.ShapeDtypeStruct((8, LANES), F32),
        ),
        grid=(t // tb,),
        in_specs=[
            pl.BlockSpec((tb, d), lambda i: (i, 0)),
            pl.BlockSpec((1, d), lambda i: (0, 0)),
            pl.BlockSpec((d, LANES), lambda i: (0, 0)),
        ],
        out_specs=(
            pl.BlockSpec((tb, d), lambda i: (i, 0)),
            pl.BlockSpec((1, ROUTE_ROWS, tb), lambda i: (i, 0, 0)),
            pl.BlockSpec((tb, LANES), lambda i: (i, 0)),
            pl.BlockSpec((8, LANES), lambda i: (0, 0)),
        ),
        scratch_shapes=[pltpu.VMEM((8, LANES), F32)],
        compiler_params=pltpu.CompilerParams(
            dimension_semantics=("arbitrary",), vmem_limit_bytes=VMEM_LIMIT),
        name="moe_router",
    )(h2d, g.reshape(1, d), rw)


def _row_copy(src_ref, src_row, dst_ref, dst_row, sem):
    return pltpu.make_async_copy(src_ref.at[pl.ds(src_row, 1)], dst_ref.at[pl.ds(dst_row, 1)], sem)


def _dispatch_kernel(off_ref, pad_lo_ref, pad_hi_ref, na_ref, hn_ref, ri_ref, xs_ref, zblk, sem):
    tb = ROUTE_TILE
    tm = EXPERT_TILE

    @pl.when(pl.program_id(0) == 0)
    def _():
        zblk[...] = jnp.zeros(zblk.shape, F32)
        n_blocks = xs_ref.shape[0] // tm

        def fill(do_start):
            def pad_row(r, c):
                cp = _row_copy(zblk, 0, xs_ref, r, sem)
                cp.start() if do_start else cp.wait()
                return c

            def tail_blk(b, c):
                cp = pltpu.make_async_copy(
                    zblk, xs_ref.at[pl.ds(pl.multiple_of(b * tm, tm), tm)], sem)
                cp.start() if do_start else cp.wait()
                return c

            for e in range(N_EXPERTS):
                lax.fori_loop(pad_lo_ref[e], pad_hi_ref[e], pad_row, 0)
            lax.fori_loop(na_ref[0], n_blocks, tail_blk, 0)

        fill(True)
        fill(False)

    def start(t, c):
        for slot in range(2):
            dst = off_ref[ri_ref[0, slot, t]] + ri_ref[0, 2 + slot, t]
            _row_copy(hn_ref, t, xs_ref, dst, sem).start()
        return c

    lax.fori_loop(0, tb, start, 0, unroll=8)

    def wait(t, c):
        _row_copy(hn_ref, 0, xs_ref, 0, sem).wait()
        _row_copy(hn_ref, 0, xs_ref, 0, sem).wait()
        return c

    lax.fori_loop(0, tb, wait, 0, unroll=8)


def _dispatch(hn, ri, off, pad_lo, pad_hi, n_active, n_rows):
    t, d = hn.shape
    tb = ROUTE_TILE
    grid_spec = pltpu.PrefetchScalarGridSpec(
        num_scalar_prefetch=4,
        grid=(t // tb,),
        in_specs=[
            pl.BlockSpec((tb, d), lambda i, *_: (i, 0)),
            pl.BlockSpec((1, ROUTE_ROWS, tb), lambda i, *_: (i, 0, 0), memory_space=pltpu.SMEM),
        ],
        out_specs=pl.BlockSpec(memory_space=pl.ANY),
        scratch_shapes=[pltpu.VMEM((EXPERT_TILE, d), F32), pltpu.SemaphoreType.DMA(())],
    )
    return pl.pallas_call(
        _dispatch_kernel,
        out_shape=jax.ShapeDtypeStruct((n_rows, d), F32),
        grid_spec=grid_spec,
        compiler_params=pltpu.CompilerParams(
            dimension_semantics=("arbitrary",), has_side_effects=True),
        name="moe_dispatch",
    )(off, pad_lo, pad_hi, n_active, hn, ri)


def _expert_kernel(be_ref, na_ref, x_ref, w1_ref, w3_ref, w2_ref, y_ref):
    i = pl.program_id(0)
    f = pl.program_id(1)

    @pl.when(i < na_ref[0])
    def _():
        x = x_ref[...].astype(BF16)
        t = _silu(jnp.dot(x, w1_ref[0], preferred_element_type=F32))
        t = t * jnp.dot(x, w3_ref[0], preferred_element_type=F32)
        part = jnp.dot(t.astype(BF16), w2_ref[0], preferred_element_type=F32)

        @pl.when(f == 0)
        def _():
            y_ref[...] = part

        @pl.when(f > 0)
        def _():
            y_ref[...] += part

    @pl.when((i >= na_ref[0]) & (f == 0))
    def _():
        y_ref[...] = jnp.zeros(y_ref.shape, F32)


def _experts(xs, blk_expert, n_active, w1, w3, w2, *, tf):
    r, d = xs.shape
    tm = EXPERT_TILE
    ff = w1.shape[2]
    nf = ff // tf

    def row_blk(i, f, be, na):
        return (jnp.maximum(jnp.minimum(i, na[0] - 1), 0), 0)

    def f_blk(i, f, na):
        return jnp.where(i < na[0], f, nf - 1)

    grid_spec = pltpu.PrefetchScalarGridSpec(
        num_scalar_prefetch=2,
        grid=(r // tm, nf),
        in_specs=[
            pl.BlockSpec((tm, d), row_blk),
            pl.BlockSpec((1, d, tf), lambda i, f, be, na: (be[i], 0, f_blk(i, f, na))),
            pl.BlockSpec((1, d, tf), lambda i, f, be, na: (be[i], 0, f_blk(i, f, na))),
            pl.BlockSpec((1, tf, d), lambda i, f, be, na: (be[i], f_blk(i, f, na), 0)),
        ],
        out_specs=pl.BlockSpec((tm, d), lambda i, f, be, na: (i, 0)),
    )
    return pl.pallas_call(
        _expert_kernel,
        out_shape=jax.ShapeDtypeStruct((r, d), F32),
        grid_spec=grid_spec,
        compiler_params=pltpu.CompilerParams(
            dimension_semantics=("arbitrary", "arbitrary"),
            vmem_limit_bytes=VMEM_LIMIT),
        name="moe_experts",
    )(blk_expert, n_active, xs, w1.astype(BF16), w3.astype(BF16), w2.astype(BF16))


def _combine_kernel(off_ref, h_ref, ri_ref, gcol_ref, y_ref, o_ref, ybuf, sem):
    tb = ROUTE_TILE

    def start(t, c):
        for slot in range(2):
            src = off_ref[ri_ref[0, slot, t]] + ri_ref[0, 2 + slot, t]
            _row_copy(y_ref, src, ybuf.at[slot], t, sem).start()
        return c

    lax.fori_loop(0, tb, start, 0, unroll=8)

    def wait(t, c):
        _row_copy(y_ref, 0, ybuf.at[0], 0, sem).wait()
        _row_copy(y_ref, 0, ybuf.at[0], 0, sem).wait()
        return c

    lax.fori_loop(0, tb, wait, 0, unroll=8)

    g = gcol_ref[...]
    o_ref[...] = h_ref[...] + (g[:, 0:1] * ybuf[0] + g[:, 1:2] * ybuf[1])


def _combine(h2d, ri, gcol, y, off):
    t, d = h2d.shape
    tb = ROUTE_TILE
    grid_spec = pltpu.PrefetchScalarGridSpec(
        num_scalar_prefetch=1,
        grid=(t // tb,),
        in_specs=[
            pl.BlockSpec((tb, d), lambda i, off: (i, 0)),
            pl.BlockSpec((1, ROUTE_ROWS, tb), lambda i, off: (i, 0, 0), memory_space=pltpu.SMEM),
            pl.BlockSpec((tb, LANES), lambda i, off: (i, 0)),
            pl.BlockSpec(memory_space=pl.ANY),
        ],
        out_specs=pl.BlockSpec((tb, d), lambda i, off: (i, 0)),
        scratch_shapes=[pltpu.VMEM((2, tb, d), F32), pltpu.SemaphoreType.DMA(())],
    )
    return pl.pallas_call(
        _combine_kernel,
        out_shape=jax.ShapeDtypeStruct((t, d), F32),
        grid_spec=grid_spec,
        compiler_params=pltpu.CompilerParams(
            dimension_semantics=("arbitrary",), vmem_limit_bytes=VMEM_LIMIT),
        name="moe_combine",
    )(off, h2d, ri, gcol, y)


def _moe(h2d, g, router_w, w1, w3, w2):
    t, d = h2d.shape
    tm = EXPERT_TILE
    n_rows = 2 * t + N_EXPERTS * tm
    hn, ri, gcol, cnt = _router(h2d, g, router_w)
    counts = cnt[0, :N_EXPERTS].astype(jnp.int32)
    blocks = (counts + tm - 1) // tm
    ends = jnp.cumsum(blocks)
    off = (ends - blocks) * tm
    n_active = ends[-1:]
    blk = jnp.arange(n_rows // tm, dtype=jnp.int32)
    blk_expert = jnp.sum(jnp.minimum(blk, n_active - 1)[:, None] >= ends[None, :], axis=1)
    blk_expert = blk_expert.astype(jnp.int32)
    xs = _dispatch(hn, ri, off, off + counts, off + blocks * tm, n_active, n_rows)
    y = _experts(xs, blk_expert, n_active, w1, w3, w2, tf=1792)
    return _combine(h2d, ri, gcol, y, off)


def _final_norm_kernel(x_ref, g_ref, o_ref):
    o_ref[...] = _rms(x_ref[...], g_ref[...])


def _final_norm(h2d, g):
    t, d = h2d.shape
    tm = 1024
    return pl.pallas_call(
        _final_norm_kernel,
        out_shape=jax.ShapeDtypeStruct((t, d), F32),
        grid=(t // tm,),
        in_specs=[pl.BlockSpec((tm, d), lambda i: (i, 0)),
                  pl.BlockSpec((1, d), lambda i: (0, 0))],
        out_specs=pl.BlockSpec((tm, d), lambda i: (i, 0)),
        name="final_norm",
    )(h2d, g.reshape(1, d))


def kernel(x, e_norm1, e_w_in, e_conv_w, e_conv_b, e_ln_g, e_ln_b, e_pool_w, e_pool_scale,
           e_w_out, e_norm2, e_ff_w1, e_ff_w3, e_ff_w2, o_norm1, o_w_qkv, o_b_qkv, o_sinks,
           o_w_o, o_b_o, o_norm2, o_router, o_exp_w1, o_exp_w3, o_exp_w2, final_norm):
    b, s, d = x.shape
    h = x
    for layer in range(DEPTH):
        i = layer // 2
        if layer % 2 == 0:
            h = _even_mixer(h, e_norm1[i], e_w_in[i], e_conv_w[i], e_conv_b[i], e_ln_g[i],
                            e_ln_b[i], e_pool_w[i], e_pool_scale[i], e_w_out[i])
            h = _ffn(h.reshape(b * s, d), e_norm2[i], e_ff_w1[i], e_ff_w3[i], e_ff_w2[i],
                     tm=512, tf=1408).reshape(b, s, d)
        else:
            h = _attn_mixer(h, o_norm1[i], o_w_qkv[i], o_b_qkv[i], o_sinks[i], o_w_o[i],
                            o_b_o[i])
            h = _moe(h.reshape(b * s, d), o_norm2[i], o_router[i], o_exp_w1[i], o_exp_w3[i],
                     o_exp_w2[i]).reshape(b, s, d)
    return _final_norm(h.reshape(b * s, d), final_norm).reshape(b, s, d)
```

```python
import functools

import jax
import jax.numpy as jnp
from jax import lax
from jax.experimental import pallas as pl
from jax.experimental.pallas import tpu as pltpu

F32 = jnp.float32
BF16 = jnp.bfloat16

D_MODEL = 1024
DEPTH = 4
RMS_EPS = 1e-5
LN_EPS = 1e-5

D_CONV = 512
D_POOL = 512
CONV_K = 31
POOL_WINDOWS = (2, 4, 8, 16)
POOL_GROUP = 128
D_IN_EVEN = 2 * D_CONV + D_POOL

HEAD_DIM = 64
N_Q_HEADS = 16
N_KV_HEADS = 4
ATTN_BLOCK = 128
D_Q = N_Q_HEADS * HEAD_DIM
D_KV = N_KV_HEADS * HEAD_DIM
D_QKV = D_Q + 2 * D_KV

N_EXPERTS = 8

LANES = 128
SUBLANES = 8
SEQ_TILE = 512
HALO = 32
CONV_ROWS = 32
VMEM_LIMIT = 56 * 1024 * 1024


def _rms(x, g):
    ms = jnp.mean(x * x, axis=-1, keepdims=True)
    return x * lax.rsqrt(ms + RMS_EPS) * g


def _silu(x):
    return x * jax.nn.sigmoid(x)


def _even_mixer_kernel(x_ref, g_ref, win_ref, cw_ref, cb_ref, lg_ref, lb_ref,
                       pw_ref, ps_ref, wout_ref, o_ref,
                       abuf, ashift, cwb, bbuf, p1, p2, p3, cat):
    s = pl.program_id(1)
    ts = SEQ_TILE
    rows = HALO + ts

    @pl.when(s == 0)
    def _():
        abuf[0:HALO, :] = jnp.zeros((HALO, D_CONV), F32)
        bbuf[0:HALO, :] = jnp.zeros((HALO, D_POOL), F32)

    x = x_ref[0]
    hn = _rms(x, g_ref[...]).astype(BF16)
    u = jnp.dot(hn, win_ref[...], preferred_element_type=F32)
    abuf[HALO:rows, :] = u[:, :D_CONV] * jax.nn.sigmoid(u[:, D_CONV:2 * D_CONV])
    bbuf[HALO:rows, :] = u[:, 2 * D_CONV:]

    a_all = abuf[...]
    for r in range(1, SUBLANES):
        ashift[r - 1] = pltpu.roll(a_all, rows - r, axis=0)
    for k in range(CONV_K):
        cwb[k] = jnp.broadcast_to(cw_ref[k:k + 1, :], (SUBLANES, D_CONV))
    for c in range(ts // CONV_ROWS):
        r0 = c * CONV_ROWS
        acc = jnp.broadcast_to(cb_ref[...], (CONV_ROWS, D_CONV))
        for k in range(CONV_K):
            q8, r = divmod(HALO - (CONV_K - 1) + k, SUBLANES)
            src = abuf if r == 0 else ashift.at[r - 1]
            start = r0 + q8 * SUBLANES
            tap = jnp.concatenate([cwb[k]] * (CONV_ROWS // SUBLANES), axis=0)
            acc = acc + tap * src[start:start + CONV_ROWS, :]
        mu = jnp.mean(acc, axis=-1, keepdims=True)
        d = acc - mu
        var = jnp.mean(d * d, axis=-1, keepdims=True)
        y = d * lax.rsqrt(var + LN_EPS) * lg_ref[...] + lb_ref[...]
        cat[r0:r0 + CONV_ROWS, 0:D_CONV] = _silu(y).astype(BF16)

    p1[8:rows, :] = bbuf[8:rows, :] + bbuf[7:rows - 1, :]
    p2[16:rows, 0:384] = p1[16:rows, 128:512] + p1[14:rows - 2, 128:512]
    p3[24:rows, 0:256] = p2[24:rows, 128:384] + p2[20:rows - 4, 128:384]
    s16 = p3[HALO:rows, 128:256] + p3[HALO - 8:rows - 8, 128:256]
    sums = (p1[HALO:rows, 0:128], p2[HALO:rows, 0:128], p3[HALO:rows, 0:128], s16)
    pos1 = (s * ts + 1 + lax.broadcasted_iota(jnp.int32, (ts, 1), 0)).astype(F32)
    for g, w in enumerate(POOL_WINDOWS):
        cnt = jnp.minimum(pos1, float(w))
        pg = sums[g] / cnt - bbuf[HALO:rows, g * POOL_GROUP:(g + 1) * POOL_GROUP]
        pm = jnp.dot(pg.astype(BF16), pw_ref[g], preferred_element_type=F32)
        pm = pm * ps_ref[:, g * POOL_GROUP:(g + 1) * POOL_GROUP]
        cat[:, D_CONV + g * POOL_GROUP:D_CONV + (g + 1) * POOL_GROUP] = pm.astype(BF16)

    o_ref[0] = x + jnp.dot(cat[...], wout_ref[...], preferred_element_type=F32)

    abuf[0:HALO, :] = abuf[ts:rows, :]
    bbuf[0:HALO, :] = bbuf[ts:rows, :]


def _even_mixer(h, g, w_in, conv_w, conv_b, ln_g, ln_b, pool_w, pool_scale, w_out):
    b, s, d = h.shape
    ts = SEQ_TILE
    rows = HALO + ts
    const = lambda shape: pl.BlockSpec(shape, lambda i, j: (0,) * len(shape))
    return pl.pallas_call(
        _even_mixer_kernel,
        out_shape=jax.ShapeDtypeStruct(h.shape, F32),
        grid=(b, s // ts),
        in_specs=[
            pl.BlockSpec((1, ts, d), lambda i, j: (i, j, 0)),
            const((1, d)),
            const((d, D_IN_EVEN)),
            const((CONV_K, D_CONV)),
            const((1, D_CONV)),
            const((1, D_CONV)),
            const((1, D_CONV)),
            const((len(POOL_WINDOWS), POOL_GROUP, POOL_GROUP)),
            const((1, D_POOL)),
            const((d, d)),
        ],
        out_specs=pl.BlockSpec((1, ts, d), lambda i, j: (i, j, 0)),
        scratch_shapes=[
            pltpu.VMEM((rows, D_CONV), F32),
            pltpu.VMEM((SUBLANES - 1, rows, D_CONV), F32),
            pltpu.VMEM((CONV_K, SUBLANES, D_CONV), F32),
            pltpu.VMEM((rows, D_POOL), F32),
            pltpu.VMEM((rows, D_POOL), F32),
            pltpu.VMEM((rows, 384), F32),
            pltpu.VMEM((rows, 256), F32),
            pltpu.VMEM((ts, d), BF16),
        ],
        compiler_params=pltpu.CompilerParams(
            dimension_semantics=("arbitrary", "arbitrary"),
            vmem_limit_bytes=VMEM_LIMIT),
        name="even_mixer",
    )(h, g.reshape(1, d), w_in.astype(BF16), conv_w, conv_b.reshape(1, -1),
      ln_g.reshape(1, -1), ln_b.reshape(1, -1), pool_w.astype(BF16),
      pool_scale.reshape(1, -1), w_out.astype(BF16))


def _attn_kernel(sink_ref, x_ref, g_ref, wqkv_ref, bqkv_ref, wot_ref, bo_ref, o_ref,
                 kbuf, vtbuf, otbuf):
    s = pl.program_id(1)
    ts = SEQ_TILE
    blk = ATTN_BLOCK
    group = N_Q_HEADS // N_KV_HEADS

    @pl.when(s == 0)
    def _():
        kbuf[0:blk, :] = jnp.zeros((blk, N_KV_HEADS * LANES), BF16)
        vtbuf[:, 0:blk] = jnp.zeros((D_KV, blk), BF16)

    x = x_ref[0]
    hn = _rms(x, g_ref[...]).astype(BF16)
    u = jnp.dot(hn, wqkv_ref[...], preferred_element_type=F32) + bqkv_ref[...]
    q = (u[:, :D_Q] * (HEAD_DIM ** -0.5)).astype(BF16)

    lo_t = lax.broadcasted_iota(jnp.int32, (ts, LANES), 1) < HEAD_DIM
    for p in range(N_KV_HEADS // 2):
        t = u[:, D_Q + p * LANES:D_Q + (p + 1) * LANES]
        r = pltpu.roll(t, HEAD_DIM, axis=1)
        kbuf[blk:blk + ts, (2 * p) * LANES:(2 * p + 1) * LANES] = jnp.where(lo_t, t, r).astype(BF16)
        kbuf[blk:blk + ts, (2 * p + 1) * LANES:(2 * p + 2) * LANES] = jnp.where(lo_t, r, t).astype(BF16)
    vtbuf[:, blk:blk + ts] = jnp.transpose(u[:, D_Q + D_KV:]).astype(BF16)

    lo_b = lax.broadcasted_iota(jnp.int32, (blk, LANES), 1) < HEAD_DIM
    zero_b = jnp.zeros((blk, LANES), BF16)
    key = lax.broadcasted_iota(jnp.int32, (2 * blk, blk), 0)
    qry = lax.broadcasted_iota(jnp.int32, (2 * blk, blk), 1)
    band = (key > qry) & (key <= qry + blk)
    band0 = band & ((key >= blk) | (s > 0))
    neg = jnp.concatenate([jnp.where(band, 0.0, -jnp.inf)] * group, axis=1)
    neg0 = jnp.concatenate([jnp.where(band0, 0.0, -jnp.inf)] * group, axis=1)

    for j in range(ts // blk):
        mask_add = neg0 if j == 0 else neg
        for g in range(N_KV_HEADS):
            kk = kbuf[j * blk:(j + 2) * blk, g * LANES:(g + 1) * LANES]
            vt = vtbuf[g * HEAD_DIM:(g + 1) * HEAD_DIM, j * blk:(j + 2) * blk]
            qs = []
            sk = []
            for i in range(group):
                h = g * group + i
                qt = q[j * blk:(j + 1) * blk, (h // 2) * LANES:(h // 2 + 1) * LANES]
                qs.append(jnp.where(lo_b, qt, zero_b) if h % 2 == 0 else jnp.where(lo_b, zero_b, qt))
                sk.append(jnp.full((1, blk), sink_ref[h], F32))
            q4 = jnp.concatenate(qs, axis=0)
            sink = jnp.concatenate(sk, axis=1)
            st = lax.dot_general(kk, q4, (((1,), (1,)), ((), ())),
                                 preferred_element_type=F32) + mask_add
            m = jnp.maximum(jnp.max(st, axis=0, keepdims=True), sink)
            e = jnp.exp(st - m)
            denom = jnp.sum(e, axis=0, keepdims=True) + jnp.exp(sink - m)
            ot = jnp.dot(vt, e.astype(BF16), preferred_element_type=F32) * (1.0 / denom)
            for i in range(group):
                h = g * group + i
                otbuf[h * HEAD_DIM:(h + 1) * HEAD_DIM, j * blk:(j + 1) * blk] = (
                    ot[:, i * blk:(i + 1) * blk].astype(BF16))

    proj_t = jnp.dot(wot_ref[...], otbuf[...], preferred_element_type=F32)
    o_ref[0] = x + jnp.transpose(proj_t) + bo_ref[...]

    kbuf[0:blk, :] = kbuf[ts:ts + blk, :]
    vtbuf[:, 0:blk] = vtbuf[:, ts:ts + blk]


def _attn_mixer(h, g, w_qkv, b_qkv, sinks, w_o, b_o):
    b, s, d = h.shape
    ts = SEQ_TILE
    const = lambda shape: pl.BlockSpec(shape, lambda i, j, sk: (0,) * len(shape))
    grid_spec = pltpu.PrefetchScalarGridSpec(
        num_scalar_prefetch=1,
        grid=(b, s // ts),
        in_specs=[
            pl.BlockSpec((1, ts, d), lambda i, j, sk: (i, j, 0)),
            const((1, d)),
            const((d, D_QKV)),
            const((1, D_QKV)),
            const((d, D_Q)),
            const((1, d)),
        ],
        out_specs=pl.BlockSpec((1, ts, d), lambda i, j, sk: (i, j, 0)),
        scratch_shapes=[
            pltpu.VMEM((ATTN_BLOCK + ts, N_KV_HEADS * LANES), BF16),
            pltpu.VMEM((D_KV, ATTN_BLOCK + ts), BF16),
            pltpu.VMEM((D_Q, ts), BF16),
        ],
    )
    return pl.pallas_call(
        _attn_kernel,
        out_shape=jax.ShapeDtypeStruct(h.shape, F32),
        grid_spec=grid_spec,
        compiler_params=pltpu.CompilerParams(
            dimension_semantics=("arbitrary", "arbitrary"),
            vmem_limit_bytes=VMEM_LIMIT),
        name="attn_mixer",
    )(sinks, h, g.reshape(1, d), w_qkv.astype(BF16), b_qkv.reshape(1, -1),
      jnp.transpose(w_o).astype(BF16), b_o.reshape(1, d))


def _ffn_kernel(x_ref, g_ref, w1_ref, w3_ref, w2_ref, o_ref, hn_s, acc_s):
    f = pl.program_id(1)

    @pl.when(f == 0)
    def _():
        x = x_ref[...]
        hn_s[...] = _rms(x, g_ref[...]).astype(BF16)
        acc_s[...] = x

    hn = hn_s[...]
    t = _silu(jnp.dot(hn, w1_ref[...], preferred_element_type=F32))
    t = t * jnp.dot(hn, w3_ref[...], preferred_element_type=F32)
    acc_s[...] += jnp.dot(t.astype(BF16), w2_ref[...], preferred_element_type=F32)

    @pl.when(f == pl.num_programs(1) - 1)
    def _():
        o_ref[...] = acc_s[...]


def _ffn(h2d, g, w1, w3, w2, *, tm, tf):
    t, d = h2d.shape
    ff = w1.shape[1]
    return pl.pallas_call(
        _ffn_kernel,
        out_shape=jax.ShapeDtypeStruct((t, d), F32),
        grid=(t // tm, ff // tf),
        in_specs=[
            pl.BlockSpec((tm, d), lambda i, f: (i, 0)),
            pl.BlockSpec((1, d), lambda i, f: (0, 0)),
            pl.BlockSpec((d, tf), lambda i, f: (0, f)),
            pl.BlockSpec((d, tf), lambda i, f: (0, f)),
            pl.BlockSpec((tf, d), lambda i, f: (f, 0)),
        ],
        out_specs=pl.BlockSpec((tm, d), lambda i, f: (i, 0)),
        scratch_shapes=[
            pltpu.VMEM((tm, d), BF16),
            pltpu.VMEM((tm, d), F32),
        ],
        compiler_params=pltpu.CompilerParams(
            dimension_semantics=("arbitrary", "arbitrary"),
            vmem_limit_bytes=VMEM_LIMIT),
        name="dense_ffn",
    )(h2d, g.reshape(1, d), w1.astype(BF16), w3.astype(BF16), w2.astype(BF16))


ROUTE_TILE = 512
EXPERT_TILE = 512
ROUTE_ROWS = 8


def _router_kernel(x_ref, g_ref, rw_ref, hn_ref, ri_ref, gcol_ref, cnt_ref, carry):
    i = pl.program_id(0)
    tb = ROUTE_TILE

    @pl.when(i == 0)
    def _():
        carry[...] = jnp.zeros(carry.shape, F32)

    hn = _rms(x_ref[...], g_ref[...])
    hn_ref[...] = hn
    logits = jnp.dot(hn, rw_ref[...], preferred_element_type=F32,
                     precision=lax.Precision.HIGHEST)
    lane = lax.broadcasted_iota(jnp.int32, (tb, LANES), 1)
    lg = jnp.where(lane < N_EXPERTS, logits, -jnp.inf)
    m1 = jnp.max(lg, axis=-1, keepdims=True)
    i1 = jnp.min(jnp.where(lg == m1, lane, LANES), axis=-1, keepdims=True)
    lg2 = jnp.where(lane == i1, -jnp.inf, lg)
    m2 = jnp.max(lg2, axis=-1, keepdims=True)
    i2 = jnp.min(jnp.where(lg2 == m2, lane, LANES), axis=-1, keepdims=True)
    e2 = jnp.exp(m2 - m1)
    den = 1.0 + e2
    gcol_ref[...] = jnp.where(lane == 0, 1.0 / den, jnp.where(lane == 1, e2 / den, 0.0))

    member = (lane == i1) | (lane == i2)
    mem_f = jnp.where(member, 1.0, 0.0)
    rr = lax.broadcasted_iota(jnp.int32, (tb, tb), 0)
    cc = lax.broadcasted_iota(jnp.int32, (tb, tb), 1)
    lower = jnp.where(cc < rr, 1.0, 0.0).astype(BF16)
    before = jnp.dot(lower, mem_f.astype(BF16), preferred_element_type=F32) + carry[0:1, :]
    r1 = jnp.sum(jnp.where(lane == i1, before, 0.0), axis=-1, keepdims=True)
    r2 = jnp.sum(jnp.where(lane == i2, before, 0.0), axis=-1, keepdims=True)
    new_carry = carry[0:1, :] + jnp.sum(mem_f, axis=0, keepdims=True)
    carry[...] = jnp.broadcast_to(new_carry, carry.shape)
    cnt_ref[...] = jnp.broadcast_to(new_carry, cnt_ref.shape)

    table = jnp.where(lane == 0, i1.astype(F32),
                      jnp.where(lane == 1, i2.astype(F32),
                                jnp.where(lane == 2, r1, jnp.where(lane == 3, r2, 0.0))))
    ri_ref[0] = jnp.transpose(table)[0:ROUTE_ROWS, :].astype(jnp.int32)


def _router(h2d, g, router_w):
    t, d = h2d.shape
    tb = ROUTE_TILE
    rw = jnp.zeros((d, LANES), F32).at[:, :N_EXPERTS].set(router_w)
    return pl.pallas_call(
        _router_kernel,
        out_shape=(
            jax.ShapeDtypeStruct((t, d), F32),
            jax.ShapeDtypeStruct((t // tb, ROUTE_ROWS, tb), jnp.int32),
            jax.ShapeDtypeStruct((t, LANES), F32),
            jax.ShapeDtypeStruct((8, LANES), F32),
        ),
        grid=(t // tb,),
        in_specs=[
            pl.BlockSpec((tb, d), lambda i: (i, 0)),
            pl.BlockSpec((1, d), lambda i: (0, 0)),
            pl.BlockSpec((d, LANES), lambda i: (0, 0)),
        ],
        out_specs=(
            pl.BlockSpec((tb, d), lambda i: (i, 0)),
            pl.BlockSpec((1, ROUTE_ROWS, tb), lambda i: (i, 0, 0)),
            pl.BlockSpec((tb, LANES), lambda i: (i, 0)),
            pl.BlockSpec((8, LANES), lambda i: (0, 0)),
        ),
        scratch_shapes=[pltpu.VMEM((8, LANES), F32)],
        compiler_params=pltpu.CompilerParams(
            dimension_semantics=("arbitrary",), vmem_limit_bytes=VMEM_LIMIT),
        name="moe_router",
    )(h2d, g.reshape(1, d), rw)


def _row_copy(src_ref, src_row, dst_ref, dst_row, sem):
    return pltpu.make_async_copy(src_ref.at[pl.ds(src_row, 1)], dst_ref.at[pl.ds(dst_row, 1)], sem)


def _dispatch_kernel(off_ref, pad_lo_ref, pad_hi_ref, na_ref, hn_ref, ri_ref, xs_ref, zblk, sem):
    tb = ROUTE_TILE
    tm = EXPERT_TILE

    @pl.when(pl.program_id(0) == 0)
    def _():
        zblk[...] = jnp.zeros(zblk.shape, F32)
        n_blocks = xs_ref.shape[0] // tm

        def fill(do_start):
            def pad_row(r, c):
                cp = _row_copy(zblk, 0, xs_ref, r, sem)
                cp.start() if do_start else cp.wait()
                return c

            def tail_blk(b, c):
                cp = pltpu.make_async_copy(
                    zblk, xs_ref.at[pl.ds(pl.multiple_of(b * tm, tm), tm)], sem)
                cp.start() if do_start else cp.wait()
                return c

            for e in range(N_EXPERTS):
                lax.fori_loop(pad_lo_ref[e], pad_hi_ref[e], pad_row, 0)
            lax.fori_loop(na_ref[0], n_blocks, tail_blk, 0)

        fill(True)
        fill(False)

    def start(t, c):
        for slot in range(2):
            dst = off_ref[ri_ref[0, slot, t]] + ri_ref[0, 2 + slot, t]
            _row_copy(hn_ref, t, xs_ref, dst, sem).start()
        return c

    lax.fori_loop(0, tb, start, 0, unroll=8)

    def wait(t, c):
        _row_copy(hn_ref, 0, xs_ref, 0, sem).wait()
        _row_copy(hn_ref, 0, xs_ref, 0, sem).wait()
        return c

    lax.fori_loop(0, tb, wait, 0, unroll=8)


def _dispatch(hn, ri, off, pad_lo, pad_hi, n_active, n_rows):
    t, d = hn.shape
    tb = ROUTE_TILE
    grid_spec = pltpu.PrefetchScalarGridSpec(
        num_scalar_prefetch=4,
        grid=(t // tb,),
        in_specs=[
            pl.BlockSpec((tb, d), lambda i, *_: (i, 0)),
            pl.BlockSpec((1, ROUTE_ROWS, tb), lambda i, *_: (i, 0, 0), memory_space=pltpu.SMEM),
        ],
        out_specs=pl.BlockSpec(memory_space=pl.ANY),
        scratch_shapes=[pltpu.VMEM((EXPERT_TILE, d), F32), pltpu.SemaphoreType.DMA(())],
    )
    return pl.pallas_call(
        _dispatch_kernel,
        out_shape=jax.ShapeDtypeStruct((n_rows, d), F32),
        grid_spec=grid_spec,
        compiler_params=pltpu.CompilerParams(
            dimension_semantics=("arbitrary",), has_side_effects=True),
        name="moe_dispatch",
    )(off, pad_lo, pad_hi, n_active, hn, ri)


def _expert_kernel(be_ref, na_ref, x_ref, w1_ref, w3_ref, w2_ref, y_ref):
    i = pl.program_id(0)
    f = pl.program_id(1)

    @pl.when(i < na_ref[0])
    def _():
        x = x_ref[...].astype(BF16)
        t = _silu(jnp.dot(x, w1_ref[0], preferred_element_type=F32))
        t = t * jnp.dot(x, w3_ref[0], preferred_element_type=F32)
        part = jnp.dot(t.astype(BF16), w2_ref[0], preferred_element_type=F32)

        @pl.when(f == 0)
        def _():
            y_ref[...] = part

        @pl.when(f > 0)
        def _():
            y_ref[...] += part

    @pl.when((i >= na_ref[0]) & (f == 0))
    def _():
        y_ref[...] = jnp.zeros(y_ref.shape, F32)


def _experts(xs, blk_expert, n_active, w1, w3, w2, *, tf):
    r, d = xs.shape
    tm = EXPERT_TILE
    ff = w1.shape[2]
    nf = ff // tf

    def row_blk(i, f, be, na):
        return (jnp.maximum(jnp.minimum(i, na[0] - 1), 0), 0)

    def f_blk(i, f, na):
        return jnp.where(i < na[0], f, nf - 1)

    grid_spec = pltpu.PrefetchScalarGridSpec(
        num_scalar_prefetch=2,
        grid=(r // tm, nf),
        in_specs=[
            pl.BlockSpec((tm, d), row_blk),
            pl.BlockSpec((1, d, tf), lambda i, f, be, na: (be[i], 0, f_blk(i, f, na))),
            pl.BlockSpec((1, d, tf), lambda i, f, be, na: (be[i], 0, f_blk(i, f, na))),
            pl.BlockSpec((1, tf, d), lambda i, f, be, na: (be[i], f_blk(i, f, na), 0)),
        ],
        out_specs=pl.BlockSpec((tm, d), lambda i, f, be, na: (i, 0)),
    )
    return pl.pallas_call(
        _expert_kernel,
        out_shape=jax.ShapeDtypeStruct((r, d), F32),
        grid_spec=grid_spec,
        compiler_params=pltpu.CompilerParams(
            dimension_semantics=("arbitrary", "arbitrary"),
            vmem_limit_bytes=VMEM_LIMIT),
        name="moe_experts",
    )(blk_expert, n_active, xs, w1, w3, w2)


def _combine_kernel(off_ref, h_ref, ri_ref, gcol_ref, y_ref, o_ref, ybuf, sem):
    tb = ROUTE_TILE

    def start(t, c):
        for slot in range(2):
            src = off_ref[ri_ref[0, slot, t]] + ri_ref[0, 2 + slot, t]
            _row_copy(y_ref, src, ybuf.at[slot], t, sem).start()
        return c

    lax.fori_loop(0, tb, start, 0, unroll=8)

    def wait(t, c):
        _row_copy(y_ref, 0, ybuf.at[0], 0, sem).wait()
        _row_copy(y_ref, 0, ybuf.at[0], 0, sem).wait()
        return c

    lax.fori_loop(0, tb, wait, 0, unroll=8)

    g = gcol_ref[...]
    o_ref[...] = h_ref[...] + (g[:, 0:1] * ybuf[0] + g[:, 1:2] * ybuf[1])


def _combine(h2d, ri, gcol, y, off):
    t, d = h2d.shape
    tb = ROUTE_TILE
    grid_spec = pltpu.PrefetchScalarGridSpec(
        num_scalar_prefetch=1,
        grid=(t // tb,),
        in_specs=[
            pl.BlockSpec((tb, d), lambda i, off: (i, 0)),
            pl.BlockSpec((1, ROUTE_ROWS, tb), lambda i, off: (i, 0, 0), memory_space=pltpu.SMEM),
            pl.BlockSpec((tb, LANES), lambda i, off: (i, 0)),
            pl.BlockSpec(memory_space=pl.ANY),
        ],
        out_specs=pl.BlockSpec((tb, d), lambda i, off: (i, 0)),
        scratch_shapes=[pltpu.VMEM((2, tb, d), F32), pltpu.SemaphoreType.DMA(())],
    )
    return pl.pallas_call(
        _combine_kernel,
        out_shape=jax.ShapeDtypeStruct((t, d), F32),
        grid_spec=grid_spec,
        compiler_params=pltpu.CompilerParams(
            dimension_semantics=("arbitrary",), vmem_limit_bytes=VMEM_LIMIT),
        name="moe_combine",
    )(off, h2d, ri, gcol, y)


def _moe(h2d, g, router_w, w1, w3, w2, layer):
    t, d = h2d.shape
    tm = EXPERT_TILE
    n_rows = 2 * t + N_EXPERTS * tm
    hn, ri, gcol, cnt = _router(h2d, g, router_w)
    counts = cnt[0, :N_EXPERTS].astype(jnp.int32)
    blocks = (counts + tm - 1) // tm
    ends = jnp.cumsum(blocks)
    off = (ends - blocks) * tm
    n_active = ends[-1:]
    blk = jnp.arange(n_rows // tm, dtype=jnp.int32)
    blk_expert = jnp.sum(jnp.minimum(blk, n_active - 1)[:, None] >= ends[None, :], axis=1)
    blk_expert = blk_expert.astype(jnp.int32) + layer * N_EXPERTS
    xs = _dispatch(hn, ri, off, off + counts, off + blocks * tm, n_active, n_rows)
    y = _experts(xs, blk_expert, n_active, w1, w3, w2, tf=1792)
    return _combine(h2d, ri, gcol, y, off)


def _final_norm_kernel(x_ref, g_ref, o_ref):
    o_ref[...] = _rms(x_ref[...], g_ref[...])


def _final_norm(h2d, g):
    t, d = h2d.shape
    tm = 1024
    return pl.pallas_call(
        _final_norm_kernel,
        out_shape=jax.ShapeDtypeStruct((t, d), F32),
        grid=(t // tm,),
        in_specs=[pl.BlockSpec((tm, d), lambda i: (i, 0)),
                  pl.BlockSpec((1, d), lambda i: (0, 0))],
        out_specs=pl.BlockSpec((tm, d), lambda i: (i, 0)),
        name="final_norm",
    )(h2d, g.reshape(1, d))


def kernel(x, e_norm1, e_w_in, e_conv_w, e_conv_b, e_ln_g, e_ln_b, e_pool_w, e_pool_scale,
           e_w_out, e_norm2, e_ff_w1, e_ff_w3, e_ff_w2, o_norm1, o_w_qkv, o_b_qkv, o_sinks,
           o_w_o, o_b_o, o_norm2, o_router, o_exp_w1, o_exp_w3, o_exp_w2, final_norm):
    b, s, d = x.shape
    expert_w = [w.astype(BF16).reshape((-1,) + w.shape[2:]) for w in (o_exp_w1, o_exp_w3, o_exp_w2)]
    h = x
    for layer in range(DEPTH):
        i = layer // 2
        if layer % 2 == 0:
            h = _even_mixer(h, e_norm1[i], e_w_in[i], e_conv_w[i], e_conv_b[i], e_ln_g[i],
                            e_ln_b[i], e_pool_w[i], e_pool_scale[i], e_w_out[i])
            h = _ffn(h.reshape(b * s, d), e_norm2[i], e_ff_w1[i], e_ff_w3[i], e_ff_w2[i],
                     tm=512, tf=1408).reshape(b, s, d)
        else:
            h = _attn_mixer(h, o_norm1[i], o_w_qkv[i], o_b_qkv[i], o_sinks[i], o_w_o[i],
                            o_b_o[i])
            h = _moe(h.reshape(b * s, d), o_norm2[i], o_router[i], *expert_w, i).reshape(b, s, d)
    return _final_norm(h.reshape(b * s, d), final_norm).reshape(b, s, d)
```

```python
import functools

import jax
import jax.numpy as jnp
from jax import lax
from jax.experimental import pallas as pl
from jax.experimental.pallas import tpu as pltpu

F32 = jnp.float32
BF16 = jnp.bfloat16

D_MODEL = 1024
DEPTH = 4
RMS_EPS = 1e-5
LN_EPS = 1e-5

D_CONV = 512
D_POOL = 512
CONV_K = 31
POOL_WINDOWS = (2, 4, 8, 16)
POOL_GROUP = 128
D_IN_EVEN = 2 * D_CONV + D_POOL

HEAD_DIM = 64
N_Q_HEADS = 16
N_KV_HEADS = 4
ATTN_BLOCK = 128
D_Q = N_Q_HEADS * HEAD_DIM
D_KV = N_KV_HEADS * HEAD_DIM
D_QKV = D_Q + 2 * D_KV

N_EXPERTS = 8

LANES = 128
SUBLANES = 8
SEQ_TILE = 512
HALO = 32
CONV_ROWS = 32
VMEM_LIMIT = 56 * 1024 * 1024


def _rms(x, g):
    ms = jnp.mean(x * x, axis=-1, keepdims=True)
    return x * lax.rsqrt(ms + RMS_EPS) * g


def _silu(x):
    return x * jax.nn.sigmoid(x)


def _even_mixer_kernel(x_ref, g_ref, win_ref, cw_ref, cb_ref, lg_ref, lb_ref,
                       pw_ref, ps_ref, wout_ref, o_ref,
                       abuf, ashift, cwb, bbuf, p1, p2, p3, cat):
    s = pl.program_id(1)
    ts = SEQ_TILE
    rows = HALO + ts

    @pl.when(s == 0)
    def _():
        abuf[0:HALO, :] = jnp.zeros((HALO, D_CONV), F32)
        bbuf[0:HALO, :] = jnp.zeros((HALO, D_POOL), F32)

    x = x_ref[0]
    hn = _rms(x, g_ref[...]).astype(BF16)
    u = jnp.dot(hn, win_ref[...], preferred_element_type=F32)
    abuf[HALO:rows, :] = u[:, :D_CONV] * jax.nn.sigmoid(u[:, D_CONV:2 * D_CONV])
    bbuf[HALO:rows, :] = u[:, 2 * D_CONV:]

    a_all = abuf[...]
    for r in range(1, SUBLANES):
        ashift[r - 1] = pltpu.roll(a_all, rows - r, axis=0)
    for k in range(CONV_K):
        cwb[k] = jnp.broadcast_to(cw_ref[k:k + 1, :], (SUBLANES, D_CONV))
    for c in range(ts // CONV_ROWS):
        r0 = c * CONV_ROWS
        acc = jnp.broadcast_to(cb_ref[...], (CONV_ROWS, D_CONV))
        for k in range(CONV_K):
            q8, r = divmod(HALO - (CONV_K - 1) + k, SUBLANES)
            src = abuf if r == 0 else ashift.at[r - 1]
            start = r0 + q8 * SUBLANES
            tap = jnp.concatenate([cwb[k]] * (CONV_ROWS // SUBLANES), axis=0)
            acc = acc + tap * src[start:start + CONV_ROWS, :]
        mu = jnp.mean(acc, axis=-1, keepdims=True)
        d = acc - mu
        var = jnp.mean(d * d, axis=-1, keepdims=True)
        y = d * lax.rsqrt(var + LN_EPS) * lg_ref[...] + lb_ref[...]
        cat[r0:r0 + CONV_ROWS, 0:D_CONV] = _silu(y).astype(BF16)

    p1[8:rows, :] = bbuf[8:rows, :] + bbuf[7:rows - 1, :]
    p2[16:rows, 0:384] = p1[16:rows, 128:512] + p1[14:rows - 2, 128:512]
    p3[24:rows, 0:256] = p2[24:rows, 128:384] + p2[20:rows - 4, 128:384]
    s16 = p3[HALO:rows, 128:256] + p3[HALO - 8:rows - 8, 128:256]
    sums = (p1[HALO:rows, 0:128], p2[HALO:rows, 0:128], p3[HALO:rows, 0:128], s16)
    pos1 = (s * ts + 1 + lax.broadcasted_iota(jnp.int32, (ts, 1), 0)).astype(F32)
    for g, w in enumerate(POOL_WINDOWS):
        cnt = jnp.minimum(pos1, float(w))
        pg = sums[g] / cnt - bbuf[HALO:rows, g * POOL_GROUP:(g + 1) * POOL_GROUP]
        pm = jnp.dot(pg.astype(BF16), pw_ref[g], preferred_element_type=F32)
        pm = pm * ps_ref[:, g * POOL_GROUP:(g + 1) * POOL_GROUP]
        cat[:, D_CONV + g * POOL_GROUP:D_CONV + (g + 1) * POOL_GROUP] = pm.astype(BF16)

    o_ref[0] = x + jnp.dot(cat[...], wout_ref[...], preferred_element_type=F32)

    abuf[0:HALO, :] = abuf[ts:rows, :]
    bbuf[0:HALO, :] = bbuf[ts:rows, :]


def _even_mixer(h, g, w_in, conv_w, conv_b, ln_g, ln_b, pool_w, pool_scale, w_out):
    b, s, d = h.shape
    ts = SEQ_TILE
    rows = HALO + ts
    const = lambda shape: pl.BlockSpec(shape, lambda i, j: (0,) * len(shape))
    return pl.pallas_call(
        _even_mixer_kernel,
        out_shape=jax.ShapeDtypeStruct(h.shape, F32),
        grid=(b, s // ts),
        in_specs=[
            pl.BlockSpec((1, ts, d), lambda i, j: (i, j, 0)),
            const((1, d)),
            const((d, D_IN_EVEN)),
            const((CONV_K, D_CONV)),
            const((1, D_CONV)),
            const((1, D_CONV)),
            const((1, D_CONV)),
            const((len(POOL_WINDOWS), POOL_GROUP, POOL_GROUP)),
            const((1, D_POOL)),
            const((d, d)),
        ],
        out_specs=pl.BlockSpec((1, ts, d), lambda i, j: (i, j, 0)),
        scratch_shapes=[
            pltpu.VMEM((rows, D_CONV), F32),
            pltpu.VMEM((SUBLANES - 1, rows, D_CONV), F32),
            pltpu.VMEM((CONV_K, SUBLANES, D_CONV), F32),
            pltpu.VMEM((rows, D_POOL), F32),
            pltpu.VMEM((rows, D_POOL), F32),
            pltpu.VMEM((rows, 384), F32),
            pltpu.VMEM((rows, 256), F32),
            pltpu.VMEM((ts, d), BF16),
        ],
        compiler_params=pltpu.CompilerParams(
            dimension_semantics=("arbitrary", "arbitrary"),
            vmem_limit_bytes=VMEM_LIMIT),
        name="even_mixer",
    )(h, g.reshape(1, d), w_in.astype(BF16), conv_w, conv_b.reshape(1, -1),
      ln_g.reshape(1, -1), ln_b.reshape(1, -1), pool_w.astype(BF16),
      pool_scale.reshape(1, -1), w_out.astype(BF16))


def _attn_kernel(sink_ref, x_ref, g_ref, wqkv_ref, bqkv_ref, wot_ref, bo_ref, o_ref,
                 kbuf, vtbuf, otbuf):
    s = pl.program_id(1)
    ts = SEQ_TILE
    blk = ATTN_BLOCK
    group = N_Q_HEADS // N_KV_HEADS

    @pl.when(s == 0)
    def _():
        kbuf[0:blk, :] = jnp.zeros((blk, N_KV_HEADS * LANES), BF16)
        vtbuf[:, 0:blk] = jnp.zeros((D_KV, blk), BF16)

    x = x_ref[0]
    hn = _rms(x, g_ref[...]).astype(BF16)
    u = jnp.dot(hn, wqkv_ref[...], preferred_element_type=F32) + bqkv_ref[...]
    q = (u[:, :D_Q] * (HEAD_DIM ** -0.5)).astype(BF16)

    lo_t = lax.broadcasted_iota(jnp.int32, (ts, LANES), 1) < HEAD_DIM
    for p in range(N_KV_HEADS // 2):
        t = u[:, D_Q + p * LANES:D_Q + (p + 1) * LANES]
        r = pltpu.roll(t, HEAD_DIM, axis=1)
        kbuf[blk:blk + ts, (2 * p) * LANES:(2 * p + 1) * LANES] = jnp.where(lo_t, t, r).astype(BF16)
        kbuf[blk:blk + ts, (2 * p + 1) * LANES:(2 * p + 2) * LANES] = jnp.where(lo_t, r, t).astype(BF16)
    vtbuf[:, blk:blk + ts] = jnp.transpose(u[:, D_Q + D_KV:]).astype(BF16)

    lo_b = lax.broadcasted_iota(jnp.int32, (blk, LANES), 1) < HEAD_DIM
    zero_b = jnp.zeros((blk, LANES), BF16)
    key = lax.broadcasted_iota(jnp.int32, (2 * blk, blk), 0)
    qry = lax.broadcasted_iota(jnp.int32, (2 * blk, blk), 1)
    band = (key > qry) & (key <= qry + blk)
    band0 = band & ((key >= blk) | (s > 0))
    neg = jnp.concatenate([jnp.where(band, 0.0, -jnp.inf)] * group, axis=1)
    neg0 = jnp.concatenate([jnp.where(band0, 0.0, -jnp.inf)] * group, axis=1)

    for j in range(ts // blk):
        mask_add = neg0 if j == 0 else neg
        for g in range(N_KV_HEADS):
            kk = kbuf[j * blk:(j + 2) * blk, g * LANES:(g + 1) * LANES]
            vt = vtbuf[g * HEAD_DIM:(g + 1) * HEAD_DIM, j * blk:(j + 2) * blk]
            qs = []
            sk = []
            for i in range(group):
                h = g * group + i
                qt = q[j * blk:(j + 1) * blk, (h // 2) * LANES:(h // 2 + 1) * LANES]
                qs.append(jnp.where(lo_b, qt, zero_b) if h % 2 == 0 else jnp.where(lo_b, zero_b, qt))
                sk.append(jnp.full((1, blk), sink_ref[h], F32))
            q4 = jnp.concatenate(qs, axis=0)
            sink = jnp.concatenate(sk, axis=1)
            st = lax.dot_general(kk, q4, (((1,), (1,)), ((), ())),
                                 preferred_element_type=F32) + mask_add
            m = jnp.maximum(jnp.max(st, axis=0, keepdims=True), sink)
            e = jnp.exp(st - m)
            denom = jnp.sum(e, axis=0, keepdims=True) + jnp.exp(sink - m)
            ot = jnp.dot(vt, e.astype(BF16), preferred_element_type=F32) * (1.0 / denom)
            for i in range(group):
                h = g * group + i
                otbuf[h * HEAD_DIM:(h + 1) * HEAD_DIM, j * blk:(j + 1) * blk] = (
                    ot[:, i * blk:(i + 1) * blk].astype(BF16))

    proj_t = jnp.dot(wot_ref[...], otbuf[...], preferred_element_type=F32)
    o_ref[0] = x + jnp.transpose(proj_t) + bo_ref[...]

    kbuf[0:blk, :] = kbuf[ts:ts + blk, :]
    vtbuf[:, 0:blk] = vtbuf[:, ts:ts + blk]


def _attn_mixer(h, g, w_qkv, b_qkv, sinks, w_o, b_o):
    b, s, d = h.shape
    ts = SEQ_TILE
    const = lambda shape: pl.BlockSpec(shape, lambda i, j, sk: (0,) * len(shape))
    grid_spec = pltpu.PrefetchScalarGridSpec(
        num_scalar_prefetch=1,
        grid=(b, s // ts),
        in_specs=[
            pl.BlockSpec((1, ts, d), lambda i, j, sk: (i, j, 0)),
            const((1, d)),
            const((d, D_QKV)),
            const((1, D_QKV)),
            const((d, D_Q)),
            const((1, d)),
        ],
        out_specs=pl.BlockSpec((1, ts, d), lambda i, j, sk: (i, j, 0)),
        scratch_shapes=[
            pltpu.VMEM((ATTN_BLOCK + ts, N_KV_HEADS * LANES), BF16),
            pltpu.VMEM((D_KV, ATTN_BLOCK + ts), BF16),
            pltpu.VMEM((D_Q, ts), BF16),
        ],
    )
    return pl.pallas_call(
        _attn_kernel,
        out_shape=jax.ShapeDtypeStruct(h.shape, F32),
        grid_spec=grid_spec,
        compiler_params=pltpu.CompilerParams(
            dimension_semantics=("arbitrary", "arbitrary"),
            vmem_limit_bytes=VMEM_LIMIT),
        name="attn_mixer",
    )(sinks, h, g.reshape(1, d), w_qkv.astype(BF16), b_qkv.reshape(1, -1),
      jnp.transpose(w_o).astype(BF16), b_o.reshape(1, d))


def _ffn_kernel(x_ref, g_ref, w1_ref, w3_ref, w2_ref, o_ref, hn_s, acc_s):
    f = pl.program_id(1)

    @pl.when(f == 0)
    def _():
        x = x_ref[...]
        hn_s[...] = _rms(x, g_ref[...]).astype(BF16)
        acc_s[...] = x

    hn = hn_s[...]
    t = _silu(jnp.dot(hn, w1_ref[...], preferred_element_type=F32))
    t = t * jnp.dot(hn, w3_ref[...], preferred_element_type=F32)
    acc_s[...] += jnp.dot(t.astype(BF16), w2_ref[...], preferred_element_type=F32)

    @pl.when(f == pl.num_programs(1) - 1)
    def _():
        o_ref[...] = acc_s[...]


def _ffn(h2d, g, w1, w3, w2, *, tm, tf):
    t, d = h2d.shape
    ff = w1.shape[1]
    return pl.pallas_call(
        _ffn_kernel,
        out_shape=jax.ShapeDtypeStruct((t, d), F32),
        grid=(t // tm, ff // tf),
        in_specs=[
            pl.BlockSpec((tm, d), lambda i, f: (i, 0)),
            pl.BlockSpec((1, d), lambda i, f: (0, 0)),
            pl.BlockSpec((d, tf), lambda i, f: (0, f)),
            pl.BlockSpec((d, tf), lambda i, f: (0, f)),
            pl.BlockSpec((tf, d), lambda i, f: (f, 0)),
        ],
        out_specs=pl.BlockSpec((tm, d), lambda i, f: (i, 0)),
        scratch_shapes=[
            pltpu.VMEM((tm, d), BF16),
            pltpu.VMEM((tm, d), F32),
        ],
        compiler_params=pltpu.CompilerParams(
            dimension_semantics=("arbitrary", "arbitrary"),
            vmem_limit_bytes=VMEM_LIMIT),
        name="dense_ffn",
    )(h2d, g.reshape(1, d), w1.astype(BF16), w3.astype(BF16), w2.astype(BF16))


ROUTE_TILE = 512
EXPERT_TILE = 512
SORT_ROWS = 2 * ROUTE_TILE + N_EXPERTS * SUBLANES
SEG_CHUNKS = tuple(SUBLANES << j for j in range((ROUTE_TILE // SUBLANES).bit_length() - 1, -1, -1))


def _router_kernel(x_ref, g_ref, rw_ref, col_ref, rowp_ref, seg_ref, cnt_ref, carry):
    i = pl.program_id(0)
    tb = ROUTE_TILE

    @pl.when(i == 0)
    def _():
        carry[...] = jnp.zeros(carry.shape, F32)

    hn = _rms(x_ref[...], g_ref[...])
    logits = jnp.dot(hn, rw_ref[...], preferred_element_type=F32,
                     precision=lax.Precision.HIGHEST)
    lane = lax.broadcasted_iota(jnp.int32, (tb, LANES), 1)
    lg = jnp.where(lane < N_EXPERTS, logits, -jnp.inf)
    m1 = jnp.max(lg, axis=-1, keepdims=True)
    i1 = jnp.min(jnp.where(lg == m1, lane, LANES), axis=-1, keepdims=True)
    lg2 = jnp.where(lane == i1, -jnp.inf, lg)
    m2 = jnp.max(lg2, axis=-1, keepdims=True)
    i2 = jnp.min(jnp.where(lg2 == m2, lane, LANES), axis=-1, keepdims=True)
    e2 = jnp.exp(m2 - m1)
    den = 1.0 + e2

    mem_f = jnp.where((lane == i1) | (lane == i2), 1.0, 0.0)
    rr = lax.broadcasted_iota(jnp.int32, (tb, tb), 0)
    cc = lax.broadcasted_iota(jnp.int32, (tb, tb), 1)
    lower = jnp.where(cc < rr, 1.0, 0.0).astype(BF16)
    before = jnp.dot(lower, mem_f.astype(BF16), preferred_element_type=F32)

    n = jnp.broadcast_to(jnp.sum(mem_f, axis=0, keepdims=True), (SUBLANES, LANES))
    n8 = jnp.floor((n + (SUBLANES - 1)) / SUBLANES) * SUBLANES
    incl = n8
    for sh in (1, 2, 4):
        incl = incl + pltpu.roll(incl, sh, axis=1)
    seg8 = incl - n8
    pos = before + seg8[0:1, :]
    rp1 = jnp.sum(jnp.where(lane == i1, pos, 0.0), axis=-1, keepdims=True)
    rp2 = jnp.sum(jnp.where(lane == i2, pos, 0.0), axis=-1, keepdims=True)

    col_ref[...] = jnp.where(lane == 0, 1.0 / den,
                             jnp.where(lane == 1, e2 / den,
                                       jnp.where(lane == 2, rp1, jnp.where(lane == 3, rp2, 0.0))))
    table = jnp.where(lane == 0, rp1, jnp.where(lane == 1, rp2, 0.0))
    rowp_ref[0] = jnp.transpose(table)[0:SUBLANES, :].astype(jnp.int32)
    sub = lax.broadcasted_iota(jnp.int32, (SUBLANES, LANES), 0)
    seg_ref[0] = jnp.where(sub == 0, n8, jnp.where(sub == 1, seg8,
                                                   jnp.where(sub == 2, carry[...], 0.0))).astype(jnp.int32)
    carry[...] = carry[...] + n8
    cnt_ref[...] = carry[...]


def _router(h2d, g, router_w):
    t, d = h2d.shape
    tb = ROUTE_TILE
    rw = jnp.zeros((d, LANES), F32).at[:, :N_EXPERTS].set(router_w)
    return pl.pallas_call(
        _router_kernel,
        out_shape=(
            jax.ShapeDtypeStruct((t, LANES), F32),
            jax.ShapeDtypeStruct((t // tb, SUBLANES, tb), jnp.int32),
            jax.ShapeDtypeStruct((t // tb, SUBLANES, LANES), jnp.int32),
            jax.ShapeDtypeStruct((SUBLANES, LANES), F32),
        ),
        grid=(t // tb,),
        in_specs=[
            pl.BlockSpec((tb, d), lambda i: (i, 0)),
            pl.BlockSpec((1, d), lambda i: (0, 0)),
            pl.BlockSpec((d, LANES), lambda i: (0, 0)),
        ],
        out_specs=(
            pl.BlockSpec((tb, LANES), lambda i: (i, 0)),
            pl.BlockSpec((1, SUBLANES, tb), lambda i: (i, 0, 0)),
            pl.BlockSpec((1, SUBLANES, LANES), lambda i: (i, 0, 0)),
            pl.BlockSpec((SUBLANES, LANES), lambda i: (0, 0)),
        ),
        scratch_shapes=[pltpu.VMEM((SUBLANES, LANES), F32)],
        compiler_params=pltpu.CompilerParams(
            dimension_semantics=("arbitrary",), vmem_limit_bytes=VMEM_LIMIT),
        name="moe_router",
    )(h2d, g.reshape(1, d), rw)


def _segment_copies(seg_ref, off_ref, make_copy, do_start):
    for e in range(N_EXPERTS):
        n8 = seg_ref[0, 0, e]
        tile_row = seg_ref[0, 1, e]
        group_row = off_ref[e] + seg_ref[0, 2, e]
        for c in SEG_CHUNKS:
            done = n8 & ~(2 * c - 1)

            @pl.when((n8 & c) != 0)
            def _():
                cp = make_copy(pl.multiple_of(tile_row + done, SUBLANES),
                               pl.multiple_of(group_row + done, SUBLANES), c)
                cp.start() if do_start else cp.wait()


def _dispatch_kernel(off_ref, pad_lo_ref, pad_hi_ref, na_ref, h_ref, g_ref, rowp_ref, seg_ref,
                     xs_ref, sorted_s, zblk, sem):
    tb = ROUTE_TILE
    tm = EXPERT_TILE

    @pl.when(pl.program_id(0) == 0)
    def _():
        zblk[...] = jnp.zeros(zblk.shape, F32)
        n_blocks = xs_ref.shape[0] // tm

        def fill(do_start):
            def pad_group(r, c):
                cp = pltpu.make_async_copy(
                    zblk.at[pl.ds(0, SUBLANES)],
                    xs_ref.at[pl.ds(pl.multiple_of(r * SUBLANES, SUBLANES), SUBLANES)], sem)
                cp.start() if do_start else cp.wait()
                return c

            def tail_blk(b, c):
                cp = pltpu.make_async_copy(
                    zblk, xs_ref.at[pl.ds(pl.multiple_of(b * tm, tm), tm)], sem)
                cp.start() if do_start else cp.wait()
                return c

            for e in range(N_EXPERTS):
                lax.fori_loop(pad_lo_ref[e], pad_hi_ref[e], pad_group, 0)
            lax.fori_loop(na_ref[0], n_blocks, tail_blk, 0)

        fill(True)
        fill(False)

    hn = _rms(h_ref[...], g_ref[...]).astype(BF16)
    row = lax.broadcasted_iota(jnp.int32, (SORT_ROWS, tb), 0)
    hit = (row == rowp_ref[0, 0:1, :]) | (row == rowp_ref[0, 1:2, :])
    sorted_s[...] = jnp.dot(jnp.where(hit, 1.0, 0.0).astype(BF16), hn, preferred_element_type=F32)

    def make_copy(tile_row, group_row, rows):
        return pltpu.make_async_copy(sorted_s.at[pl.ds(tile_row, rows)],
                                     xs_ref.at[pl.ds(group_row, rows)], sem)

    _segment_copies(seg_ref, off_ref, make_copy, True)
    _segment_copies(seg_ref, off_ref, make_copy, False)


def _dispatch(h2d, g, rowp, seg, off, pad_lo, pad_hi, n_active, n_rows):
    t, d = h2d.shape
    tb = ROUTE_TILE
    grid_spec = pltpu.PrefetchScalarGridSpec(
        num_scalar_prefetch=4,
        grid=(t // tb,),
        in_specs=[
            pl.BlockSpec((tb, d), lambda i, *_: (i, 0)),
            pl.BlockSpec((1, d), lambda i, *_: (0, 0)),
            pl.BlockSpec((1, SUBLANES, tb), lambda i, *_: (i, 0, 0)),
            pl.BlockSpec((1, SUBLANES, LANES), lambda i, *_: (i, 0, 0), memory_space=pltpu.SMEM),
        ],
        out_specs=pl.BlockSpec(memory_space=pl.ANY),
        scratch_shapes=[pltpu.VMEM((SORT_ROWS, d), F32),
                        pltpu.VMEM((EXPERT_TILE, d), F32),
                        pltpu.SemaphoreType.DMA(())],
    )
    return pl.pallas_call(
        _dispatch_kernel,
        out_shape=jax.ShapeDtypeStruct((n_rows, d), F32),
        grid_spec=grid_spec,
        compiler_params=pltpu.CompilerParams(
            dimension_semantics=("arbitrary",), has_side_effects=True,
            vmem_limit_bytes=VMEM_LIMIT),
        name="moe_dispatch",
    )(off, pad_lo, pad_hi, n_active, h2d, g.reshape(1, d), rowp, seg)


def _expert_kernel(be_ref, na_ref, x_ref, w1_ref, w3_ref, w2_ref, y_ref):
    i = pl.program_id(0)
    f = pl.program_id(1)

    @pl.when(i < na_ref[0])
    def _():
        x = x_ref[...].astype(BF16)
        t = _silu(jnp.dot(x, w1_ref[0], preferred_element_type=F32))
        t = t * jnp.dot(x, w3_ref[0], preferred_element_type=F32)
        part = jnp.dot(t.astype(BF16), w2_ref[0], preferred_element_type=F32)

        @pl.when(f == 0)
        def _():
            y_ref[...] = part

        @pl.when(f > 0)
        def _():
            y_ref[...] += part

    @pl.when((i >= na_ref[0]) & (f == 0))
    def _():
        y_ref[...] = jnp.zeros(y_ref.shape, F32)


def _experts(xs, blk_expert, n_active, w1, w3, w2, *, tf):
    r, d = xs.shape
    tm = EXPERT_TILE
    ff = w1.shape[2]
    nf = ff // tf

    def row_blk(i, f, be, na):
        return (jnp.maximum(jnp.minimum(i, na[0] - 1), 0), 0)

    def f_blk(i, f, na):
        return jnp.where(i < na[0], f, nf - 1)

    grid_spec = pltpu.PrefetchScalarGridSpec(
        num_scalar_prefetch=2,
        grid=(r // tm, nf),
        in_specs=[
            pl.BlockSpec((tm, d), row_blk),
            pl.BlockSpec((1, d, tf), lambda i, f, be, na: (be[i], 0, f_blk(i, f, na))),
            pl.BlockSpec((1, d, tf), lambda i, f, be, na: (be[i], 0, f_blk(i, f, na))),
            pl.BlockSpec((1, tf, d), lambda i, f, be, na: (be[i], f_blk(i, f, na), 0)),
        ],
        out_specs=pl.BlockSpec((tm, d), lambda i, f, be, na: (i, 0)),
    )
    return pl.pallas_call(
        _expert_kernel,
        out_shape=jax.ShapeDtypeStruct((r, d), F32),
        grid_spec=grid_spec,
        compiler_params=pltpu.CompilerParams(
            dimension_semantics=("arbitrary", "arbitrary"),
            vmem_limit_bytes=VMEM_LIMIT),
        name="moe_experts",
    )(blk_expert, n_active, xs, w1, w3, w2)


def _combine_kernel(off_ref, h_ref, col_ref, seg_ref, y_ref, o_ref, ysort, sem):
    tb = ROUTE_TILE

    @pl.when(pl.program_id(0) == 0)
    def _():
        ysort[...] = jnp.zeros(ysort.shape, F32)

    def make_copy(tile_row, group_row, rows):
        return pltpu.make_async_copy(y_ref.at[pl.ds(group_row, rows)],
                                     ysort.at[pl.ds(tile_row, rows)], sem)

    _segment_copies(seg_ref, off_ref, make_copy, True)
    _segment_copies(seg_ref, off_ref, make_copy, False)

    col = col_ref[...]
    row = lax.broadcasted_iota(jnp.int32, (tb, SORT_ROWS), 1)
    yb = ysort[...].astype(BF16)
    acc = None
    for slot in range(2):
        pick = jnp.where(row == col[:, 2 + slot:3 + slot].astype(jnp.int32), 1.0, 0.0).astype(BF16)
        term = col[:, slot:slot + 1] * jnp.dot(pick, yb, preferred_element_type=F32)
        acc = term if acc is None else acc + term
    o_ref[...] = h_ref[...] + acc


def _combine(h2d, col, seg, y, off):
    t, d = h2d.shape
    tb = ROUTE_TILE
    grid_spec = pltpu.PrefetchScalarGridSpec(
        num_scalar_prefetch=1,
        grid=(t // tb,),
        in_specs=[
            pl.BlockSpec((tb, d), lambda i, off: (i, 0)),
            pl.BlockSpec((tb, LANES), lambda i, off: (i, 0)),
            pl.BlockSpec((1, SUBLANES, LANES), lambda i, off: (i, 0, 0), memory_space=pltpu.SMEM),
            pl.BlockSpec(memory_space=pl.ANY),
        ],
        out_specs=pl.BlockSpec((tb, d), lambda i, off: (i, 0)),
        scratch_shapes=[pltpu.VMEM((SORT_ROWS, d), F32), pltpu.SemaphoreType.DMA(())],
    )
    return pl.pallas_call(
        _combine_kernel,
        out_shape=jax.ShapeDtypeStruct((t, d), F32),
        grid_spec=grid_spec,
        compiler_params=pltpu.CompilerParams(
            dimension_semantics=("arbitrary",), vmem_limit_bytes=VMEM_LIMIT),
        name="moe_combine",
    )(off, h2d, col, seg, y)


def _moe(h2d, g, router_w, w1, w3, w2, layer):
    t, d = h2d.shape
    tm = EXPERT_TILE
    n_tiles = t // ROUTE_TILE
    n_rows = 2 * t + n_tiles * N_EXPERTS * SUBLANES + N_EXPERTS * tm
    n_rows = -(-n_rows // tm) * tm
    col, rowp, seg, cnt = _router(h2d, g, router_w)
    counts = cnt[0, :N_EXPERTS].astype(jnp.int32)
    blocks = (counts + tm - 1) // tm
    ends = jnp.cumsum(blocks)
    off = (ends - blocks) * tm
    n_active = ends[-1:]
    blk = jnp.arange(n_rows // tm, dtype=jnp.int32)
    blk_expert = jnp.sum(jnp.minimum(blk, n_active - 1)[:, None] >= ends[None, :], axis=1)
    blk_expert = blk_expert.astype(jnp.int32) + layer * N_EXPERTS
    xs = _dispatch(h2d, g, rowp, seg, off, (off + counts) // SUBLANES,
                   (off + blocks * tm) // SUBLANES, n_active, n_rows)
    y = _experts(xs, blk_expert, n_active, w1, w3, w2, tf=1792)
    return _combine(h2d, col, seg, y, off)


def _final_norm_kernel(x_ref, g_ref, o_ref):
    o_ref[...] = _rms(x_ref[...], g_ref[...])


def _final_norm(h2d, g):
    t, d = h2d.shape
    tm = 1024
    return pl.pallas_call(
        _final_norm_kernel,
        out_shape=jax.ShapeDtypeStruct((t, d), F32),
        grid=(t // tm,),
        in_specs=[pl.BlockSpec((tm, d), lambda i: (i, 0)),
                  pl.BlockSpec((1, d), lambda i: (0, 0))],
        out_specs=pl.BlockSpec((tm, d), lambda i: (i, 0)),
        name="final_norm",
    )(h2d, g.reshape(1, d))


def kernel(x, e_norm1, e_w_in, e_conv_w, e_conv_b, e_ln_g, e_ln_b, e_pool_w, e_pool_scale,
           e_w_out, e_norm2, e_ff_w1, e_ff_w3, e_ff_w2, o_norm1, o_w_qkv, o_b_qkv, o_sinks,
           o_w_o, o_b_o, o_norm2, o_router, o_exp_w1, o_exp_w3, o_exp_w2, final_norm):
    b, s, d = x.shape
    expert_w = [w.astype(BF16).reshape((-1,) + w.shape[2:]) for w in (o_exp_w1, o_exp_w3, o_exp_w2)]
    h = x
    for layer in range(DEPTH):
        i = layer // 2
        if layer % 2 == 0:
            h = _even_mixer(h, e_norm1[i], e_w_in[i], e_conv_w[i], e_conv_b[i], e_ln_g[i],
                            e_ln_b[i], e_pool_w[i], e_pool_scale[i], e_w_out[i])
            h = _ffn(h.reshape(b * s, d), e_norm2[i], e_ff_w1[i], e_ff_w3[i], e_ff_w2[i],
                     tm=512, tf=1408).reshape(b, s, d)
        else:
            h = _attn_mixer(h, o_norm1[i], o_w_qkv[i], o_b_qkv[i], o_sinks[i], o_w_o[i],
                            o_b_o[i])
            h = _moe(h.reshape(b * s, d), o_norm2[i], o_router[i], *expert_w, i).reshape(b, s, d)
    return _final_norm(h.reshape(b * s, d), final_norm).reshape(b, s, d)
```

```python
import functools

import jax
import jax.numpy as jnp
from jax import lax
from jax.experimental import pallas as pl
from jax.experimental.pallas import tpu as pltpu

F32 = jnp.float32
BF16 = jnp.bfloat16

D_MODEL = 1024
DEPTH = 4
RMS_EPS = 1e-5
LN_EPS = 1e-5

D_CONV = 512
D_POOL = 512
CONV_K = 31
POOL_WINDOWS = (2, 4, 8, 16)
POOL_GROUP = 128
D_IN_EVEN = 2 * D_CONV + D_POOL

HEAD_DIM = 64
N_Q_HEADS = 16
N_KV_HEADS = 4
ATTN_BLOCK = 128
D_Q = N_Q_HEADS * HEAD_DIM
D_KV = N_KV_HEADS * HEAD_DIM
D_QKV = D_Q + 2 * D_KV

N_EXPERTS = 8

LANES = 128
SUBLANES = 8
SEQ_TILE = 512
HALO = 32
CONV_ROWS = 32
VMEM_LIMIT = 56 * 1024 * 1024


def _rms(x, g):
    ms = jnp.mean(x * x, axis=-1, keepdims=True)
    return x * lax.rsqrt(ms + RMS_EPS) * g


def _silu(x):
    return x * jax.nn.sigmoid(x)


def _even_mixer_kernel(x_ref, g_ref, win_ref, cw_ref, cb_ref, lg_ref, lb_ref,
                       pw_ref, ps_ref, wout_ref, o_ref,
                       abuf, ashift, cwb, bbuf, p1, p2, p3, cat):
    s = pl.program_id(1)
    ts = SEQ_TILE
    rows = HALO + ts

    @pl.when(s == 0)
    def _():
        abuf[0:HALO, :] = jnp.zeros((HALO, D_CONV), F32)
        bbuf[0:HALO, :] = jnp.zeros((HALO, D_POOL), F32)

    x = x_ref[0]
    hn = _rms(x, g_ref[...]).astype(BF16)
    u = jnp.dot(hn, win_ref[...], preferred_element_type=F32)
    abuf[HALO:rows, :] = u[:, :D_CONV] * jax.nn.sigmoid(u[:, D_CONV:2 * D_CONV])
    bbuf[HALO:rows, :] = u[:, 2 * D_CONV:]

    a_all = abuf[...]
    for r in range(1, SUBLANES):
        ashift[r - 1] = pltpu.roll(a_all, rows - r, axis=0)
    for k in range(CONV_K):
        cwb[k] = jnp.broadcast_to(cw_ref[k:k + 1, :], (SUBLANES, D_CONV))
    for c in range(ts // CONV_ROWS):
        r0 = c * CONV_ROWS
        acc = jnp.broadcast_to(cb_ref[...], (CONV_ROWS, D_CONV))
        for k in range(CONV_K):
            q8, r = divmod(HALO - (CONV_K - 1) + k, SUBLANES)
            src = abuf if r == 0 else ashift.at[r - 1]
            start = r0 + q8 * SUBLANES
            tap = jnp.concatenate([cwb[k]] * (CONV_ROWS // SUBLANES), axis=0)
            acc = acc + tap * src[start:start + CONV_ROWS, :]
        mu = jnp.mean(acc, axis=-1, keepdims=True)
        d = acc - mu
        var = jnp.mean(d * d, axis=-1, keepdims=True)
        y = d * lax.rsqrt(var + LN_EPS) * lg_ref[...] + lb_ref[...]
        cat[r0:r0 + CONV_ROWS, 0:D_CONV] = _silu(y).astype(BF16)

    p1[8:rows, :] = bbuf[8:rows, :] + bbuf[7:rows - 1, :]
    p2[16:rows, 0:384] = p1[16:rows, 128:512] + p1[14:rows - 2, 128:512]
    p3[24:rows, 0:256] = p2[24:rows, 128:384] + p2[20:rows - 4, 128:384]
    s16 = p3[HALO:rows, 128:256] + p3[HALO - 8:rows - 8, 128:256]
    sums = (p1[HALO:rows, 0:128], p2[HALO:rows, 0:128], p3[HALO:rows, 0:128], s16)
    pos1 = (s * ts + 1 + lax.broadcasted_iota(jnp.int32, (ts, 1), 0)).astype(F32)
    for g, w in enumerate(POOL_WINDOWS):
        cnt = jnp.minimum(pos1, float(w))
        pg = sums[g] / cnt - bbuf[HALO:rows, g * POOL_GROUP:(g + 1) * POOL_GROUP]
        pm = jnp.dot(pg.astype(BF16), pw_ref[g], preferred_element_type=F32)
        pm = pm * ps_ref[:, g * POOL_GROUP:(g + 1) * POOL_GROUP]
        cat[:, D_CONV + g * POOL_GROUP:D_CONV + (g + 1) * POOL_GROUP] = pm.astype(BF16)

    o_ref[0] = x + jnp.dot(cat[...], wout_ref[...], preferred_element_type=F32)

    abuf[0:HALO, :] = abuf[ts:rows, :]
    bbuf[0:HALO, :] = bbuf[ts:rows, :]


def _even_mixer(h, g, w_in, conv_w, conv_b, ln_g, ln_b, pool_w, pool_scale, w_out):
    b, s, d = h.shape
    ts = SEQ_TILE
    rows = HALO + ts
    const = lambda shape: pl.BlockSpec(shape, lambda i, j: (0,) * len(shape))
    return pl.pallas_call(
        _even_mixer_kernel,
        out_shape=jax.ShapeDtypeStruct(h.shape, F32),
        grid=(b, s // ts),
        in_specs=[
            pl.BlockSpec((1, ts, d), lambda i, j: (i, j, 0)),
            const((1, d)),
            const((d, D_IN_EVEN)),
            const((CONV_K, D_CONV)),
            const((1, D_CONV)),
            const((1, D_CONV)),
            const((1, D_CONV)),
            const((len(POOL_WINDOWS), POOL_GROUP, POOL_GROUP)),
            const((1, D_POOL)),
            const((d, d)),
        ],
        out_specs=pl.BlockSpec((1, ts, d), lambda i, j: (i, j, 0)),
        scratch_shapes=[
            pltpu.VMEM((rows, D_CONV), F32),
            pltpu.VMEM((SUBLANES - 1, rows, D_CONV), F32),
            pltpu.VMEM((CONV_K, SUBLANES, D_CONV), F32),
            pltpu.VMEM((rows, D_POOL), F32),
            pltpu.VMEM((rows, D_POOL), F32),
            pltpu.VMEM((rows, 384), F32),
            pltpu.VMEM((rows, 256), F32),
            pltpu.VMEM((ts, d), BF16),
        ],
        compiler_params=pltpu.CompilerParams(
            dimension_semantics=("arbitrary", "arbitrary"),
            vmem_limit_bytes=VMEM_LIMIT),
        name="even_mixer",
    )(h, g.reshape(1, d), w_in.astype(BF16), conv_w, conv_b.reshape(1, -1),
      ln_g.reshape(1, -1), ln_b.reshape(1, -1), pool_w.astype(BF16),
      pool_scale.reshape(1, -1), w_out.astype(BF16))


def _attn_kernel(sink_ref, x_ref, g_ref, wqkv_ref, bqkv_ref, wot_ref, bo_ref, o_ref,
                 kbuf, vtbuf, otbuf):
    s = pl.program_id(1)
    ts = SEQ_TILE
    blk = ATTN_BLOCK
    group = N_Q_HEADS // N_KV_HEADS

    @pl.when(s == 0)
    def _():
        kbuf[0:blk, :] = jnp.zeros((blk, N_KV_HEADS * LANES), BF16)
        vtbuf[:, 0:blk] = jnp.zeros((D_KV, blk), BF16)

    x = x_ref[0]
    hn = _rms(x, g_ref[...]).astype(BF16)
    u = jnp.dot(hn, wqkv_ref[...], preferred_element_type=F32) + bqkv_ref[...]
    q = (u[:, :D_Q] * (HEAD_DIM ** -0.5)).astype(BF16)

    lo_t = lax.broadcasted_iota(jnp.int32, (ts, LANES), 1) < HEAD_DIM
    for p in range(N_KV_HEADS // 2):
        t = u[:, D_Q + p * LANES:D_Q + (p + 1) * LANES]
        r = pltpu.roll(t, HEAD_DIM, axis=1)
        kbuf[blk:blk + ts, (2 * p) * LANES:(2 * p + 1) * LANES] = jnp.where(lo_t, t, r).astype(BF16)
        kbuf[blk:blk + ts, (2 * p + 1) * LANES:(2 * p + 2) * LANES] = jnp.where(lo_t, r, t).astype(BF16)
    vtbuf[:, blk:blk + ts] = jnp.transpose(u[:, D_Q + D_KV:]).astype(BF16)

    lo_b = lax.broadcasted_iota(jnp.int32, (blk, LANES), 1) < HEAD_DIM
    zero_b = jnp.zeros((blk, LANES), BF16)
    key = lax.broadcasted_iota(jnp.int32, (2 * blk, blk), 0)
    qry = lax.broadcasted_iota(jnp.int32, (2 * blk, blk), 1)
    band = (key > qry) & (key <= qry + blk)
    band0 = band & ((key >= blk) | (s > 0))
    neg = jnp.concatenate([jnp.where(band, 0.0, -jnp.inf)] * group, axis=1)
    neg0 = jnp.concatenate([jnp.where(band0, 0.0, -jnp.inf)] * group, axis=1)

    for j in range(ts // blk):
        mask_add = neg0 if j == 0 else neg
        for g in range(N_KV_HEADS):
            kk = kbuf[j * blk:(j + 2) * blk, g * LANES:(g + 1) * LANES]
            vt = vtbuf[g * HEAD_DIM:(g + 1) * HEAD_DIM, j * blk:(j + 2) * blk]
            qs = []
            sk = []
            for i in range(group):
                h = g * group + i
                qt = q[j * blk:(j + 1) * blk, (h // 2) * LANES:(h // 2 + 1) * LANES]
                qs.append(jnp.where(lo_b, qt, zero_b) if h % 2 == 0 else jnp.where(lo_b, zero_b, qt))
                sk.append(jnp.full((1, blk), sink_ref[h], F32))
            q4 = jnp.concatenate(qs, axis=0)
            sink = jnp.concatenate(sk, axis=1)
            st = lax.dot_general(kk, q4, (((1,), (1,)), ((), ())),
                                 preferred_element_type=F32) + mask_add
            m = jnp.maximum(jnp.max(st, axis=0, keepdims=True), sink)
            e = jnp.exp(st - m)
            denom = jnp.sum(e, axis=0, keepdims=True) + jnp.exp(sink - m)
            ot = jnp.dot(vt, e.astype(BF16), preferred_element_type=F32) * (1.0 / denom)
            for i in range(group):
                h = g * group + i
                otbuf[h * HEAD_DIM:(h + 1) * HEAD_DIM, j * blk:(j + 1) * blk] = (
                    ot[:, i * blk:(i + 1) * blk].astype(BF16))

    proj_t = jnp.dot(wot_ref[...], otbuf[...], preferred_element_type=F32)
    o_ref[0] = x + jnp.transpose(proj_t) + bo_ref[...]

    kbuf[0:blk, :] = kbuf[ts:ts + blk, :]
    vtbuf[:, 0:blk] = vtbuf[:, ts:ts + blk]


def _attn_mixer(h, g, w_qkv, b_qkv, sinks, w_o, b_o):
    b, s, d = h.shape
    ts = SEQ_TILE
    const = lambda shape: pl.BlockSpec(shape, lambda i, j, sk: (0,) * len(shape))
    grid_spec = pltpu.PrefetchScalarGridSpec(
        num_scalar_prefetch=1,
        grid=(b, s // ts),
        in_specs=[
            pl.BlockSpec((1, ts, d), lambda i, j, sk: (i, j, 0)),
            const((1, d)),
            const((d, D_QKV)),
            const((1, D_QKV)),
            const((d, D_Q)),
            const((1, d)),
        ],
        out_specs=pl.BlockSpec((1, ts, d), lambda i, j, sk: (i, j, 0)),
        scratch_shapes=[
            pltpu.VMEM((ATTN_BLOCK + ts, N_KV_HEADS * LANES), BF16),
            pltpu.VMEM((D_KV, ATTN_BLOCK + ts), BF16),
            pltpu.VMEM((D_Q, ts), BF16),
        ],
    )
    return pl.pallas_call(
        _attn_kernel,
        out_shape=jax.ShapeDtypeStruct(h.shape, F32),
        grid_spec=grid_spec,
        compiler_params=pltpu.CompilerParams(
            dimension_semantics=("arbitrary", "arbitrary"),
            vmem_limit_bytes=VMEM_LIMIT),
        name="attn_mixer",
    )(sinks, h, g.reshape(1, d), w_qkv.astype(BF16), b_qkv.reshape(1, -1),
      jnp.transpose(w_o).astype(BF16), b_o.reshape(1, d))


def _ffn_kernel(x_ref, g_ref, w1_ref, w3_ref, w2_ref, o_ref, hn_s, acc_s):
    f = pl.program_id(1)

    @pl.when(f == 0)
    def _():
        x = x_ref[...]
        hn_s[...] = _rms(x, g_ref[...]).astype(BF16)
        acc_s[...] = x

    hn = hn_s[...]
    t = _silu(jnp.dot(hn, w1_ref[...], preferred_element_type=F32))
    t = t * jnp.dot(hn, w3_ref[...], preferred_element_type=F32)
    acc_s[...] += jnp.dot(t.astype(BF16), w2_ref[...], preferred_element_type=F32)

    @pl.when(f == pl.num_programs(1) - 1)
    def _():
        o_ref[...] = acc_s[...]


def _ffn(h2d, g, w1, w3, w2, *, tm, tf):
    t, d = h2d.shape
    ff = w1.shape[1]
    return pl.pallas_call(
        _ffn_kernel,
        out_shape=jax.ShapeDtypeStruct((t, d), F32),
        grid=(t // tm, ff // tf),
        in_specs=[
            pl.BlockSpec((tm, d), lambda i, f: (i, 0)),
            pl.BlockSpec((1, d), lambda i, f: (0, 0)),
            pl.BlockSpec((d, tf), lambda i, f: (0, f)),
            pl.BlockSpec((d, tf), lambda i, f: (0, f)),
            pl.BlockSpec((tf, d), lambda i, f: (f, 0)),
        ],
        out_specs=pl.BlockSpec((tm, d), lambda i, f: (i, 0)),
        scratch_shapes=[
            pltpu.VMEM((tm, d), BF16),
            pltpu.VMEM((tm, d), F32),
        ],
        compiler_params=pltpu.CompilerParams(
            dimension_semantics=("arbitrary", "arbitrary"),
            vmem_limit_bytes=VMEM_LIMIT),
        name="dense_ffn",
    )(h2d, g.reshape(1, d), w1.astype(BF16), w3.astype(BF16), w2.astype(BF16))


ROUTE_TILE = 512
EXPERT_TILE = 512
SORT_ROWS = 2 * ROUTE_TILE + N_EXPERTS * SUBLANES
SEG_CHUNKS = tuple(SUBLANES << j for j in range((ROUTE_TILE // SUBLANES).bit_length() - 1, -1, -1))


def _router_kernel(x_ref, g_ref, rw_ref, col_ref, rowp_ref, seg_ref, cnt_ref, carry):
    i = pl.program_id(0)
    tb = ROUTE_TILE

    @pl.when(i == 0)
    def _():
        carry[...] = jnp.zeros(carry.shape, F32)

    hn = _rms(x_ref[...], g_ref[...])
    rw = rw_ref[...]
    hn_hi = hn.astype(BF16)
    hn_lo = (hn - hn_hi.astype(F32)).astype(BF16)
    rw_hi = rw.astype(BF16)
    rw_lo = (rw - rw_hi.astype(F32)).astype(BF16)
    logits = jnp.dot(hn_hi, rw_hi, preferred_element_type=F32) + (
        jnp.dot(hn_lo, rw_hi, preferred_element_type=F32)
        + jnp.dot(hn_hi, rw_lo, preferred_element_type=F32))
    lane = lax.broadcasted_iota(jnp.int32, (tb, LANES), 1)
    lg = jnp.where(lane < N_EXPERTS, logits, -jnp.inf)
    m1 = jnp.max(lg, axis=-1, keepdims=True)
    i1 = jnp.min(jnp.where(lg == m1, lane, LANES), axis=-1, keepdims=True)
    lg2 = jnp.where(lane == i1, -jnp.inf, lg)
    m2 = jnp.max(lg2, axis=-1, keepdims=True)
    i2 = jnp.min(jnp.where(lg2 == m2, lane, LANES), axis=-1, keepdims=True)
    e2 = jnp.exp(m2 - m1)
    den = 1.0 + e2

    mem_f = jnp.where((lane == i1) | (lane == i2), 1.0, 0.0)
    rr = lax.broadcasted_iota(jnp.int32, (tb, tb), 0)
    cc = lax.broadcasted_iota(jnp.int32, (tb, tb), 1)
    lower = jnp.where(cc < rr, 1.0, 0.0).astype(BF16)
    before = jnp.dot(lower, mem_f.astype(BF16), preferred_element_type=F32)

    n = jnp.broadcast_to(jnp.sum(mem_f, axis=0, keepdims=True), (SUBLANES, LANES))
    n8 = jnp.floor((n + (SUBLANES - 1)) / SUBLANES) * SUBLANES
    incl = n8
    for sh in (1, 2, 4):
        incl = incl + pltpu.roll(incl, sh, axis=1)
    seg8 = incl - n8
    pos = before + seg8[0:1, :]
    rp1 = jnp.sum(jnp.where(lane == i1, pos, 0.0), axis=-1, keepdims=True)
    rp2 = jnp.sum(jnp.where(lane == i2, pos, 0.0), axis=-1, keepdims=True)

    col_ref[...] = jnp.where(lane == 0, 1.0 / den,
                             jnp.where(lane == 1, e2 / den,
                                       jnp.where(lane == 2, rp1, jnp.where(lane == 3, rp2, 0.0))))
    table = jnp.where(lane == 0, rp1, jnp.where(lane == 1, rp2, 0.0))
    rowp_ref[0] = jnp.transpose(table)[0:SUBLANES, :].astype(jnp.int32)
    sub = lax.broadcasted_iota(jnp.int32, (SUBLANES, LANES), 0)
    seg_ref[0] = jnp.where(sub == 0, n8, jnp.where(sub == 1, seg8,
                                                   jnp.where(sub == 2, carry[...], 0.0))).astype(jnp.int32)
    carry[...] = carry[...] + n8
    cnt_ref[...] = carry[...]


def _router(h2d, g, router_w):
    t, d = h2d.shape
    tb = ROUTE_TILE
    rw = jnp.zeros((d, LANES), F32).at[:, :N_EXPERTS].set(router_w)
    return pl.pallas_call(
        _router_kernel,
        out_shape=(
            jax.ShapeDtypeStruct((t, LANES), F32),
            jax.ShapeDtypeStruct((t // tb, SUBLANES, tb), jnp.int32),
            jax.ShapeDtypeStruct((t // tb, SUBLANES, LANES), jnp.int32),
            jax.ShapeDtypeStruct((SUBLANES, LANES), F32),
        ),
        grid=(t // tb,),
        in_specs=[
            pl.BlockSpec((tb, d), lambda i: (i, 0)),
            pl.BlockSpec((1, d), lambda i: (0, 0)),
            pl.BlockSpec((d, LANES), lambda i: (0, 0)),
        ],
        out_specs=(
            pl.BlockSpec((tb, LANES), lambda i: (i, 0)),
            pl.BlockSpec((1, SUBLANES, tb), lambda i: (i, 0, 0)),
            pl.BlockSpec((1, SUBLANES, LANES), lambda i: (i, 0, 0)),
            pl.BlockSpec((SUBLANES, LANES), lambda i: (0, 0)),
        ),
        scratch_shapes=[pltpu.VMEM((SUBLANES, LANES), F32)],
        compiler_params=pltpu.CompilerParams(
            dimension_semantics=("arbitrary",), vmem_limit_bytes=VMEM_LIMIT),
        name="moe_router",
    )(h2d, g.reshape(1, d), rw)


def _segment_copies(seg_ref, off_ref, make_copy, do_start):
    for e in range(N_EXPERTS):
        n8 = seg_ref[0, 0, e]
        tile_row = seg_ref[0, 1, e]
        group_row = off_ref[e] + seg_ref[0, 2, e]
        for c in SEG_CHUNKS:
            done = n8 & ~(2 * c - 1)

            @pl.when((n8 & c) != 0)
            def _():
                cp = make_copy(pl.multiple_of(tile_row + done, SUBLANES),
                               pl.multiple_of(group_row + done, SUBLANES), c)
                cp.start() if do_start else cp.wait()


def _dispatch_kernel(off_ref, pad_lo_ref, pad_hi_ref, na_ref, h_ref, g_ref, rowp_ref, seg_ref,
                     seg_prev_ref, xs_ref, sorted_s, zblk, sem, zsem):
    tb = ROUTE_TILE
    tm = EXPERT_TILE
    i = pl.program_id(0)
    slot = i % 2

    @pl.when(i == 0)
    def _():
        zblk[...] = jnp.zeros(zblk.shape, F32)
        n_blocks = xs_ref.shape[0] // tm

        def fill(do_start):
            def pad_group(r, c):
                cp = pltpu.make_async_copy(
                    zblk.at[pl.ds(0, SUBLANES)],
                    xs_ref.at[pl.ds(pl.multiple_of(r * SUBLANES, SUBLANES), SUBLANES)], zsem)
                cp.start() if do_start else cp.wait()
                return c

            def tail_blk(b, c):
                cp = pltpu.make_async_copy(
                    zblk, xs_ref.at[pl.ds(pl.multiple_of(b * tm, tm), tm)], zsem)
                cp.start() if do_start else cp.wait()
                return c

            for e in range(N_EXPERTS):
                lax.fori_loop(pad_lo_ref[e], pad_hi_ref[e], pad_group, 0)
            lax.fori_loop(na_ref[0], n_blocks, tail_blk, 0)

        fill(True)
        fill(False)

    hn = _rms(h_ref[...], g_ref[...]).astype(BF16)
    row = lax.broadcasted_iota(jnp.int32, (SORT_ROWS, tb), 0)
    hit = (row == rowp_ref[0, 0:1, :]) | (row == rowp_ref[0, 1:2, :])
    sorted_s[slot] = jnp.dot(jnp.where(hit, 1.0, 0.0).astype(BF16), hn, preferred_element_type=F32)

    def copier(buf):
        def make_copy(tile_row, group_row, rows):
            return pltpu.make_async_copy(sorted_s.at[buf, pl.ds(tile_row, rows)],
                                         xs_ref.at[pl.ds(group_row, rows)], sem.at[buf])
        return make_copy

    _segment_copies(seg_ref, off_ref, copier(slot), True)

    @pl.when(i > 0)
    def _():
        _segment_copies(seg_prev_ref, off_ref, copier(1 - slot), False)

    @pl.when(i == pl.num_programs(0) - 1)
    def _():
        _segment_copies(seg_ref, off_ref, copier(slot), False)


def _dispatch(h2d, g, rowp, seg, off, pad_lo, pad_hi, n_active, n_rows):
    t, d = h2d.shape
    tb = ROUTE_TILE
    seg_spec = lambda index: pl.BlockSpec((1, SUBLANES, LANES), index, memory_space=pltpu.SMEM)
    grid_spec = pltpu.PrefetchScalarGridSpec(
        num_scalar_prefetch=4,
        grid=(t // tb,),
        in_specs=[
            pl.BlockSpec((tb, d), lambda i, *_: (i, 0)),
            pl.BlockSpec((1, d), lambda i, *_: (0, 0)),
            pl.BlockSpec((1, SUBLANES, tb), lambda i, *_: (i, 0, 0)),
            seg_spec(lambda i, *_: (i, 0, 0)),
            seg_spec(lambda i, *_: (jnp.maximum(i - 1, 0), 0, 0)),
        ],
        out_specs=pl.BlockSpec(memory_space=pl.ANY),
        scratch_shapes=[pltpu.VMEM((2, SORT_ROWS, d), F32),
                        pltpu.VMEM((EXPERT_TILE, d), F32),
                        pltpu.SemaphoreType.DMA((2,)),
                        pltpu.SemaphoreType.DMA(())],
    )
    return pl.pallas_call(
        _dispatch_kernel,
        out_shape=jax.ShapeDtypeStruct((n_rows, d), F32),
        grid_spec=grid_spec,
        compiler_params=pltpu.CompilerParams(
            dimension_semantics=("arbitrary",), has_side_effects=True,
            vmem_limit_bytes=VMEM_LIMIT),
        name="moe_dispatch",
    )(off, pad_lo, pad_hi, n_active, h2d, g.reshape(1, d), rowp, seg, seg)


def _expert_kernel(be_ref, na_ref, x_ref, w1_ref, w3_ref, w2_ref, y_ref):
    i = pl.program_id(0)
    f = pl.program_id(1)

    @pl.when(i < na_ref[0])
    def _():
        x = x_ref[...].astype(BF16)
        t = _silu(jnp.dot(x, w1_ref[0], preferred_element_type=F32))
        t = t * jnp.dot(x, w3_ref[0], preferred_element_type=F32)
        part = jnp.dot(t.astype(BF16), w2_ref[0], preferred_element_type=F32)

        @pl.when(f == 0)
        def _():
            y_ref[...] = part

        @pl.when(f > 0)
        def _():
            y_ref[...] += part

    @pl.when((i >= na_ref[0]) & (f == 0))
    def _():
        y_ref[...] = jnp.zeros(y_ref.shape, F32)


def _experts(xs, blk_expert, n_active, w1, w3, w2, *, tf):
    r, d = xs.shape
    tm = EXPERT_TILE
    ff = w1.shape[2]
    nf = ff // tf

    def row_blk(i, f, be, na):
        return (jnp.maximum(jnp.minimum(i, na[0] - 1), 0), 0)

    def f_blk(i, f, na):
        return jnp.where(i < na[0], f, nf - 1)

    grid_spec = pltpu.PrefetchScalarGridSpec(
        num_scalar_prefetch=2,
        grid=(r // tm, nf),
        in_specs=[
            pl.BlockSpec((tm, d), row_blk),
            pl.BlockSpec((1, d, tf), lambda i, f, be, na: (be[i], 0, f_blk(i, f, na))),
            pl.BlockSpec((1, d, tf), lambda i, f, be, na: (be[i], 0, f_blk(i, f, na))),
            pl.BlockSpec((1, tf, d), lambda i, f, be, na: (be[i], f_blk(i, f, na), 0)),
        ],
        out_specs=pl.BlockSpec((tm, d), lambda i, f, be, na: (i, 0)),
    )
    return pl.pallas_call(
        _expert_kernel,
        out_shape=jax.ShapeDtypeStruct((r, d), F32),
        grid_spec=grid_spec,
        compiler_params=pltpu.CompilerParams(
            dimension_semantics=("arbitrary", "arbitrary"),
            vmem_limit_bytes=VMEM_LIMIT),
        name="moe_experts",
    )(blk_expert, n_active, xs, w1, w3, w2)


def _combine_kernel(off_ref, h_ref, col_ref, seg_ref, seg_next_ref, y_ref, gf_ref, o_ref,
                    ysort, sem, *, apply_final_norm):
    tb = ROUTE_TILE
    i = pl.program_id(0)
    slot = i % 2

    def copier(buf):
        def make_copy(tile_row, group_row, rows):
            return pltpu.make_async_copy(y_ref.at[pl.ds(group_row, rows)],
                                         ysort.at[buf, pl.ds(tile_row, rows)], sem.at[buf])
        return make_copy

    @pl.when(i == 0)
    def _():
        ysort[...] = jnp.zeros(ysort.shape, F32)
        _segment_copies(seg_ref, off_ref, copier(0), True)

    @pl.when(i + 1 < pl.num_programs(0))
    def _():
        _segment_copies(seg_next_ref, off_ref, copier(1 - slot), True)

    _segment_copies(seg_ref, off_ref, copier(slot), False)

    col = col_ref[...]
    row = lax.broadcasted_iota(jnp.int32, (tb, SORT_ROWS), 1)
    yb = ysort[slot].astype(BF16)
    acc = None
    for k in range(2):
        pick = jnp.where(row == col[:, 2 + k:3 + k].astype(jnp.int32), 1.0, 0.0).astype(BF16)
        term = col[:, k:k + 1] * jnp.dot(pick, yb, preferred_element_type=F32)
        acc = term if acc is None else acc + term
    out = h_ref[...] + acc
    o_ref[...] = _rms(out, gf_ref[...]) if apply_final_norm else out


def _combine(h2d, col, seg, y, off, final_g, *, apply_final_norm):
    t, d = h2d.shape
    tb = ROUTE_TILE
    n_tiles = t // tb
    seg_spec = lambda index: pl.BlockSpec((1, SUBLANES, LANES), index, memory_space=pltpu.SMEM)
    grid_spec = pltpu.PrefetchScalarGridSpec(
        num_scalar_prefetch=1,
        grid=(n_tiles,),
        in_specs=[
            pl.BlockSpec((tb, d), lambda i, off: (i, 0)),
            pl.BlockSpec((tb, LANES), lambda i, off: (i, 0)),
            seg_spec(lambda i, off: (i, 0, 0)),
            seg_spec(lambda i, off: (jnp.minimum(i + 1, n_tiles - 1), 0, 0)),
            pl.BlockSpec(memory_space=pl.ANY),
            pl.BlockSpec((1, d), lambda i, off: (0, 0)),
        ],
        out_specs=pl.BlockSpec((tb, d), lambda i, off: (i, 0)),
        scratch_shapes=[pltpu.VMEM((2, SORT_ROWS, d), F32), pltpu.SemaphoreType.DMA((2,))],
    )
    return pl.pallas_call(
        functools.partial(_combine_kernel, apply_final_norm=apply_final_norm),
        out_shape=jax.ShapeDtypeStruct((t, d), F32),
        grid_spec=grid_spec,
        compiler_params=pltpu.CompilerParams(
            dimension_semantics=("arbitrary",), vmem_limit_bytes=VMEM_LIMIT),
        name="moe_combine",
    )(off, h2d, col, seg, seg, y, final_g.reshape(1, d))


def _moe(h2d, g, router_w, w1, w3, w2, layer, final_g, *, apply_final_norm):
    t, d = h2d.shape
    tm = EXPERT_TILE
    n_tiles = t // ROUTE_TILE
    n_rows = 2 * t + n_tiles * N_EXPERTS * SUBLANES + N_EXPERTS * tm
    n_rows = -(-n_rows // tm) * tm
    col, rowp, seg, cnt = _router(h2d, g, router_w)
    counts = cnt[0, :N_EXPERTS].astype(jnp.int32)
    blocks = (counts + tm - 1) // tm
    ends = jnp.cumsum(blocks)
    off = (ends - blocks) * tm
    n_active = ends[-1:]
    blk = jnp.arange(n_rows // tm, dtype=jnp.int32)
    blk_expert = jnp.sum(jnp.minimum(blk, n_active - 1)[:, None] >= ends[None, :], axis=1)
    blk_expert = blk_expert.astype(jnp.int32) + layer * N_EXPERTS
    xs = _dispatch(h2d, g, rowp, seg, off, (off + counts) // SUBLANES,
                   (off + blocks * tm) // SUBLANES, n_active, n_rows)
    y = _experts(xs, blk_expert, n_active, w1, w3, w2, tf=1792)
    return _combine(h2d, col, seg, y, off, final_g, apply_final_norm=apply_final_norm)


def kernel(x, e_norm1, e_w_in, e_conv_w, e_conv_b, e_ln_g, e_ln_b, e_pool_w, e_pool_scale,
           e_w_out, e_norm2, e_ff_w1, e_ff_w3, e_ff_w2, o_norm1, o_w_qkv, o_b_qkv, o_sinks,
           o_w_o, o_b_o, o_norm2, o_router, o_exp_w1, o_exp_w3, o_exp_w2, final_norm):
    b, s, d = x.shape
    assert DEPTH % 2 == 0, "the final norm is fused into the last (odd, MoE) layer"
    expert_w = [w.astype(BF16).reshape((-1,) + w.shape[2:]) for w in (o_exp_w1, o_exp_w3, o_exp_w2)]
    h = x
    for layer in range(DEPTH):
        i = layer // 2
        if layer % 2 == 0:
            h = _even_mixer(h, e_norm1[i], e_w_in[i], e_conv_w[i], e_conv_b[i], e_ln_g[i],
                            e_ln_b[i], e_pool_w[i], e_pool_scale[i], e_w_out[i])
            h = _ffn(h.reshape(b * s, d), e_norm2[i], e_ff_w1[i], e_ff_w3[i], e_ff_w2[i],
                     tm=512, tf=1408).reshape(b, s, d)
        else:
            h = _attn_mixer(h, o_norm1[i], o_w_qkv[i], o_b_qkv[i], o_sinks[i], o_w_o[i],
                            o_b_o[i])
            h = _moe(h.reshape(b * s, d), o_norm2[i], o_router[i], *expert_w, i, final_norm,
                     apply_final_norm=(layer == DEPTH - 1)).reshape(b, s, d)
    return h
```

```python
import functools

import jax
import jax.numpy as jnp
from jax import lax
from jax.experimental import pallas as pl
from jax.experimental.pallas import tpu as pltpu

F32 = jnp.float32
BF16 = jnp.bfloat16

D_MODEL = 1024
DEPTH = 4
RMS_EPS = 1e-5
LN_EPS = 1e-5

D_CONV = 512
D_POOL = 512
CONV_K = 31
POOL_WINDOWS = (2, 4, 8, 16)
POOL_GROUP = 128
D_IN_EVEN = 2 * D_CONV + D_POOL

HEAD_DIM = 64
N_Q_HEADS = 16
N_KV_HEADS = 4
ATTN_BLOCK = 128
D_Q = N_Q_HEADS * HEAD_DIM
D_KV = N_KV_HEADS * HEAD_DIM
D_QKV = D_Q + 2 * D_KV

N_EXPERTS = 8

LANES = 128
SUBLANES = 8
SEQ_TILE = 512
HALO = 32
CONV_ROWS = 32
VMEM_LIMIT = 56 * 1024 * 1024


def _rms(x, g):
    ms = jnp.mean(x * x, axis=-1, keepdims=True)
    return x * lax.rsqrt(ms + RMS_EPS) * g


def _silu(x):
    return x * jax.nn.sigmoid(x)


def _even_mixer_kernel(x_ref, g_ref, win_ref, cw_ref, cb_ref, lg_ref, lb_ref,
                       pw_ref, ps_ref, wout_ref, o_ref,
                       abuf, ashift, cwb, bbuf, p1, p2, p3, cat):
    s = pl.program_id(1)
    ts = SEQ_TILE
    rows = HALO + ts

    @pl.when(s == 0)
    def _():
        abuf[0:HALO, :] = jnp.zeros((HALO, D_CONV), F32)
        bbuf[0:HALO, :] = jnp.zeros((HALO, D_POOL), F32)

    x = x_ref[0]
    hn = _rms(x, g_ref[...]).astype(BF16)
    u = jnp.dot(hn, win_ref[...], preferred_element_type=F32)
    abuf[HALO:rows, :] = u[:, :D_CONV] * jax.nn.sigmoid(u[:, D_CONV:2 * D_CONV])
    bbuf[HALO:rows, :] = u[:, 2 * D_CONV:]

    a_all = abuf[...]
    for r in range(1, SUBLANES):
        ashift[r - 1] = pltpu.roll(a_all, rows - r, axis=0)
    for k in range(CONV_K):
        cwb[k] = jnp.broadcast_to(cw_ref[k:k + 1, :], (SUBLANES, D_CONV))
    for c in range(ts // CONV_ROWS):
        r0 = c * CONV_ROWS
        acc = jnp.broadcast_to(cb_ref[...], (CONV_ROWS, D_CONV))
        for k in range(CONV_K):
            q8, r = divmod(HALO - (CONV_K - 1) + k, SUBLANES)
            src = abuf if r == 0 else ashift.at[r - 1]
            start = r0 + q8 * SUBLANES
            tap = jnp.concatenate([cwb[k]] * (CONV_ROWS // SUBLANES), axis=0)
            acc = acc + tap * src[start:start + CONV_ROWS, :]
        mu = jnp.mean(acc, axis=-1, keepdims=True)
        d = acc - mu
        var = jnp.mean(d * d, axis=-1, keepdims=True)
        y = d * lax.rsqrt(var + LN_EPS) * lg_ref[...] + lb_ref[...]
        cat[r0:r0 + CONV_ROWS, 0:D_CONV] = _silu(y).astype(BF16)

    p1[8:rows, :] = bbuf[8:rows, :] + bbuf[7:rows - 1, :]
    p2[16:rows, 0:384] = p1[16:rows, 128:512] + p1[14:rows - 2, 128:512]
    p3[24:rows, 0:256] = p2[24:rows, 128:384] + p2[20:rows - 4, 128:384]
    s16 = p3[HALO:rows, 128:256] + p3[HALO - 8:rows - 8, 128:256]
    sums = (p1[HALO:rows, 0:128], p2[HALO:rows, 0:128], p3[HALO:rows, 0:128], s16)
    pos1 = (s * ts + 1 + lax.broadcasted_iota(jnp.int32, (ts, 1), 0)).astype(F32)
    for g, w in enumerate(POOL_WINDOWS):
        cnt = jnp.minimum(pos1, float(w))
        pg = sums[g] / cnt - bbuf[HALO:rows, g * POOL_GROUP:(g + 1) * POOL_GROUP]
        pm = jnp.dot(pg.astype(BF16), pw_ref[g], preferred_element_type=F32)
        pm = pm * ps_ref[:, g * POOL_GROUP:(g + 1) * POOL_GROUP]
        cat[:, D_CONV + g * POOL_GROUP:D_CONV + (g + 1) * POOL_GROUP] = pm.astype(BF16)

    o_ref[0] = x + jnp.dot(cat[...], wout_ref[...], preferred_element_type=F32)

    abuf[0:HALO, :] = abuf[ts:rows, :]
    bbuf[0:HALO, :] = bbuf[ts:rows, :]


def _even_mixer(h, g, w_in, conv_w, conv_b, ln_g, ln_b, pool_w, pool_scale, w_out):
    b, s, d = h.shape
    ts = SEQ_TILE
    rows = HALO + ts
    const = lambda shape: pl.BlockSpec(shape, lambda i, j: (0,) * len(shape))
    return pl.pallas_call(
        _even_mixer_kernel,
        out_shape=jax.ShapeDtypeStruct(h.shape, F32),
        grid=(b, s // ts),
        in_specs=[
            pl.BlockSpec((1, ts, d), lambda i, j: (i, j, 0)),
            const((1, d)),
            const((d, D_IN_EVEN)),
            const((CONV_K, D_CONV)),
            const((1, D_CONV)),
            const((1, D_CONV)),
            const((1, D_CONV)),
            const((len(POOL_WINDOWS), POOL_GROUP, POOL_GROUP)),
            const((1, D_POOL)),
            const((d, d)),
        ],
        out_specs=pl.BlockSpec((1, ts, d), lambda i, j: (i, j, 0)),
        scratch_shapes=[
            pltpu.VMEM((rows, D_CONV), F32),
            pltpu.VMEM((SUBLANES - 1, rows, D_CONV), F32),
            pltpu.VMEM((CONV_K, SUBLANES, D_CONV), F32),
            pltpu.VMEM((rows, D_POOL), F32),
            pltpu.VMEM((rows, D_POOL), F32),
            pltpu.VMEM((rows, 384), F32),
            pltpu.VMEM((rows, 256), F32),
            pltpu.VMEM((ts, d), BF16),
        ],
        compiler_params=pltpu.CompilerParams(
            dimension_semantics=("arbitrary", "arbitrary"),
            vmem_limit_bytes=VMEM_LIMIT),
        name="even_mixer",
    )(h, g.reshape(1, d), w_in.astype(BF16), conv_w, conv_b.reshape(1, -1),
      ln_g.reshape(1, -1), ln_b.reshape(1, -1), pool_w.astype(BF16),
      pool_scale.reshape(1, -1), w_out.astype(BF16))


def _attn_kernel(sink_ref, x_ref, g_ref, wqkv_ref, bqkv_ref, wot_ref, bo_ref, o_ref,
                 kbuf, vtbuf, otbuf):
    s = pl.program_id(1)
    ts = SEQ_TILE
    blk = ATTN_BLOCK
    group = N_Q_HEADS // N_KV_HEADS

    @pl.when(s == 0)
    def _():
        kbuf[0:blk, :] = jnp.zeros((blk, N_KV_HEADS * LANES), BF16)
        vtbuf[:, 0:blk] = jnp.zeros((D_KV, blk), BF16)

    x = x_ref[0]
    hn = _rms(x, g_ref[...]).astype(BF16)
    u = jnp.dot(hn, wqkv_ref[...], preferred_element_type=F32) + bqkv_ref[...]
    q = (u[:, :D_Q] * (HEAD_DIM ** -0.5)).astype(BF16)

    lo_t = lax.broadcasted_iota(jnp.int32, (ts, LANES), 1) < HEAD_DIM
    for p in range(N_KV_HEADS // 2):
        t = u[:, D_Q + p * LANES:D_Q + (p + 1) * LANES]
        r = pltpu.roll(t, HEAD_DIM, axis=1)
        kbuf[blk:blk + ts, (2 * p) * LANES:(2 * p + 1) * LANES] = jnp.where(lo_t, t, r).astype(BF16)
        kbuf[blk:blk + ts, (2 * p + 1) * LANES:(2 * p + 2) * LANES] = jnp.where(lo_t, r, t).astype(BF16)
    vtbuf[:, blk:blk + ts] = jnp.transpose(u[:, D_Q + D_KV:]).astype(BF16)

    lo_b = lax.broadcasted_iota(jnp.int32, (blk, LANES), 1) < HEAD_DIM
    zero_b = jnp.zeros((blk, LANES), BF16)
    key = lax.broadcasted_iota(jnp.int32, (2 * blk, blk), 0)
    qry = lax.broadcasted_iota(jnp.int32, (2 * blk, blk), 1)
    band = (key > qry) & (key <= qry + blk)
    band0 = band & ((key >= blk) | (s > 0))
    neg = jnp.concatenate([jnp.where(band, 0.0, -jnp.inf)] * group, axis=1)
    neg0 = jnp.concatenate([jnp.where(band0, 0.0, -jnp.inf)] * group, axis=1)

    for j in range(ts // blk):
        mask_add = neg0 if j == 0 else neg
        for g in range(N_KV_HEADS):
            kk = kbuf[j * blk:(j + 2) * blk, g * LANES:(g + 1) * LANES]
            vt = vtbuf[g * HEAD_DIM:(g + 1) * HEAD_DIM, j * blk:(j + 2) * blk]
            qs = []
            sk = []
            for i in range(group):
                h = g * group + i
                qt = q[j * blk:(j + 1) * blk, (h // 2) * LANES:(h // 2 + 1) * LANES]
                qs.append(jnp.where(lo_b, qt, zero_b) if h % 2 == 0 else jnp.where(lo_b, zero_b, qt))
                sk.append(jnp.full((1, blk), sink_ref[h], F32))
            q4 = jnp.concatenate(qs, axis=0)
            sink = jnp.concatenate(sk, axis=1)
            st = lax.dot_general(kk, q4, (((1,), (1,)), ((), ())),
                                 preferred_element_type=F32) + mask_add
            m = jnp.maximum(jnp.max(st, axis=0, keepdims=True), sink)
            e = jnp.exp(st - m)
            denom = jnp.sum(e, axis=0, keepdims=True) + jnp.exp(sink - m)
            ot = jnp.dot(vt, e.astype(BF16), preferred_element_type=F32) * (1.0 / denom)
            for i in range(group):
                h = g * group + i
                otbuf[h * HEAD_DIM:(h + 1) * HEAD_DIM, j * blk:(j + 1) * blk] = (
                    ot[:, i * blk:(i + 1) * blk].astype(BF16))

    proj_t = jnp.dot(wot_ref[...], otbuf[...], preferred_element_type=F32)
    o_ref[0] = x + jnp.transpose(proj_t) + bo_ref[...]

    kbuf[0:blk, :] = kbuf[ts:ts + blk, :]
    vtbuf[:, 0:blk] = vtbuf[:, ts:ts + blk]


def _attn_mixer(h, g, w_qkv, b_qkv, sinks, w_o, b_o):
    b, s, d = h.shape
    ts = SEQ_TILE
    const = lambda shape: pl.BlockSpec(shape, lambda i, j, sk: (0,) * len(shape))
    grid_spec = pltpu.PrefetchScalarGridSpec(
        num_scalar_prefetch=1,
        grid=(b, s // ts),
        in_specs=[
            pl.BlockSpec((1, ts, d), lambda i, j, sk: (i, j, 0)),
            const((1, d)),
            const((d, D_QKV)),
            const((1, D_QKV)),
            const((d, D_Q)),
            const((1, d)),
        ],
        out_specs=pl.BlockSpec((1, ts, d), lambda i, j, sk: (i, j, 0)),
        scratch_shapes=[
            pltpu.VMEM((ATTN_BLOCK + ts, N_KV_HEADS * LANES), BF16),
            pltpu.VMEM((D_KV, ATTN_BLOCK + ts), BF16),
            pltpu.VMEM((D_Q, ts), BF16),
        ],
    )
    return pl.pallas_call(
        _attn_kernel,
        out_shape=jax.ShapeDtypeStruct(h.shape, F32),
        grid_spec=grid_spec,
        compiler_params=pltpu.CompilerParams(
            dimension_semantics=("arbitrary", "arbitrary"),
            vmem_limit_bytes=VMEM_LIMIT),
        name="attn_mixer",
    )(sinks, h, g.reshape(1, d), w_qkv.astype(BF16), b_qkv.reshape(1, -1),
      jnp.transpose(w_o).astype(BF16), b_o.reshape(1, d))


def _ffn_kernel(x_ref, g_ref, w1_ref, w3_ref, w2_ref, o_ref, *, tf):
    x = x_ref[...]
    hn = _rms(x, g_ref[...]).astype(BF16)
    acc = x
    for f0 in range(0, w1_ref.shape[1], tf):
        t = _silu(jnp.dot(hn, w1_ref[:, f0:f0 + tf], preferred_element_type=F32))
        t = t * jnp.dot(hn, w3_ref[:, f0:f0 + tf], preferred_element_type=F32)
        acc = acc + jnp.dot(t.astype(BF16), w2_ref[f0:f0 + tf, :], preferred_element_type=F32)
    o_ref[...] = acc


def _ffn(h2d, g, w1, w3, w2, *, tm, tf):
    t, d = h2d.shape
    ff = w1.shape[1]
    resident = lambda shape: pl.BlockSpec(shape, lambda i: (0, 0), pipeline_mode=pl.Buffered(1))
    return pl.pallas_call(
        functools.partial(_ffn_kernel, tf=tf),
        out_shape=jax.ShapeDtypeStruct((t, d), F32),
        grid=(t // tm,),
        in_specs=[
            pl.BlockSpec((tm, d), lambda i: (i, 0)),
            pl.BlockSpec((1, d), lambda i: (0, 0)),
            resident((d, ff)),
            resident((d, ff)),
            resident((ff, d)),
        ],
        out_specs=pl.BlockSpec((tm, d), lambda i: (i, 0)),
        compiler_params=pltpu.CompilerParams(
            dimension_semantics=("arbitrary",), vmem_limit_bytes=VMEM_LIMIT),
        name="dense_ffn",
    )(h2d, g.reshape(1, d), w1.astype(BF16), w3.astype(BF16), w2.astype(BF16))


ROUTE_TILE = 512
EXPERT_TILE = 512
SORT_ROWS = 2 * ROUTE_TILE + N_EXPERTS * SUBLANES
SEG_CHUNKS = tuple(SUBLANES << j for j in range((ROUTE_TILE // SUBLANES).bit_length() - 1, -1, -1))


def _router_kernel(x_ref, g_ref, rw_ref, col_ref, rowp_ref, seg_ref, cnt_ref, carry):
    i = pl.program_id(0)
    tb = ROUTE_TILE

    @pl.when(i == 0)
    def _():
        carry[...] = jnp.zeros(carry.shape, F32)

    hn = _rms(x_ref[...], g_ref[...])
    rw = rw_ref[...]
    hn_hi = hn.astype(BF16)
    hn_lo = (hn - hn_hi.astype(F32)).astype(BF16)
    rw_hi = rw.astype(BF16)
    rw_lo = (rw - rw_hi.astype(F32)).astype(BF16)
    logits = jnp.dot(hn_hi, rw_hi, preferred_element_type=F32) + (
        jnp.dot(hn_lo, rw_hi, preferred_element_type=F32)
        + jnp.dot(hn_hi, rw_lo, preferred_element_type=F32))
    lane = lax.broadcasted_iota(jnp.int32, (tb, LANES), 1)
    lg = jnp.where(lane < N_EXPERTS, logits, -jnp.inf)
    m1 = jnp.max(lg, axis=-1, keepdims=True)
    i1 = jnp.min(jnp.where(lg == m1, lane, LANES), axis=-1, keepdims=True)
    lg2 = jnp.where(lane == i1, -jnp.inf, lg)
    m2 = jnp.max(lg2, axis=-1, keepdims=True)
    i2 = jnp.min(jnp.where(lg2 == m2, lane, LANES), axis=-1, keepdims=True)
    e2 = jnp.exp(m2 - m1)
    den = 1.0 + e2

    mem_f = jnp.where((lane == i1) | (lane == i2), 1.0, 0.0)
    rr = lax.broadcasted_iota(jnp.int32, (tb, tb), 0)
    cc = lax.broadcasted_iota(jnp.int32, (tb, tb), 1)
    lower = jnp.where(cc < rr, 1.0, 0.0).astype(BF16)
    before = jnp.dot(lower, mem_f.astype(BF16), preferred_element_type=F32)

    n = jnp.broadcast_to(jnp.sum(mem_f, axis=0, keepdims=True), (SUBLANES, LANES))
    n8 = jnp.floor((n + (SUBLANES - 1)) / SUBLANES) * SUBLANES
    incl = n8
    for sh in (1, 2, 4):
        incl = incl + pltpu.roll(incl, sh, axis=1)
    seg8 = incl - n8
    pos = before + seg8[0:1, :]
    rp1 = jnp.sum(jnp.where(lane == i1, pos, 0.0), axis=-1, keepdims=True)
    rp2 = jnp.sum(jnp.where(lane == i2, pos, 0.0), axis=-1, keepdims=True)

    col_ref[...] = jnp.where(lane == 0, 1.0 / den,
                             jnp.where(lane == 1, e2 / den,
                                       jnp.where(lane == 2, rp1, jnp.where(lane == 3, rp2, 0.0))))
    table = jnp.where(lane == 0, rp1, jnp.where(lane == 1, rp2, 0.0))
    rowp_ref[0] = jnp.transpose(table)[0:SUBLANES, :].astype(jnp.int32)
    sub = lax.broadcasted_iota(jnp.int32, (SUBLANES, LANES), 0)
    seg_ref[0] = jnp.where(sub == 0, n8, jnp.where(sub == 1, seg8,
                                                   jnp.where(sub == 2, carry[...], 0.0))).astype(jnp.int32)
    carry[...] = carry[...] + n8
    cnt_ref[...] = carry[...]


def _router(h2d, g, router_w):
    t, d = h2d.shape
    tb = ROUTE_TILE
    rw = jnp.zeros((d, LANES), F32).at[:, :N_EXPERTS].set(router_w)
    return pl.pallas_call(
        _router_kernel,
        out_shape=(
            jax.ShapeDtypeStruct((t, LANES), F32),
            jax.ShapeDtypeStruct((t // tb, SUBLANES, tb), jnp.int32),
            jax.ShapeDtypeStruct((t // tb, SUBLANES, LANES), jnp.int32),
            jax.ShapeDtypeStruct((SUBLANES, LANES), F32),
        ),
        grid=(t // tb,),
        in_specs=[
            pl.BlockSpec((tb, d), lambda i: (i, 0)),
            pl.BlockSpec((1, d), lambda i: (0, 0)),
            pl.BlockSpec((d, LANES), lambda i: (0, 0)),
        ],
        out_specs=(
            pl.BlockSpec((tb, LANES), lambda i: (i, 0)),
            pl.BlockSpec((1, SUBLANES, tb), lambda i: (i, 0, 0)),
            pl.BlockSpec((1, SUBLANES, LANES), lambda i: (i, 0, 0)),
            pl.BlockSpec((SUBLANES, LANES), lambda i: (0, 0)),
        ),
        scratch_shapes=[pltpu.VMEM((SUBLANES, LANES), F32)],
        compiler_params=pltpu.CompilerParams(
            dimension_semantics=("arbitrary",), vmem_limit_bytes=VMEM_LIMIT),
        name="moe_router",
    )(h2d, g.reshape(1, d), rw)


def _segment_copies(seg_ref, off_ref, make_copy, do_start):
    for e in range(N_EXPERTS):
        n8 = seg_ref[0, 0, e]
        tile_row = seg_ref[0, 1, e]
        group_row = off_ref[e] + seg_ref[0, 2, e]
        for c in SEG_CHUNKS:
            done = n8 & ~(2 * c - 1)

            @pl.when((n8 & c) != 0)
            def _():
                cp = make_copy(pl.multiple_of(tile_row + done, SUBLANES),
                               pl.multiple_of(group_row + done, SUBLANES), c)
                cp.start() if do_start else cp.wait()


def _dispatch_kernel(off_ref, pad_lo_ref, pad_hi_ref, na_ref, h_ref, g_ref, rowp_ref, seg_ref,
                     seg_prev_ref, xs_ref, sorted_s, zblk, sem, zsem):
    tb = ROUTE_TILE
    tm = EXPERT_TILE
    i = pl.program_id(0)
    slot = i % 2

    @pl.when(i == 0)
    def _():
        zblk[...] = jnp.zeros(zblk.shape, F32)
        n_blocks = xs_ref.shape[0] // tm

        def fill(do_start):
            def pad_group(r, c):
                cp = pltpu.make_async_copy(
                    zblk.at[pl.ds(0, SUBLANES)],
                    xs_ref.at[pl.ds(pl.multiple_of(r * SUBLANES, SUBLANES), SUBLANES)], zsem)
                cp.start() if do_start else cp.wait()
                return c

            def tail_blk(b, c):
                cp = pltpu.make_async_copy(
                    zblk, xs_ref.at[pl.ds(pl.multiple_of(b * tm, tm), tm)], zsem)
                cp.start() if do_start else cp.wait()
                return c

            for e in range(N_EXPERTS):
                lax.fori_loop(pad_lo_ref[e], pad_hi_ref[e], pad_group, 0)
            lax.fori_loop(na_ref[0], n_blocks, tail_blk, 0)

        fill(True)
        fill(False)

    hn = _rms(h_ref[...], g_ref[...]).astype(BF16)
    row = lax.broadcasted_iota(jnp.int32, (SORT_ROWS, tb), 0)
    hit = (row == rowp_ref[0, 0:1, :]) | (row == rowp_ref[0, 1:2, :])
    sorted_s[slot] = jnp.dot(jnp.where(hit, 1.0, 0.0).astype(BF16), hn, preferred_element_type=F32)

    def copier(buf):
        def make_copy(tile_row, group_row, rows):
            return pltpu.make_async_copy(sorted_s.at[buf, pl.ds(tile_row, rows)],
                                         xs_ref.at[pl.ds(group_row, rows)], sem.at[buf])
        return make_copy

    _segment_copies(seg_ref, off_ref, copier(slot), True)

    @pl.when(i > 0)
    def _():
        _segment_copies(seg_prev_ref, off_ref, copier(1 - slot), False)

    @pl.when(i == pl.num_programs(0) - 1)
    def _():
        _segment_copies(seg_ref, off_ref, copier(slot), False)


def _dispatch(h2d, g, rowp, seg, off, pad_lo, pad_hi, n_active, n_rows):
    t, d = h2d.shape
    tb = ROUTE_TILE
    seg_spec = lambda index: pl.BlockSpec((1, SUBLANES, LANES), index, memory_space=pltpu.SMEM)
    grid_spec = pltpu.PrefetchScalarGridSpec(
        num_scalar_prefetch=4,
        grid=(t // tb,),
        in_specs=[
            pl.BlockSpec((tb, d), lambda i, *_: (i, 0)),
            pl.BlockSpec((1, d), lambda i, *_: (0, 0)),
            pl.BlockSpec((1, SUBLANES, tb), lambda i, *_: (i, 0, 0)),
            seg_spec(lambda i, *_: (i, 0, 0)),
            seg_spec(lambda i, *_: (jnp.maximum(i - 1, 0), 0, 0)),
        ],
        out_specs=pl.BlockSpec(memory_space=pl.ANY),
        scratch_shapes=[pltpu.VMEM((2, SORT_ROWS, d), F32),
                        pltpu.VMEM((EXPERT_TILE, d), F32),
                        pltpu.SemaphoreType.DMA((2,)),
                        pltpu.SemaphoreType.DMA(())],
    )
    return pl.pallas_call(
        _dispatch_kernel,
        out_shape=jax.ShapeDtypeStruct((n_rows, d), F32),
        grid_spec=grid_spec,
        compiler_params=pltpu.CompilerParams(
            dimension_semantics=("arbitrary",), has_side_effects=True,
            vmem_limit_bytes=VMEM_LIMIT),
        name="moe_dispatch",
    )(off, pad_lo, pad_hi, n_active, h2d, g.reshape(1, d), rowp, seg, seg)


def _expert_kernel(be_ref, na_ref, vr_ref, x_ref, w1_ref, w3_ref, w2_ref, y_ref):
    i = pl.program_id(0)
    f = pl.program_id(1)
    tm = y_ref.shape[0]

    def swiglu_rows(rows):
        x = x_ref[0:rows, :].astype(BF16)
        t = _silu(jnp.dot(x, w1_ref[0], preferred_element_type=F32))
        t = t * jnp.dot(x, w3_ref[0], preferred_element_type=F32)
        part = jnp.dot(t.astype(BF16), w2_ref[0], preferred_element_type=F32)

        @pl.when(f == 0)
        def _():
            y_ref[0:rows, :] = part
            if rows < tm:
                y_ref[rows:tm, :] = jnp.zeros((tm - rows, y_ref.shape[1]), F32)

        @pl.when(f > 0)
        def _():
            y_ref[0:rows, :] += part

    active = i < na_ref[0]
    half_full = vr_ref[i] <= tm // 2

    @pl.when(active & jnp.logical_not(half_full))
    def _():
        swiglu_rows(tm)

    @pl.when(active & half_full)
    def _():
        swiglu_rows(tm // 2)

    @pl.when(jnp.logical_not(active) & (f == 0))
    def _():
        y_ref[...] = jnp.zeros(y_ref.shape, F32)


def _experts(xs, blk_expert, n_active, valid_rows, w1, w3, w2, *, tf):
    r, d = xs.shape
    tm = EXPERT_TILE
    ff = w1.shape[2]
    nf = ff // tf

    def row_blk(i, f, be, na, vr):
        return (jnp.maximum(jnp.minimum(i, na[0] - 1), 0), 0)

    def f_blk(i, f, na):
        return jnp.where(i < na[0], f, nf - 1)

    grid_spec = pltpu.PrefetchScalarGridSpec(
        num_scalar_prefetch=3,
        grid=(r // tm, nf),
        in_specs=[
            pl.BlockSpec((tm, d), row_blk),
            pl.BlockSpec((1, d, tf), lambda i, f, be, na, vr: (be[i], 0, f_blk(i, f, na))),
            pl.BlockSpec((1, d, tf), lambda i, f, be, na, vr: (be[i], 0, f_blk(i, f, na))),
            pl.BlockSpec((1, tf, d), lambda i, f, be, na, vr: (be[i], f_blk(i, f, na), 0)),
        ],
        out_specs=pl.BlockSpec((tm, d), lambda i, f, be, na, vr: (i, 0)),
    )
    return pl.pallas_call(
        _expert_kernel,
        out_shape=jax.ShapeDtypeStruct((r, d), F32),
        grid_spec=grid_spec,
        compiler_params=pltpu.CompilerParams(
            dimension_semantics=("arbitrary", "arbitrary"),
            vmem_limit_bytes=VMEM_LIMIT),
        name="moe_experts",
    )(blk_expert, n_active, valid_rows, xs, w1, w3, w2)


def _combine_kernel(off_ref, h_ref, col_ref, seg_ref, seg_next_ref, y_ref, gf_ref, o_ref,
                    ysort, sem, *, apply_final_norm):
    tb = ROUTE_TILE
    i = pl.program_id(0)
    slot = i % 2

    def copier(buf):
        def make_copy(tile_row, group_row, rows):
            return pltpu.make_async_copy(y_ref.at[pl.ds(group_row, rows)],
                                         ysort.at[buf, pl.ds(tile_row, rows)], sem.at[buf])
        return make_copy

    @pl.when(i == 0)
    def _():
        ysort[...] = jnp.zeros(ysort.shape, F32)
        _segment_copies(seg_ref, off_ref, copier(0), True)

    @pl.when(i + 1 < pl.num_programs(0))
    def _():
        _segment_copies(seg_next_ref, off_ref, copier(1 - slot), True)

    _segment_copies(seg_ref, off_ref, copier(slot), False)

    col = col_ref[...]
    row = lax.broadcasted_iota(jnp.int32, (tb, SORT_ROWS), 1)
    yb = ysort[slot].astype(BF16)
    acc = None
    for k in range(2):
        pick = jnp.where(row == col[:, 2 + k:3 + k].astype(jnp.int32), 1.0, 0.0).astype(BF16)
        term = col[:, k:k + 1] * jnp.dot(pick, yb, preferred_element_type=F32)
        acc = term if acc is None else acc + term
    out = h_ref[...] + acc
    o_ref[...] = _rms(out, gf_ref[...]) if apply_final_norm else out


def _combine(h2d, col, seg, y, off, final_g, *, apply_final_norm):
    t, d = h2d.shape
    tb = ROUTE_TILE
    n_tiles = t // tb
    seg_spec = lambda index: pl.BlockSpec((1, SUBLANES, LANES), index, memory_space=pltpu.SMEM)
    grid_spec = pltpu.PrefetchScalarGridSpec(
        num_scalar_prefetch=1,
        grid=(n_tiles,),
        in_specs=[
            pl.BlockSpec((tb, d), lambda i, off: (i, 0)),
            pl.BlockSpec((tb, LANES), lambda i, off: (i, 0)),
            seg_spec(lambda i, off: (i, 0, 0)),
            seg_spec(lambda i, off: (jnp.minimum(i + 1, n_tiles - 1), 0, 0)),
            pl.BlockSpec(memory_space=pl.ANY),
            pl.BlockSpec((1, d), lambda i, off: (0, 0)),
        ],
        out_specs=pl.BlockSpec((tb, d), lambda i, off: (i, 0)),
        scratch_shapes=[pltpu.VMEM((2, SORT_ROWS, d), F32), pltpu.SemaphoreType.DMA((2,))],
    )
    return pl.pallas_call(
        functools.partial(_combine_kernel, apply_final_norm=apply_final_norm),
        out_shape=jax.ShapeDtypeStruct((t, d), F32),
        grid_spec=grid_spec,
        compiler_params=pltpu.CompilerParams(
            dimension_semantics=("arbitrary",), vmem_limit_bytes=VMEM_LIMIT),
        name="moe_combine",
    )(off, h2d, col, seg, seg, y, final_g.reshape(1, d))


def _moe(h2d, g, router_w, w1, w3, w2, layer, final_g, *, apply_final_norm):
    t, d = h2d.shape
    tm = EXPERT_TILE
    n_tiles = t // ROUTE_TILE
    n_rows = 2 * t + n_tiles * N_EXPERTS * SUBLANES + N_EXPERTS * tm
    n_rows = -(-n_rows // tm) * tm
    col, rowp, seg, cnt = _router(h2d, g, router_w)
    counts = cnt[0, :N_EXPERTS].astype(jnp.int32)
    blocks = (counts + tm - 1) // tm
    ends = jnp.cumsum(blocks)
    off = (ends - blocks) * tm
    n_active = ends[-1:]
    blk = jnp.arange(n_rows // tm, dtype=jnp.int32)
    blk_expert = jnp.sum(jnp.minimum(blk, n_active - 1)[:, None] >= ends[None, :], axis=1)
    blk_expert = blk_expert.astype(jnp.int32)
    valid_rows = jnp.clip(counts[blk_expert] - (blk - (ends - blocks)[blk_expert]) * tm, 0, tm)
    blk_expert = blk_expert + layer * N_EXPERTS
    xs = _dispatch(h2d, g, rowp, seg, off, (off + counts) // SUBLANES,
                   (off + blocks * tm) // SUBLANES, n_active, n_rows)
    y = _experts(xs, blk_expert, n_active, valid_rows, w1, w3, w2, tf=1792)
    return _combine(h2d, col, seg, y, off, final_g, apply_final_norm=apply_final_norm)


def kernel(x, e_norm1, e_w_in, e_conv_w, e_conv_b, e_ln_g, e_ln_b, e_pool_w, e_pool_scale,
           e_w_out, e_norm2, e_ff_w1, e_ff_w3, e_ff_w2, o_norm1, o_w_qkv, o_b_qkv, o_sinks,
           o_w_o, o_b_o, o_norm2, o_router, o_exp_w1, o_exp_w3, o_exp_w2, final_norm):
    b, s, d = x.shape
    assert DEPTH % 2 == 0, "the final norm is fused into the last (odd, MoE) layer"
    expert_w = [w.astype(BF16).reshape((-1,) + w.shape[2:]) for w in (o_exp_w1, o_exp_w3, o_exp_w2)]
    h = x
    for layer in range(DEPTH):
        i = layer // 2
        if layer % 2 == 0:
            h = _even_mixer(h, e_norm1[i], e_w_in[i], e_conv_w[i], e_conv_b[i], e_ln_g[i],
                            e_ln_b[i], e_pool_w[i], e_pool_scale[i], e_w_out[i])
            h = _ffn(h.reshape(b * s, d), e_norm2[i], e_ff_w1[i], e_ff_w3[i], e_ff_w2[i],
                     tm=1024, tf=1408).reshape(b, s, d)
        else:
            h = _attn_mixer(h, o_norm1[i], o_w_qkv[i], o_b_qkv[i], o_sinks[i], o_w_o[i],
                            o_b_o[i])
            h = _moe(h.reshape(b * s, d), o_norm2[i], o_router[i], *expert_w, i, final_norm,
                     apply_final_norm=(layer == DEPTH - 1)).reshape(b, s, d)
    return h
```

```python
import functools

import jax
import jax.numpy as jnp
from jax import lax
from jax.experimental import pallas as pl
from jax.experimental.pallas import tpu as pltpu

F32 = jnp.float32
BF16 = jnp.bfloat16

D_MODEL = 1024
DEPTH = 4
RMS_EPS = 1e-5
LN_EPS = 1e-5

D_CONV = 512
D_POOL = 512
CONV_K = 31
POOL_WINDOWS = (2, 4, 8, 16)
POOL_GROUP = 128
D_IN_EVEN = 2 * D_CONV + D_POOL

HEAD_DIM = 64
N_Q_HEADS = 16
N_KV_HEADS = 4
ATTN_BLOCK = 128
D_Q = N_Q_HEADS * HEAD_DIM
D_KV = N_KV_HEADS * HEAD_DIM
D_QKV = D_Q + 2 * D_KV

N_EXPERTS = 8

LANES = 128
SUBLANES = 8
SEQ_TILE = 512
HALO = 32
CONV_ROWS = 32
SCORE_LOOKAHEAD = 2
FFN_LOOKAHEAD = 2
VMEM_LIMIT = 56 * 1024 * 1024


def _rms(x, g):
    ms = jnp.mean(x * x, axis=-1, keepdims=True)
    return x * lax.rsqrt(ms + RMS_EPS) * g


def _silu(x):
    return x * jax.nn.sigmoid(x)


def _even_mixer_kernel(x_ref, g_ref, win_ref, cw_ref, cb_ref, lg_ref, lb_ref,
                       pw_ref, ps_ref, wout_ref, o_ref,
                       abuf, ashift, cwb, bbuf, p1, p2, p3, cat):
    s = pl.program_id(1)
    ts = SEQ_TILE
    rows = HALO + ts

    @pl.when(s == 0)
    def _():
        abuf[0:HALO, :] = jnp.zeros((HALO, D_CONV), F32)
        bbuf[0:HALO, :] = jnp.zeros((HALO, D_POOL), F32)

    x = x_ref[0]
    hn = _rms(x, g_ref[...]).astype(BF16)
    u = jnp.dot(hn, win_ref[...], preferred_element_type=F32)
    abuf[HALO:rows, :] = u[:, :D_CONV] * jax.nn.sigmoid(u[:, D_CONV:2 * D_CONV])
    bbuf[HALO:rows, :] = u[:, 2 * D_CONV:]

    a_all = abuf[...]
    for r in range(1, SUBLANES):
        ashift[r - 1] = pltpu.roll(a_all, rows - r, axis=0)
    for k in range(CONV_K):
        cwb[k] = jnp.broadcast_to(cw_ref[k:k + 1, :], (SUBLANES, D_CONV))
    for c in range(ts // CONV_ROWS):
        r0 = c * CONV_ROWS
        acc = jnp.broadcast_to(cb_ref[...], (CONV_ROWS, D_CONV))
        for k in range(CONV_K):
            q8, r = divmod(HALO - (CONV_K - 1) + k, SUBLANES)
            src = abuf if r == 0 else ashift.at[r - 1]
            start = r0 + q8 * SUBLANES
            tap = jnp.concatenate([cwb[k]] * (CONV_ROWS // SUBLANES), axis=0)
            acc = acc + tap * src[start:start + CONV_ROWS, :]
        mu = jnp.mean(acc, axis=-1, keepdims=True)
        d = acc - mu
        var = jnp.mean(d * d, axis=-1, keepdims=True)
        y = d * lax.rsqrt(var + LN_EPS) * lg_ref[...] + lb_ref[...]
        cat[r0:r0 + CONV_ROWS, 0:D_CONV] = _silu(y).astype(BF16)

    p1[8:rows, :] = bbuf[8:rows, :] + bbuf[7:rows - 1, :]
    p2[16:rows, 0:384] = p1[16:rows, 128:512] + p1[14:rows - 2, 128:512]
    p3[24:rows, 0:256] = p2[24:rows, 128:384] + p2[20:rows - 4, 128:384]
    s16 = p3[HALO:rows, 128:256] + p3[HALO - 8:rows - 8, 128:256]
    sums = (p1[HALO:rows, 0:128], p2[HALO:rows, 0:128], p3[HALO:rows, 0:128], s16)
    pos1 = (s * ts + 1 + lax.broadcasted_iota(jnp.int32, (ts, 1), 0)).astype(F32)
    for g, w in enumerate(POOL_WINDOWS):
        cnt = jnp.minimum(pos1, float(w))
        pg = sums[g] / cnt - bbuf[HALO:rows, g * POOL_GROUP:(g + 1) * POOL_GROUP]
        pm = jnp.dot(pg.astype(BF16), pw_ref[g], preferred_element_type=F32)
        pm = pm * ps_ref[:, g * POOL_GROUP:(g + 1) * POOL_GROUP]
        cat[:, D_CONV + g * POOL_GROUP:D_CONV + (g + 1) * POOL_GROUP] = pm.astype(BF16)

    o_ref[0] = x + jnp.dot(cat[...], wout_ref[...], preferred_element_type=F32)

    abuf[0:HALO, :] = abuf[ts:rows, :]
    bbuf[0:HALO, :] = bbuf[ts:rows, :]


def _even_mixer(h, g, w_in, conv_w, conv_b, ln_g, ln_b, pool_w, pool_scale, w_out):
    b, s, d = h.shape
    ts = SEQ_TILE
    rows = HALO + ts
    const = lambda shape: pl.BlockSpec(shape, lambda i, j: (0,) * len(shape))
    return pl.pallas_call(
        _even_mixer_kernel,
        out_shape=jax.ShapeDtypeStruct(h.shape, F32),
        grid=(b, s // ts),
        in_specs=[
            pl.BlockSpec((1, ts, d), lambda i, j: (i, j, 0)),
            const((1, d)),
            const((d, D_IN_EVEN)),
            const((CONV_K, D_CONV)),
            const((1, D_CONV)),
            const((1, D_CONV)),
            const((1, D_CONV)),
            const((len(POOL_WINDOWS), POOL_GROUP, POOL_GROUP)),
            const((1, D_POOL)),
            const((d, d)),
        ],
        out_specs=pl.BlockSpec((1, ts, d), lambda i, j: (i, j, 0)),
        scratch_shapes=[
            pltpu.VMEM((rows, D_CONV), F32),
            pltpu.VMEM((SUBLANES - 1, rows, D_CONV), F32),
            pltpu.VMEM((CONV_K, SUBLANES, D_CONV), F32),
            pltpu.VMEM((rows, D_POOL), F32),
            pltpu.VMEM((rows, D_POOL), F32),
            pltpu.VMEM((rows, 384), F32),
            pltpu.VMEM((rows, 256), F32),
            pltpu.VMEM((ts, d), BF16),
        ],
        compiler_params=pltpu.CompilerParams(
            dimension_semantics=("arbitrary", "arbitrary"),
            vmem_limit_bytes=VMEM_LIMIT),
        name="even_mixer",
    )(h, g.reshape(1, d), w_in.astype(BF16), conv_w, conv_b.reshape(1, -1),
      ln_g.reshape(1, -1), ln_b.reshape(1, -1), pool_w.astype(BF16),
      pool_scale.reshape(1, -1), w_out.astype(BF16))


def _attn_kernel(sink_ref, x_ref, g_ref, wqkv_ref, bqkv_ref, wot_ref, bo_ref, o_ref,
                 kbuf, vtbuf, otbuf):
    s = pl.program_id(1)
    ts = SEQ_TILE
    blk = ATTN_BLOCK
    group = N_Q_HEADS // N_KV_HEADS

    @pl.when(s == 0)
    def _():
        kbuf[0:blk, :] = jnp.zeros((blk, N_KV_HEADS * LANES), BF16)
        vtbuf[:, 0:blk] = jnp.zeros((D_KV, blk), BF16)

    x = x_ref[0]
    hn = _rms(x, g_ref[...]).astype(BF16)
    u = jnp.dot(hn, wqkv_ref[...], preferred_element_type=F32) + bqkv_ref[...]
    q = (u[:, :D_Q] * (HEAD_DIM ** -0.5)).astype(BF16)

    lo_t = lax.broadcasted_iota(jnp.int32, (ts, LANES), 1) < HEAD_DIM
    for p in range(N_KV_HEADS // 2):
        t = u[:, D_Q + p * LANES:D_Q + (p + 1) * LANES]
        r = pltpu.roll(t, HEAD_DIM, axis=1)
        kbuf[blk:blk + ts, (2 * p) * LANES:(2 * p + 1) * LANES] = jnp.where(lo_t, t, r).astype(BF16)
        kbuf[blk:blk + ts, (2 * p + 1) * LANES:(2 * p + 2) * LANES] = jnp.where(lo_t, r, t).astype(BF16)
    vtbuf[:, blk:blk + ts] = jnp.transpose(u[:, D_Q + D_KV:]).astype(BF16)

    lo_b = lax.broadcasted_iota(jnp.int32, (blk, LANES), 1) < HEAD_DIM
    zero_b = jnp.zeros((blk, LANES), BF16)
    key = lax.broadcasted_iota(jnp.int32, (2 * blk, blk), 0)
    qry = lax.broadcasted_iota(jnp.int32, (2 * blk, blk), 1)
    band = (key > qry) & (key <= qry + blk)
    band0 = band & ((key >= blk) | (s > 0))
    neg = jnp.concatenate([jnp.where(band, 0.0, -jnp.inf)] * group, axis=1)
    neg0 = jnp.concatenate([jnp.where(band0, 0.0, -jnp.inf)] * group, axis=1)

    def scores(j, g):
        kk = kbuf[j * blk:(j + 2) * blk, g * LANES:(g + 1) * LANES]
        qs = []
        sk = []
        for i in range(group):
            h = g * group + i
            qt = q[j * blk:(j + 1) * blk, (h // 2) * LANES:(h // 2 + 1) * LANES]
            qs.append(jnp.where(lo_b, qt, zero_b) if h % 2 == 0 else jnp.where(lo_b, zero_b, qt))
            sk.append(jnp.full((1, blk), sink_ref[h], F32))
        q4 = jnp.concatenate(qs, axis=0)
        st = lax.dot_general(kk, q4, (((1,), (1,)), ((), ())), preferred_element_type=F32)
        return st + (neg0 if j == 0 else neg), jnp.concatenate(sk, axis=1)

    pairs = [(j, g) for j in range(ts // blk) for g in range(N_KV_HEADS)]
    ahead = [scores(*p) for p in pairs[:SCORE_LOOKAHEAD]]
    for idx, (j, g) in enumerate(pairs):
        st, sink = ahead.pop(0)
        if idx + SCORE_LOOKAHEAD < len(pairs):
            ahead.append(scores(*pairs[idx + SCORE_LOOKAHEAD]))
        vt = vtbuf[g * HEAD_DIM:(g + 1) * HEAD_DIM, j * blk:(j + 2) * blk]
        m = jnp.maximum(jnp.max(st, axis=0, keepdims=True), sink)
        e = jnp.exp(st - m)
        denom = jnp.sum(e, axis=0, keepdims=True) + jnp.exp(sink - m)
        ot = jnp.dot(vt, e.astype(BF16), preferred_element_type=F32) * (1.0 / denom)
        for i in range(group):
            h = g * group + i
            otbuf[h * HEAD_DIM:(h + 1) * HEAD_DIM, j * blk:(j + 1) * blk] = (
                ot[:, i * blk:(i + 1) * blk].astype(BF16))

    proj_t = jnp.dot(wot_ref[...], otbuf[...], preferred_element_type=F32)
    o_ref[0] = x + jnp.transpose(proj_t) + bo_ref[...]

    kbuf[0:blk, :] = kbuf[ts:ts + blk, :]
    vtbuf[:, 0:blk] = vtbuf[:, ts:ts + blk]


def _attn_mixer(h, g, w_qkv, b_qkv, sinks, w_o, b_o):
    b, s, d = h.shape
    ts = SEQ_TILE
    const = lambda shape: pl.BlockSpec(shape, lambda i, j, sk: (0,) * len(shape))
    grid_spec = pltpu.PrefetchScalarGridSpec(
        num_scalar_prefetch=1,
        grid=(b, s // ts),
        in_specs=[
            pl.BlockSpec((1, ts, d), lambda i, j, sk: (i, j, 0)),
            const((1, d)),
            const((d, D_QKV)),
            const((1, D_QKV)),
            const((d, D_Q)),
            const((1, d)),
        ],
        out_specs=pl.BlockSpec((1, ts, d), lambda i, j, sk: (i, j, 0)),
        scratch_shapes=[
            pltpu.VMEM((ATTN_BLOCK + ts, N_KV_HEADS * LANES), BF16),
            pltpu.VMEM((D_KV, ATTN_BLOCK + ts), BF16),
            pltpu.VMEM((D_Q, ts), BF16),
        ],
    )
    return pl.pallas_call(
        _attn_kernel,
        out_shape=jax.ShapeDtypeStruct(h.shape, F32),
        grid_spec=grid_spec,
        compiler_params=pltpu.CompilerParams(
            dimension_semantics=("arbitrary", "arbitrary"),
            vmem_limit_bytes=VMEM_LIMIT),
        name="attn_mixer",
    )(sinks, h, g.reshape(1, d), w_qkv.astype(BF16), b_qkv.reshape(1, -1),
      jnp.transpose(w_o).astype(BF16), b_o.reshape(1, d))


def _ffn_kernel(x_ref, g_ref, w1_ref, w3_ref, w2_ref, o_ref, hid, *, tf):
    x = x_ref[...]
    hn = _rms(x, g_ref[...]).astype(BF16)

    def up(f0):
        return (jnp.dot(hn, w1_ref[:, f0:f0 + tf], preferred_element_type=F32),
                jnp.dot(hn, w3_ref[:, f0:f0 + tf], preferred_element_type=F32))

    starts = list(range(0, w1_ref.shape[1], tf))
    ahead = [up(f0) for f0 in starts[:FFN_LOOKAHEAD]]
    for idx, f0 in enumerate(starts):
        h1, h3 = ahead.pop(0)
        if idx + FFN_LOOKAHEAD < len(starts):
            ahead.append(up(starts[idx + FFN_LOOKAHEAD]))
        hid[:, f0:f0 + tf] = (_silu(h1) * h3).astype(BF16)
    o_ref[...] = x + jnp.dot(hid[...], w2_ref[...], preferred_element_type=F32)


def _ffn(h2d, g, w1, w3, w2, *, tm, tf):
    t, d = h2d.shape
    ff = w1.shape[1]
    resident = lambda shape: pl.BlockSpec(shape, lambda i: (0, 0), pipeline_mode=pl.Buffered(1))
    return pl.pallas_call(
        functools.partial(_ffn_kernel, tf=tf),
        out_shape=jax.ShapeDtypeStruct((t, d), F32),
        grid=(t // tm,),
        in_specs=[
            pl.BlockSpec((tm, d), lambda i: (i, 0)),
            pl.BlockSpec((1, d), lambda i: (0, 0)),
            resident((d, ff)),
            resident((d, ff)),
            resident((ff, d)),
        ],
        out_specs=pl.BlockSpec((tm, d), lambda i: (i, 0)),
        scratch_shapes=[pltpu.VMEM((tm, ff), BF16)],
        compiler_params=pltpu.CompilerParams(
            dimension_semantics=("arbitrary",), vmem_limit_bytes=VMEM_LIMIT),
        name="dense_ffn",
    )(h2d, g.reshape(1, d), w1.astype(BF16), w3.astype(BF16), w2.astype(BF16))


ROUTE_TILE = 512
EXPERT_TILE = 512
EXPERT_UP_CHUNK = 256
SORT_ROWS = 2 * ROUTE_TILE + N_EXPERTS * SUBLANES
SEG_CHUNKS = tuple(SUBLANES << j for j in range((ROUTE_TILE // SUBLANES).bit_length() - 1, -1, -1))


def _router_kernel(x_ref, g_ref, rw_ref, col_ref, rowp_ref, seg_ref, cnt_ref, carry):
    i = pl.program_id(0)
    tb = ROUTE_TILE

    @pl.when(i == 0)
    def _():
        carry[...] = jnp.zeros(carry.shape, F32)

    hn = _rms(x_ref[...], g_ref[...])
    rw = rw_ref[...]
    hn_hi = hn.astype(BF16)
    hn_lo = (hn - hn_hi.astype(F32)).astype(BF16)
    rw_hi = rw.astype(BF16)
    rw_lo = (rw - rw_hi.astype(F32)).astype(BF16)
    logits = jnp.dot(hn_hi, rw_hi, preferred_element_type=F32) + (
        jnp.dot(hn_lo, rw_hi, preferred_element_type=F32)
        + jnp.dot(hn_hi, rw_lo, preferred_element_type=F32))
    lane = lax.broadcasted_iota(jnp.int32, (tb, LANES), 1)
    lg = jnp.where(lane < N_EXPERTS, logits, -jnp.inf)
    m1 = jnp.max(lg, axis=-1, keepdims=True)
    i1 = jnp.min(jnp.where(lg == m1, lane, LANES), axis=-1, keepdims=True)
    lg2 = jnp.where(lane == i1, -jnp.inf, lg)
    m2 = jnp.max(lg2, axis=-1, keepdims=True)
    i2 = jnp.min(jnp.where(lg2 == m2, lane, LANES), axis=-1, keepdims=True)
    e2 = jnp.exp(m2 - m1)
    den = 1.0 + e2

    mem_f = jnp.where((lane == i1) | (lane == i2), 1.0, 0.0)
    rr = lax.broadcasted_iota(jnp.int32, (tb, tb), 0)
    cc = lax.broadcasted_iota(jnp.int32, (tb, tb), 1)
    lower = jnp.where(cc < rr, 1.0, 0.0).astype(BF16)
    before = jnp.dot(lower, mem_f.astype(BF16), preferred_element_type=F32)

    n = jnp.broadcast_to(jnp.sum(mem_f, axis=0, keepdims=True), (SUBLANES, LANES))
    n8 = jnp.floor((n + (SUBLANES - 1)) / SUBLANES) * SUBLANES
    incl = n8
    for sh in (1, 2, 4):
        incl = incl + pltpu.roll(incl, sh, axis=1)
    seg8 = incl - n8
    pos = before + seg8[0:1, :]
    rp1 = jnp.sum(jnp.where(lane == i1, pos, 0.0), axis=-1, keepdims=True)
    rp2 = jnp.sum(jnp.where(lane == i2, pos, 0.0), axis=-1, keepdims=True)

    col_ref[...] = jnp.where(lane == 0, 1.0 / den,
                             jnp.where(lane == 1, e2 / den,
                                       jnp.where(lane == 2, rp1, jnp.where(lane == 3, rp2, 0.0))))
    table = jnp.where(lane == 0, rp1, jnp.where(lane == 1, rp2, 0.0))
    rowp_ref[0] = jnp.transpose(table)[0:SUBLANES, :].astype(jnp.int32)
    sub = lax.broadcasted_iota(jnp.int32, (SUBLANES, LANES), 0)
    seg_ref[0] = jnp.where(sub == 0, n8, jnp.where(sub == 1, seg8,
                                                   jnp.where(sub == 2, carry[...], 0.0))).astype(jnp.int32)
    carry[...] = carry[...] + n8
    cnt_ref[...] = carry[...]


def _router(h2d, g, router_w):
    t, d = h2d.shape
    tb = ROUTE_TILE
    rw = jnp.zeros((d, LANES), F32).at[:, :N_EXPERTS].set(router_w)
    return pl.pallas_call(
        _router_kernel,
        out_shape=(
            jax.ShapeDtypeStruct((t, LANES), F32),
            jax.ShapeDtypeStruct((t // tb, SUBLANES, tb), jnp.int32),
            jax.ShapeDtypeStruct((t // tb, SUBLANES, LANES), jnp.int32),
            jax.ShapeDtypeStruct((SUBLANES, LANES), F32),
        ),
        grid=(t // tb,),
        in_specs=[
            pl.BlockSpec((tb, d), lambda i: (i, 0)),
            pl.BlockSpec((1, d), lambda i: (0, 0)),
            pl.BlockSpec((d, LANES), lambda i: (0, 0)),
        ],
        out_specs=(
            pl.BlockSpec((tb, LANES), lambda i: (i, 0)),
            pl.BlockSpec((1, SUBLANES, tb), lambda i: (i, 0, 0)),
            pl.BlockSpec((1, SUBLANES, LANES), lambda i: (i, 0, 0)),
            pl.BlockSpec((SUBLANES, LANES), lambda i: (0, 0)),
        ),
        scratch_shapes=[pltpu.VMEM((SUBLANES, LANES), F32)],
        compiler_params=pltpu.CompilerParams(
            dimension_semantics=("arbitrary",), vmem_limit_bytes=VMEM_LIMIT),
        name="moe_router",
    )(h2d, g.reshape(1, d), rw)


def _segment_copies(seg_ref, off_ref, make_copy, do_start):
    for e in range(N_EXPERTS):
        n8 = seg_ref[0, 0, e]
        tile_row = seg_ref[0, 1, e]
        group_row = off_ref[e] + seg_ref[0, 2, e]
        for c in SEG_CHUNKS:
            done = n8 & ~(2 * c - 1)

            @pl.when((n8 & c) != 0)
            def _():
                cp = make_copy(pl.multiple_of(tile_row + done, SUBLANES),
                               pl.multiple_of(group_row + done, SUBLANES), c)
                cp.start() if do_start else cp.wait()


def _dispatch_kernel(off_ref, pad_lo_ref, pad_hi_ref, na_ref, h_ref, g_ref, rowp_ref, seg_ref,
                     seg_prev_ref, xs_ref, sorted_s, zblk, sem, zsem):
    tb = ROUTE_TILE
    tm = EXPERT_TILE
    i = pl.program_id(0)
    slot = i % 2

    @pl.when(i == 0)
    def _():
        zblk[...] = jnp.zeros(zblk.shape, F32)
        n_blocks = xs_ref.shape[0] // tm

        def fill(do_start):
            def pad_group(r, c):
                cp = pltpu.make_async_copy(
                    zblk.at[pl.ds(0, SUBLANES)],
                    xs_ref.at[pl.ds(pl.multiple_of(r * SUBLANES, SUBLANES), SUBLANES)], zsem)
                cp.start() if do_start else cp.wait()
                return c

            def tail_blk(b, c):
                cp = pltpu.make_async_copy(
                    zblk, xs_ref.at[pl.ds(pl.multiple_of(b * tm, tm), tm)], zsem)
                cp.start() if do_start else cp.wait()
                return c

            for e in range(N_EXPERTS):
                lax.fori_loop(pad_lo_ref[e], pad_hi_ref[e], pad_group, 0)
            lax.fori_loop(na_ref[0], n_blocks, tail_blk, 0)

        fill(True)
        fill(False)

    hn = _rms(h_ref[...], g_ref[...]).astype(BF16)
    row = lax.broadcasted_iota(jnp.int32, (SORT_ROWS, tb), 0)
    hit = (row == rowp_ref[0, 0:1, :]) | (row == rowp_ref[0, 1:2, :])
    sorted_s[slot] = jnp.dot(jnp.where(hit, 1.0, 0.0).astype(BF16), hn, preferred_element_type=F32)

    def copier(buf):
        def make_copy(tile_row, group_row, rows):
            return pltpu.make_async_copy(sorted_s.at[buf, pl.ds(tile_row, rows)],
                                         xs_ref.at[pl.ds(group_row, rows)], sem.at[buf])
        return make_copy

    _segment_copies(seg_ref, off_ref, copier(slot), True)

    @pl.when(i > 0)
    def _():
        _segment_copies(seg_prev_ref, off_ref, copier(1 - slot), False)

    @pl.when(i == pl.num_programs(0) - 1)
    def _():
        _segment_copies(seg_ref, off_ref, copier(slot), False)


def _dispatch(h2d, g, rowp, seg, off, pad_lo, pad_hi, n_active, n_rows):
    t, d = h2d.shape
    tb = ROUTE_TILE
    seg_spec = lambda index: pl.BlockSpec((1, SUBLANES, LANES), index, memory_space=pltpu.SMEM)
    grid_spec = pltpu.PrefetchScalarGridSpec(
        num_scalar_prefetch=4,
        grid=(t // tb,),
        in_specs=[
            pl.BlockSpec((tb, d), lambda i, *_: (i, 0)),
            pl.BlockSpec((1, d), lambda i, *_: (0, 0)),
            pl.BlockSpec((1, SUBLANES, tb), lambda i, *_: (i, 0, 0)),
            seg_spec(lambda i, *_: (i, 0, 0)),
            seg_spec(lambda i, *_: (jnp.maximum(i - 1, 0), 0, 0)),
        ],
        out_specs=pl.BlockSpec(memory_space=pl.ANY),
        scratch_shapes=[pltpu.VMEM((2, SORT_ROWS, d), F32),
                        pltpu.VMEM((EXPERT_TILE, d), F32),
                        pltpu.SemaphoreType.DMA((2,)),
                        pltpu.SemaphoreType.DMA(())],
    )
    return pl.pallas_call(
        _dispatch_kernel,
        out_shape=jax.ShapeDtypeStruct((n_rows, d), F32),
        grid_spec=grid_spec,
        compiler_params=pltpu.CompilerParams(
            dimension_semantics=("arbitrary",), has_side_effects=True,
            vmem_limit_bytes=VMEM_LIMIT),
        name="moe_dispatch",
    )(off, pad_lo, pad_hi, n_active, h2d, g.reshape(1, d), rowp, seg, seg)


def _expert_kernel(be_ref, na_ref, vr_ref, x_ref, w1_ref, w3_ref, w2_ref, y_ref, hid):
    i = pl.program_id(0)
    f = pl.program_id(1)
    tm = y_ref.shape[0]

    def swiglu_rows(rows):
        x = x_ref[0:rows, :].astype(BF16)

        def up(f0):
            return (jnp.dot(x, w1_ref[0, :, f0:f0 + EXPERT_UP_CHUNK], preferred_element_type=F32),
                    jnp.dot(x, w3_ref[0, :, f0:f0 + EXPERT_UP_CHUNK], preferred_element_type=F32))

        starts = list(range(0, w1_ref.shape[2], EXPERT_UP_CHUNK))
        ahead = [up(f0) for f0 in starts[:FFN_LOOKAHEAD]]
        for idx, f0 in enumerate(starts):
            h1, h3 = ahead.pop(0)
            if idx + FFN_LOOKAHEAD < len(starts):
                ahead.append(up(starts[idx + FFN_LOOKAHEAD]))
            hid[0:rows, f0:f0 + EXPERT_UP_CHUNK] = (_silu(h1) * h3).astype(BF16)
        part = jnp.dot(hid[0:rows, :], w2_ref[0], preferred_element_type=F32)

        @pl.when(f == 0)
        def _():
            y_ref[0:rows, :] = part
            if rows < tm:
                y_ref[rows:tm, :] = jnp.zeros((tm - rows, y_ref.shape[1]), F32)

        @pl.when(f > 0)
        def _():
            y_ref[0:rows, :] += part

    active = i < na_ref[0]
    half_full = vr_ref[i] <= tm // 2

    @pl.when(active & jnp.logical_not(half_full))
    def _():
        swiglu_rows(tm)

    @pl.when(active & half_full)
    def _():
        swiglu_rows(tm // 2)

    @pl.when(jnp.logical_not(active) & (f == 0))
    def _():
        y_ref[...] = jnp.zeros(y_ref.shape, F32)


def _experts(xs, blk_expert, n_active, valid_rows, w1, w3, w2, *, tf):
    r, d = xs.shape
    tm = EXPERT_TILE
    ff = w1.shape[2]
    nf = ff // tf

    def row_blk(i, f, be, na, vr):
        return (jnp.maximum(jnp.minimum(i, na[0] - 1), 0), 0)

    def f_blk(i, f, na):
        return jnp.where(i < na[0], f, nf - 1)

    grid_spec = pltpu.PrefetchScalarGridSpec(
        num_scalar_prefetch=3,
        grid=(r // tm, nf),
        in_specs=[
            pl.BlockSpec((tm, d), row_blk),
            pl.BlockSpec((1, d, tf), lambda i, f, be, na, vr: (be[i], 0, f_blk(i, f, na))),
            pl.BlockSpec((1, d, tf), lambda i, f, be, na, vr: (be[i], 0, f_blk(i, f, na))),
            pl.BlockSpec((1, tf, d), lambda i, f, be, na, vr: (be[i], f_blk(i, f, na), 0)),
        ],
        out_specs=pl.BlockSpec((tm, d), lambda i, f, be, na, vr: (i, 0)),
        scratch_shapes=[pltpu.VMEM((tm, tf), BF16)],
    )
    return pl.pallas_call(
        _expert_kernel,
        out_shape=jax.ShapeDtypeStruct((r, d), F32),
        grid_spec=grid_spec,
        compiler_params=pltpu.CompilerParams(
            dimension_semantics=("arbitrary", "arbitrary"),
            vmem_limit_bytes=VMEM_LIMIT),
        name="moe_experts",
    )(blk_expert, n_active, valid_rows, xs, w1, w3, w2)


def _combine_kernel(off_ref, h_ref, col_ref, seg_ref, seg_next_ref, y_ref, gf_ref, o_ref,
                    ysort, sem, *, apply_final_norm):
    tb = ROUTE_TILE
    i = pl.program_id(0)
    slot = i % 2

    def copier(buf):
        def make_copy(tile_row, group_row, rows):
            return pltpu.make_async_copy(y_ref.at[pl.ds(group_row, rows)],
                                         ysort.at[buf, pl.ds(tile_row, rows)], sem.at[buf])
        return make_copy

    @pl.when(i == 0)
    def _():
        ysort[...] = jnp.zeros(ysort.shape, F32)
        _segment_copies(seg_ref, off_ref, copier(0), True)

    @pl.when(i + 1 < pl.num_programs(0))
    def _():
        _segment_copies(seg_next_ref, off_ref, copier(1 - slot), True)

    _segment_copies(seg_ref, off_ref, copier(slot), False)

    col = col_ref[...]
    row = lax.broadcasted_iota(jnp.int32, (tb, SORT_ROWS), 1)
    yb = ysort[slot].astype(BF16)
    acc = None
    for k in range(2):
        pick = jnp.where(row == col[:, 2 + k:3 + k].astype(jnp.int32), 1.0, 0.0).astype(BF16)
        term = col[:, k:k + 1] * jnp.dot(pick, yb, preferred_element_type=F32)
        acc = term if acc is None else acc + term
    out = h_ref[...] + acc
    o_ref[...] = _rms(out, gf_ref[...]) if apply_final_norm else out


def _combine(h2d, col, seg, y, off, final_g, *, apply_final_norm):
    t, d = h2d.shape
    tb = ROUTE_TILE
    n_tiles = t // tb
    seg_spec = lambda index: pl.BlockSpec((1, SUBLANES, LANES), index, memory_space=pltpu.SMEM)
    grid_spec = pltpu.PrefetchScalarGridSpec(
        num_scalar_prefetch=1,
        grid=(n_tiles,),
        in_specs=[
            pl.BlockSpec((tb, d), lambda i, off: (i, 0)),
            pl.BlockSpec((tb, LANES), lambda i, off: (i, 0)),
            seg_spec(lambda i, off: (i, 0, 0)),
            seg_spec(lambda i, off: (jnp.minimum(i + 1, n_tiles - 1), 0, 0)),
            pl.BlockSpec(memory_space=pl.ANY),
            pl.BlockSpec((1, d), lambda i, off: (0, 0)),
        ],
        out_specs=pl.BlockSpec((tb, d), lambda i, off: (i, 0)),
        scratch_shapes=[pltpu.VMEM((2, SORT_ROWS, d), F32), pltpu.SemaphoreType.DMA((2,))],
    )
    return pl.pallas_call(
        functools.partial(_combine_kernel, apply_final_norm=apply_final_norm),
        out_shape=jax.ShapeDtypeStruct((t, d), F32),
        grid_spec=grid_spec,
        compiler_params=pltpu.CompilerParams(
            dimension_semantics=("arbitrary",), vmem_limit_bytes=VMEM_LIMIT),
        name="moe_combine",
    )(off, h2d, col, seg, seg, y, final_g.reshape(1, d))


def _moe(h2d, g, router_w, w1, w3, w2, layer, final_g, *, apply_final_norm):
    t, d = h2d.shape
    tm = EXPERT_TILE
    n_tiles = t // ROUTE_TILE
    n_rows = 2 * t + n_tiles * N_EXPERTS * SUBLANES + N_EXPERTS * tm
    n_rows = -(-n_rows // tm) * tm
    col, rowp, seg, cnt = _router(h2d, g, router_w)
    counts = cnt[0, :N_EXPERTS].astype(jnp.int32)
    blocks = (counts + tm - 1) // tm
    ends = jnp.cumsum(blocks)
    off = (ends - blocks) * tm
    n_active = ends[-1:]
    blk = jnp.arange(n_rows // tm, dtype=jnp.int32)
    blk_expert = jnp.sum(jnp.minimum(blk, n_active - 1)[:, None] >= ends[None, :], axis=1)
    blk_expert = blk_expert.astype(jnp.int32)
    valid_rows = jnp.clip(counts[blk_expert] - (blk - (ends - blocks)[blk_expert]) * tm, 0, tm)
    blk_expert = blk_expert + layer * N_EXPERTS
    xs = _dispatch(h2d, g, rowp, seg, off, (off + counts) // SUBLANES,
                   (off + blocks * tm) // SUBLANES, n_active, n_rows)
    y = _experts(xs, blk_expert, n_active, valid_rows, w1, w3, w2, tf=1792)
    return _combine(h2d, col, seg, y, off, final_g, apply_final_norm=apply_final_norm)


def kernel(x, e_norm1, e_w_in, e_conv_w, e_conv_b, e_ln_g, e_ln_b, e_pool_w, e_pool_scale,
           e_w_out, e_norm2, e_ff_w1, e_ff_w3, e_ff_w2, o_norm1, o_w_qkv, o_b_qkv, o_sinks,
           o_w_o, o_b_o, o_norm2, o_router, o_exp_w1, o_exp_w3, o_exp_w2, final_norm):
    b, s, d = x.shape
    assert DEPTH % 2 == 0, "the final norm is fused into the last (odd, MoE) layer"
    expert_w = [w.astype(BF16).reshape((-1,) + w.shape[2:]) for w in (o_exp_w1, o_exp_w3, o_exp_w2)]
    h = x
    for layer in range(DEPTH):
        i = layer // 2
        if layer % 2 == 0:
            h = _even_mixer(h, e_norm1[i], e_w_in[i], e_conv_w[i], e_conv_b[i], e_ln_g[i],
                            e_ln_b[i], e_pool_w[i], e_pool_scale[i], e_w_out[i])
            h = _ffn(h.reshape(b * s, d), e_norm2[i], e_ff_w1[i], e_ff_w3[i], e_ff_w2[i],
                     tm=1024, tf=256).reshape(b, s, d)
        else:
            h = _attn_mixer(h, o_norm1[i], o_w_qkv[i], o_b_qkv[i], o_sinks[i], o_w_o[i],
                            o_b_o[i])
            h = _moe(h.reshape(b * s, d), o_norm2[i], o_router[i], *expert_w, i, final_norm,
                     apply_final_norm=(layer == DEPTH - 1)).reshape(b, s, d)
    return h
```

```python
import functools

import jax
import jax.numpy as jnp
from jax import lax
from jax.experimental import pallas as pl
from jax.experimental.pallas import tpu as pltpu

F32 = jnp.float32
BF16 = jnp.bfloat16

D_MODEL = 1024
DEPTH = 4
RMS_EPS = 1e-5
LN_EPS = 1e-5

D_CONV = 512
D_POOL = 512
CONV_K = 31
POOL_WINDOWS = (2, 4, 8, 16)
POOL_GROUP = 128
D_IN_EVEN = 2 * D_CONV + D_POOL

HEAD_DIM = 64
N_Q_HEADS = 16
N_KV_HEADS = 4
ATTN_BLOCK = 128
D_Q = N_Q_HEADS * HEAD_DIM
D_KV = N_KV_HEADS * HEAD_DIM
D_QKV = D_Q + 2 * D_KV

N_EXPERTS = 8

LANES = 128
SUBLANES = 8
SEQ_TILE = 512
HALO = 32
CONV_ROWS = 32
SCORE_LOOKAHEAD = 2
FFN_LOOKAHEAD = 2
VMEM_LIMIT = 56 * 1024 * 1024


def _rms(x, g):
    ms = jnp.mean(x * x, axis=-1, keepdims=True)
    return x * lax.rsqrt(ms + RMS_EPS) * g


def _silu(x):
    return x * jax.nn.sigmoid(x)


def _even_mixer_kernel(x_ref, g_ref, win_ref, cw_ref, cb_ref, lg_ref, lb_ref,
                       pw_ref, ps_ref, wout_ref, o_ref,
                       abuf, ashift, cwb, bbuf, p1, p2, p3, cat):
    s = pl.program_id(1)
    ts = SEQ_TILE
    rows = HALO + ts

    @pl.when(s == 0)
    def _():
        abuf[0:HALO, :] = jnp.zeros((HALO, D_CONV), F32)
        bbuf[0:HALO, :] = jnp.zeros((HALO, D_POOL), F32)

    x = x_ref[0]
    hn = _rms(x, g_ref[...]).astype(BF16)
    u = jnp.dot(hn, win_ref[...], preferred_element_type=F32)
    abuf[HALO:rows, :] = u[:, :D_CONV] * jax.nn.sigmoid(u[:, D_CONV:2 * D_CONV])
    bbuf[HALO:rows, :] = u[:, 2 * D_CONV:]

    a_all = abuf[...]
    for r in range(1, SUBLANES):
        ashift[r - 1] = pltpu.roll(a_all, rows - r, axis=0)
    for k in range(CONV_K):
        cwb[k] = jnp.broadcast_to(cw_ref[k:k + 1, :], (SUBLANES, D_CONV))
    for c in range(ts // CONV_ROWS):
        r0 = c * CONV_ROWS
        acc = jnp.broadcast_to(cb_ref[...], (CONV_ROWS, D_CONV))
        for k in range(CONV_K):
            q8, r = divmod(HALO - (CONV_K - 1) + k, SUBLANES)
            src = abuf if r == 0 else ashift.at[r - 1]
            start = r0 + q8 * SUBLANES
            tap = jnp.concatenate([cwb[k]] * (CONV_ROWS // SUBLANES), axis=0)
            acc = acc + tap * src[start:start + CONV_ROWS, :]
        mu = jnp.mean(acc, axis=-1, keepdims=True)
        d = acc - mu
        var = jnp.mean(d * d, axis=-1, keepdims=True)
        y = d * lax.rsqrt(var + LN_EPS) * lg_ref[...] + lb_ref[...]
        cat[r0:r0 + CONV_ROWS, 0:D_CONV] = _silu(y).astype(BF16)

    p1[8:rows, :] = bbuf[8:rows, :] + bbuf[7:rows - 1, :]
    p2[16:rows, 0:384] = p1[16:rows, 128:512] + p1[14:rows - 2, 128:512]
    p3[24:rows, 0:256] = p2[24:rows, 128:384] + p2[20:rows - 4, 128:384]
    s16 = p3[HALO:rows, 128:256] + p3[HALO - 8:rows - 8, 128:256]
    sums = (p1[HALO:rows, 0:128], p2[HALO:rows, 0:128], p3[HALO:rows, 0:128], s16)
    pos1 = (s * ts + 1 + lax.broadcasted_iota(jnp.int32, (ts, 1), 0)).astype(F32)
    for g, w in enumerate(POOL_WINDOWS):
        cnt = jnp.minimum(pos1, float(w))
        pg = sums[g] / cnt - bbuf[HALO:rows, g * POOL_GROUP:(g + 1) * POOL_GROUP]
        pm = jnp.dot(pg.astype(BF16), pw_ref[g], preferred_element_type=F32)
        pm = pm * ps_ref[:, g * POOL_GROUP:(g + 1) * POOL_GROUP]
        cat[:, D_CONV + g * POOL_GROUP:D_CONV + (g + 1) * POOL_GROUP] = pm.astype(BF16)

    o_ref[0] = x + jnp.dot(cat[...], wout_ref[...], preferred_element_type=F32)

    abuf[0:HALO, :] = abuf[ts:rows, :]
    bbuf[0:HALO, :] = bbuf[ts:rows, :]


def _even_mixer(h, g, w_in, conv_w, conv_b, ln_g, ln_b, pool_w, pool_scale, w_out):
    b, s, d = h.shape
    ts = SEQ_TILE
    rows = HALO + ts
    const = lambda shape: pl.BlockSpec(shape, lambda i, j: (0,) * len(shape))
    return pl.pallas_call(
        _even_mixer_kernel,
        out_shape=jax.ShapeDtypeStruct(h.shape, F32),
        grid=(b, s // ts),
        in_specs=[
            pl.BlockSpec((1, ts, d), lambda i, j: (i, j, 0)),
            const((1, d)),
            const((d, D_IN_EVEN)),
            const((CONV_K, D_CONV)),
            const((1, D_CONV)),
            const((1, D_CONV)),
            const((1, D_CONV)),
            const((len(POOL_WINDOWS), POOL_GROUP, POOL_GROUP)),
            const((1, D_POOL)),
            const((d, d)),
        ],
        out_specs=pl.BlockSpec((1, ts, d), lambda i, j: (i, j, 0)),
        scratch_shapes=[
            pltpu.VMEM((rows, D_CONV), F32),
            pltpu.VMEM((SUBLANES - 1, rows, D_CONV), F32),
            pltpu.VMEM((CONV_K, SUBLANES, D_CONV), F32),
            pltpu.VMEM((rows, D_POOL), F32),
            pltpu.VMEM((rows, D_POOL), F32),
            pltpu.VMEM((rows, 384), F32),
            pltpu.VMEM((rows, 256), F32),
            pltpu.VMEM((ts, d), BF16),
        ],
        compiler_params=pltpu.CompilerParams(
            dimension_semantics=("arbitrary", "arbitrary"),
            vmem_limit_bytes=VMEM_LIMIT),
        name="even_mixer",
    )(h, g.reshape(1, d), w_in.astype(BF16), conv_w, conv_b.reshape(1, -1),
      ln_g.reshape(1, -1), ln_b.reshape(1, -1), pool_w.astype(BF16),
      pool_scale.reshape(1, -1), w_out.astype(BF16))


def _attn_kernel(sink_ref, x_ref, g_ref, wqkv_ref, bqkv_ref, wot_ref, bo_ref, o_ref,
                 kbuf, vtbuf, otbuf):
    s = pl.program_id(1)
    ts = SEQ_TILE
    blk = ATTN_BLOCK
    group = N_Q_HEADS // N_KV_HEADS

    @pl.when(s == 0)
    def _():
        kbuf[0:blk, :] = jnp.zeros((blk, N_KV_HEADS * LANES), BF16)
        vtbuf[:, 0:blk] = jnp.zeros((D_KV, blk), BF16)

    x = x_ref[0]
    hn = _rms(x, g_ref[...]).astype(BF16)
    u = jnp.dot(hn, wqkv_ref[...], preferred_element_type=F32) + bqkv_ref[...]
    q = (u[:, :D_Q] * (HEAD_DIM ** -0.5)).astype(BF16)

    lo_t = lax.broadcasted_iota(jnp.int32, (ts, LANES), 1) < HEAD_DIM
    for p in range(N_KV_HEADS // 2):
        t = u[:, D_Q + p * LANES:D_Q + (p + 1) * LANES]
        r = pltpu.roll(t, HEAD_DIM, axis=1)
        kbuf[blk:blk + ts, (2 * p) * LANES:(2 * p + 1) * LANES] = jnp.where(lo_t, t, r).astype(BF16)
        kbuf[blk:blk + ts, (2 * p + 1) * LANES:(2 * p + 2) * LANES] = jnp.where(lo_t, r, t).astype(BF16)
    vtbuf[:, blk:blk + ts] = jnp.transpose(u[:, D_Q + D_KV:]).astype(BF16)

    lo_b = lax.broadcasted_iota(jnp.int32, (blk, LANES), 1) < HEAD_DIM
    zero_b = jnp.zeros((blk, LANES), BF16)
    key = lax.broadcasted_iota(jnp.int32, (2 * blk, blk), 0)
    qry = lax.broadcasted_iota(jnp.int32, (2 * blk, blk), 1)
    band = (key > qry) & (key <= qry + blk)
    band0 = band & ((key >= blk) | (s > 0))
    neg = jnp.concatenate([jnp.where(band, 0.0, -jnp.inf)] * group, axis=1)
    neg0 = jnp.concatenate([jnp.where(band0, 0.0, -jnp.inf)] * group, axis=1)

    def scores(j, g):
        kk = kbuf[j * blk:(j + 2) * blk, g * LANES:(g + 1) * LANES]
        qs = []
        sk = []
        for i in range(group):
            h = g * group + i
            qt = q[j * blk:(j + 1) * blk, (h // 2) * LANES:(h // 2 + 1) * LANES]
            qs.append(jnp.where(lo_b, qt, zero_b) if h % 2 == 0 else jnp.where(lo_b, zero_b, qt))
            sk.append(jnp.full((1, blk), sink_ref[h], F32))
        q4 = jnp.concatenate(qs, axis=0)
        st = lax.dot_general(kk, q4, (((1,), (1,)), ((), ())), preferred_element_type=F32)
        return st + (neg0 if j == 0 else neg), jnp.concatenate(sk, axis=1)

    pairs = [(j, g) for j in range(ts // blk) for g in range(N_KV_HEADS)]
    ahead = [scores(*p) for p in pairs[:SCORE_LOOKAHEAD]]
    for idx, (j, g) in enumerate(pairs):
        st, sink = ahead.pop(0)
        if idx + SCORE_LOOKAHEAD < len(pairs):
            ahead.append(scores(*pairs[idx + SCORE_LOOKAHEAD]))
        vt = vtbuf[g * HEAD_DIM:(g + 1) * HEAD_DIM, j * blk:(j + 2) * blk]
        m = jnp.maximum(jnp.max(st, axis=0, keepdims=True), sink)
        e = jnp.exp(st - m)
        denom = jnp.sum(e, axis=0, keepdims=True) + jnp.exp(sink - m)
        ot = jnp.dot(vt, e.astype(BF16), preferred_element_type=F32) * (1.0 / denom)
        for i in range(group):
            h = g * group + i
            otbuf[h * HEAD_DIM:(h + 1) * HEAD_DIM, j * blk:(j + 1) * blk] = (
                ot[:, i * blk:(i + 1) * blk].astype(BF16))

    proj_t = jnp.dot(wot_ref[...], otbuf[...], preferred_element_type=F32)
    o_ref[0] = x + jnp.transpose(proj_t) + bo_ref[...]

    kbuf[0:blk, :] = kbuf[ts:ts + blk, :]
    vtbuf[:, 0:blk] = vtbuf[:, ts:ts + blk]


def _attn_mixer(h, g, w_qkv, b_qkv, sinks, w_o, b_o):
    b, s, d = h.shape
    ts = SEQ_TILE
    const = lambda shape: pl.BlockSpec(shape, lambda i, j, sk: (0,) * len(shape))
    grid_spec = pltpu.PrefetchScalarGridSpec(
        num_scalar_prefetch=1,
        grid=(b, s // ts),
        in_specs=[
            pl.BlockSpec((1, ts, d), lambda i, j, sk: (i, j, 0)),
            const((1, d)),
            const((d, D_QKV)),
            const((1, D_QKV)),
            const((d, D_Q)),
            const((1, d)),
        ],
        out_specs=pl.BlockSpec((1, ts, d), lambda i, j, sk: (i, j, 0)),
        scratch_shapes=[
            pltpu.VMEM((ATTN_BLOCK + ts, N_KV_HEADS * LANES), BF16),
            pltpu.VMEM((D_KV, ATTN_BLOCK + ts), BF16),
            pltpu.VMEM((D_Q, ts), BF16),
        ],
    )
    return pl.pallas_call(
        _attn_kernel,
        out_shape=jax.ShapeDtypeStruct(h.shape, F32),
        grid_spec=grid_spec,
        compiler_params=pltpu.CompilerParams(
            dimension_semantics=("arbitrary", "arbitrary"),
            vmem_limit_bytes=VMEM_LIMIT),
        name="attn_mixer",
    )(sinks, h, g.reshape(1, d), w_qkv.astype(BF16), b_qkv.reshape(1, -1),
      jnp.transpose(w_o).astype(BF16), b_o.reshape(1, d))


def _ffn_kernel(x_ref, g_ref, w1_ref, w3_ref, w2_ref, o_ref, hid, *, tf):
    x = x_ref[...]
    hn = _rms(x, g_ref[...]).astype(BF16)

    def up(f0):
        return (jnp.dot(hn, w1_ref[:, f0:f0 + tf], preferred_element_type=F32),
                jnp.dot(hn, w3_ref[:, f0:f0 + tf], preferred_element_type=F32))

    starts = list(range(0, w1_ref.shape[1], tf))
    ahead = [up(f0) for f0 in starts[:FFN_LOOKAHEAD]]
    for idx, f0 in enumerate(starts):
        h1, h3 = ahead.pop(0)
        if idx + FFN_LOOKAHEAD < len(starts):
            ahead.append(up(starts[idx + FFN_LOOKAHEAD]))
        hid[:, f0:f0 + tf] = (_silu(h1) * h3).astype(BF16)
    o_ref[...] = x + jnp.dot(hid[...], w2_ref[...], preferred_element_type=F32)


def _ffn(h2d, g, w1, w3, w2, *, tm, tf):
    t, d = h2d.shape
    ff = w1.shape[1]
    resident = lambda shape: pl.BlockSpec(shape, lambda i: (0, 0), pipeline_mode=pl.Buffered(1))
    return pl.pallas_call(
        functools.partial(_ffn_kernel, tf=tf),
        out_shape=jax.ShapeDtypeStruct((t, d), F32),
        grid=(t // tm,),
        in_specs=[
            pl.BlockSpec((tm, d), lambda i: (i, 0)),
            pl.BlockSpec((1, d), lambda i: (0, 0)),
            resident((d, ff)),
            resident((d, ff)),
            resident((ff, d)),
        ],
        out_specs=pl.BlockSpec((tm, d), lambda i: (i, 0)),
        scratch_shapes=[pltpu.VMEM((tm, ff), BF16)],
        compiler_params=pltpu.CompilerParams(
            dimension_semantics=("arbitrary",), vmem_limit_bytes=VMEM_LIMIT),
        name="dense_ffn",
    )(h2d, g.reshape(1, d), w1.astype(BF16), w3.astype(BF16), w2.astype(BF16))


ROUTE_TILE = 512
EXPERT_TILE = 512
EXPERT_UP_CHUNK = 256
SORT_ROWS = 2 * ROUTE_TILE + N_EXPERTS * SUBLANES
SEG_CHUNKS = tuple(SUBLANES << j for j in range((ROUTE_TILE // SUBLANES).bit_length() - 1, -1, -1))


def _router_kernel(x_ref, g_ref, rwt_ref, col_ref, rowp_ref, seg_ref, cnt_ref, carry, upper):
    i = pl.program_id(0)
    tb = ROUTE_TILE
    ne = N_EXPERTS

    @pl.when(i == 0)
    def _():
        carry[...] = jnp.zeros(carry.shape, F32)
        rr = lax.broadcasted_iota(jnp.int32, (tb, tb), 0)
        cc = lax.broadcasted_iota(jnp.int32, (tb, tb), 1)
        upper[...] = jnp.where(rr < cc, 1.0, 0.0).astype(BF16)

    def nt(a, bmat):
        return lax.dot_general(a, bmat, (((1,), (1,)), ((), ())), preferred_element_type=F32)

    hn = _rms(x_ref[...], g_ref[...])
    rwt = rwt_ref[...]
    hn_hi = hn.astype(BF16)
    hn_lo = (hn - hn_hi.astype(F32)).astype(BF16)
    rw_hi = rwt.astype(BF16)
    rw_lo = (rwt - rw_hi.astype(F32)).astype(BF16)
    logits_t = nt(rw_hi, hn_hi) + (nt(rw_hi, hn_lo) + nt(rw_lo, hn_hi))
    lg = logits_t[0:ne, :]
    sub = lax.broadcasted_iota(jnp.int32, (ne, tb), 0)
    m1 = jnp.max(lg, axis=0, keepdims=True)
    i1 = jnp.min(jnp.where(lg == m1, sub, ne), axis=0, keepdims=True)
    lg2 = jnp.where(sub == i1, -jnp.inf, lg)
    m2 = jnp.max(lg2, axis=0, keepdims=True)
    i2 = jnp.min(jnp.where(lg2 == m2, sub, ne), axis=0, keepdims=True)
    e2 = jnp.exp(m2 - m1)
    den = 1.0 + e2

    mem = jnp.where((sub == i1) | (sub == i2), 1.0, 0.0)
    mem_rows = jnp.concatenate([mem, jnp.zeros((LANES - ne, tb), F32)], axis=0).astype(BF16)
    before = jnp.dot(mem_rows[0:2 * ne, :], upper[...], preferred_element_type=F32)[0:ne, :]

    n = nt(jnp.ones((2 * SUBLANES, tb), BF16), mem_rows)[0:SUBLANES, :]
    n8 = jnp.floor((n + (SUBLANES - 1)) / SUBLANES) * SUBLANES
    incl = n8
    for sh in (1, 2, 4):
        incl = incl + pltpu.roll(incl, sh, axis=1)
    seg8 = incl - n8
    seg8_t = jnp.transpose(jnp.broadcast_to(seg8[0:1, :], (LANES, LANES)))[0:ne, :]
    pos = before + jnp.concatenate([seg8_t] * (tb // LANES), axis=1)
    rp1 = jnp.sum(jnp.where(sub == i1, pos, 0.0), axis=0, keepdims=True)
    rp2 = jnp.sum(jnp.where(sub == i2, pos, 0.0), axis=0, keepdims=True)

    rowp_ref[0] = jnp.where(sub == 0, rp1, jnp.where(sub == 1, rp2, 0.0)).astype(jnp.int32)
    table = jnp.where(sub == 0, 1.0 / den,
                      jnp.where(sub == 1, e2 / den,
                                jnp.where(sub == 2, rp1, jnp.where(sub == 3, rp2, 0.0))))
    col_ref[...] = jnp.transpose(
        jnp.concatenate([table, jnp.zeros((LANES - ne, tb), F32)], axis=0))
    fld = lax.broadcasted_iota(jnp.int32, (SUBLANES, LANES), 0)
    seg_ref[0] = jnp.where(fld == 0, n8, jnp.where(fld == 1, seg8,
                                                   jnp.where(fld == 2, carry[...], 0.0))).astype(jnp.int32)
    carry[...] = carry[...] + n8
    cnt_ref[...] = carry[...]


def _router(h2d, g, router_w):
    t, d = h2d.shape
    tb = ROUTE_TILE
    rwt = jnp.zeros((LANES, d), F32).at[:N_EXPERTS, :].set(jnp.transpose(router_w))
    return pl.pallas_call(
        _router_kernel,
        out_shape=(
            jax.ShapeDtypeStruct((t, LANES), F32),
            jax.ShapeDtypeStruct((t // tb, SUBLANES, tb), jnp.int32),
            jax.ShapeDtypeStruct((t // tb, SUBLANES, LANES), jnp.int32),
            jax.ShapeDtypeStruct((SUBLANES, LANES), F32),
        ),
        grid=(t // tb,),
        in_specs=[
            pl.BlockSpec((tb, d), lambda i: (i, 0)),
            pl.BlockSpec((1, d), lambda i: (0, 0)),
            pl.BlockSpec((LANES, d), lambda i: (0, 0)),
        ],
        out_specs=(
            pl.BlockSpec((tb, LANES), lambda i: (i, 0)),
            pl.BlockSpec((1, SUBLANES, tb), lambda i: (i, 0, 0)),
            pl.BlockSpec((1, SUBLANES, LANES), lambda i: (i, 0, 0)),
            pl.BlockSpec((SUBLANES, LANES), lambda i: (0, 0)),
        ),
        scratch_shapes=[pltpu.VMEM((SUBLANES, LANES), F32), pltpu.VMEM((tb, tb), BF16)],
        compiler_params=pltpu.CompilerParams(
            dimension_semantics=("arbitrary",), vmem_limit_bytes=VMEM_LIMIT),
        name="moe_router",
    )(h2d, g.reshape(1, d), rwt)


def _segment_copies(seg_ref, off_ref, make_copy, do_start):
    for e in range(N_EXPERTS):
        n8 = seg_ref[0, 0, e]
        tile_row = seg_ref[0, 1, e]
        group_row = off_ref[e] + seg_ref[0, 2, e]
        for c in SEG_CHUNKS:
            done = n8 & ~(2 * c - 1)

            @pl.when((n8 & c) != 0)
            def _():
                cp = make_copy(pl.multiple_of(tile_row + done, SUBLANES),
                               pl.multiple_of(group_row + done, SUBLANES), c)
                cp.start() if do_start else cp.wait()


def _dispatch_kernel(off_ref, pad_lo_ref, pad_hi_ref, na_ref, h_ref, g_ref, rowp_ref, seg_ref,
                     seg_prev_ref, xs_ref, sorted_s, zblk, sem, zsem):
    tb = ROUTE_TILE
    tm = EXPERT_TILE
    i = pl.program_id(0)
    slot = i % 2

    @pl.when(i == 0)
    def _():
        zblk[...] = jnp.zeros(zblk.shape, F32)
        n_blocks = xs_ref.shape[0] // tm

        def fill(do_start):
            def pad_group(r, c):
                cp = pltpu.make_async_copy(
                    zblk.at[pl.ds(0, SUBLANES)],
                    xs_ref.at[pl.ds(pl.multiple_of(r * SUBLANES, SUBLANES), SUBLANES)], zsem)
                cp.start() if do_start else cp.wait()
                return c

            def tail_blk(b, c):
                cp = pltpu.make_async_copy(
                    zblk, xs_ref.at[pl.ds(pl.multiple_of(b * tm, tm), tm)], zsem)
                cp.start() if do_start else cp.wait()
                return c

            for e in range(N_EXPERTS):
                lax.fori_loop(pad_lo_ref[e], pad_hi_ref[e], pad_group, 0)
            lax.fori_loop(na_ref[0], n_blocks, tail_blk, 0)

        fill(True)
        fill(False)

    hn = _rms(h_ref[...], g_ref[...]).astype(BF16)
    row = lax.broadcasted_iota(jnp.int32, (SORT_ROWS, tb), 0)
    hit = (row == rowp_ref[0, 0:1, :]) | (row == rowp_ref[0, 1:2, :])
    sorted_s[slot] = jnp.dot(jnp.where(hit, 1.0, 0.0).astype(BF16), hn, preferred_element_type=F32)

    def copier(buf):
        def make_copy(tile_row, group_row, rows):
            return pltpu.make_async_copy(sorted_s.at[buf, pl.ds(tile_row, rows)],
                                         xs_ref.at[pl.ds(group_row, rows)], sem.at[buf])
        return make_copy

    _segment_copies(seg_ref, off_ref, copier(slot), True)

    @pl.when(i > 0)
    def _():
        _segment_copies(seg_prev_ref, off_ref, copier(1 - slot), False)

    @pl.when(i == pl.num_programs(0) - 1)
    def _():
        _segment_copies(seg_ref, off_ref, copier(slot), False)


def _dispatch(h2d, g, rowp, seg, off, pad_lo, pad_hi, n_active, n_rows):
    t, d = h2d.shape
    tb = ROUTE_TILE
    seg_spec = lambda index: pl.BlockSpec((1, SUBLANES, LANES), index, memory_space=pltpu.SMEM)
    grid_spec = pltpu.PrefetchScalarGridSpec(
        num_scalar_prefetch=4,
        grid=(t // tb,),
        in_specs=[
            pl.BlockSpec((tb, d), lambda i, *_: (i, 0)),
            pl.BlockSpec((1, d), lambda i, *_: (0, 0)),
            pl.BlockSpec((1, SUBLANES, tb), lambda i, *_: (i, 0, 0)),
            seg_spec(lambda i, *_: (i, 0, 0)),
            seg_spec(lambda i, *_: (jnp.maximum(i - 1, 0), 0, 0)),
        ],
        out_specs=pl.BlockSpec(memory_space=pl.ANY),
        scratch_shapes=[pltpu.VMEM((2, SORT_ROWS, d), F32),
                        pltpu.VMEM((EXPERT_TILE, d), F32),
                        pltpu.SemaphoreType.DMA((2,)),
                        pltpu.SemaphoreType.DMA(())],
    )
    return pl.pallas_call(
        _dispatch_kernel,
        out_shape=jax.ShapeDtypeStruct((n_rows, d), F32),
        grid_spec=grid_spec,
        compiler_params=pltpu.CompilerParams(
            dimension_semantics=("arbitrary",), has_side_effects=True,
            vmem_limit_bytes=VMEM_LIMIT),
        name="moe_dispatch",
    )(off, pad_lo, pad_hi, n_active, h2d, g.reshape(1, d), rowp, seg, seg)


def _expert_kernel(be_ref, na_ref, vr_ref, x_ref, w1_ref, w3_ref, w2_ref, y_ref, hid):
    i = pl.program_id(0)
    f = pl.program_id(1)
    tm = y_ref.shape[0]

    def swiglu_rows(rows):
        x = x_ref[0:rows, :].astype(BF16)

        def up(f0):
            return (jnp.dot(x, w1_ref[0, :, f0:f0 + EXPERT_UP_CHUNK], preferred_element_type=F32),
                    jnp.dot(x, w3_ref[0, :, f0:f0 + EXPERT_UP_CHUNK], preferred_element_type=F32))

        starts = list(range(0, w1_ref.shape[2], EXPERT_UP_CHUNK))
        ahead = [up(f0) for f0 in starts[:FFN_LOOKAHEAD]]
        for idx, f0 in enumerate(starts):
            h1, h3 = ahead.pop(0)
            if idx + FFN_LOOKAHEAD < len(starts):
                ahead.append(up(starts[idx + FFN_LOOKAHEAD]))
            hid[0:rows, f0:f0 + EXPERT_UP_CHUNK] = (_silu(h1) * h3).astype(BF16)
        part = jnp.dot(hid[0:rows, :], w2_ref[0], preferred_element_type=F32)

        @pl.when(f == 0)
        def _():
            y_ref[0:rows, :] = part
            if rows < tm:
                y_ref[rows:tm, :] = jnp.zeros((tm - rows, y_ref.shape[1]), F32)

        @pl.when(f > 0)
        def _():
            y_ref[0:rows, :] += part

    active = i < na_ref[0]
    half_full = vr_ref[i] <= tm // 2

    @pl.when(active & jnp.logical_not(half_full))
    def _():
        swiglu_rows(tm)

    @pl.when(active & half_full)
    def _():
        swiglu_rows(tm // 2)

    @pl.when(jnp.logical_not(active) & (f == 0))
    def _():
        y_ref[...] = jnp.zeros(y_ref.shape, F32)


def _experts(xs, blk_expert, n_active, valid_rows, w1, w3, w2, *, tf):
    r, d = xs.shape
    tm = EXPERT_TILE
    ff = w1.shape[2]
    nf = ff // tf

    def row_blk(i, f, be, na, vr):
        return (jnp.maximum(jnp.minimum(i, na[0] - 1), 0), 0)

    def f_blk(i, f, na):
        return jnp.where(i < na[0], f, nf - 1)

    grid_spec = pltpu.PrefetchScalarGridSpec(
        num_scalar_prefetch=3,
        grid=(r // tm, nf),
        in_specs=[
            pl.BlockSpec((tm, d), row_blk),
            pl.BlockSpec((1, d, tf), lambda i, f, be, na, vr: (be[i], 0, f_blk(i, f, na))),
            pl.BlockSpec((1, d, tf), lambda i, f, be, na, vr: (be[i], 0, f_blk(i, f, na))),
            pl.BlockSpec((1, tf, d), lambda i, f, be, na, vr: (be[i], f_blk(i, f, na), 0)),
        ],
        out_specs=pl.BlockSpec((tm, d), lambda i, f, be, na, vr: (i, 0)),
        scratch_shapes=[pltpu.VMEM((tm, tf), BF16)],
    )
    return pl.pallas_call(
        _expert_kernel,
        out_shape=jax.ShapeDtypeStruct((r, d), F32),
        grid_spec=grid_spec,
        compiler_params=pltpu.CompilerParams(
            dimension_semantics=("arbitrary", "arbitrary"),
            vmem_limit_bytes=VMEM_LIMIT),
        name="moe_experts",
    )(blk_expert, n_active, valid_rows, xs, w1, w3, w2)


def _combine_kernel(off_ref, h_ref, col_ref, seg_ref, seg_next_ref, y_ref, gf_ref, o_ref,
                    ysort, sem, *, apply_final_norm):
    tb = ROUTE_TILE
    i = pl.program_id(0)
    slot = i % 2

    def copier(buf):
        def make_copy(tile_row, group_row, rows):
            return pltpu.make_async_copy(y_ref.at[pl.ds(group_row, rows)],
                                         ysort.at[buf, pl.ds(tile_row, rows)], sem.at[buf])
        return make_copy

    @pl.when(i == 0)
    def _():
        ysort[...] = jnp.zeros(ysort.shape, F32)
        _segment_copies(seg_ref, off_ref, copier(0), True)

    @pl.when(i + 1 < pl.num_programs(0))
    def _():
        _segment_copies(seg_next_ref, off_ref, copier(1 - slot), True)

    _segment_copies(seg_ref, off_ref, copier(slot), False)

    col = col_ref[...]
    row = lax.broadcasted_iota(jnp.int32, (tb, SORT_ROWS), 1)
    yb = ysort[slot].astype(BF16)
    acc = None
    for k in range(2):
        pick = jnp.where(row == col[:, 2 + k:3 + k].astype(jnp.int32), 1.0, 0.0).astype(BF16)
        term = col[:, k:k + 1] * jnp.dot(pick, yb, preferred_element_type=F32)
        acc = term if acc is None else acc + term
    out = h_ref[...] + acc
    o_ref[...] = _rms(out, gf_ref[...]) if apply_final_norm else out


def _combine(h2d, col, seg, y, off, final_g, *, apply_final_norm):
    t, d = h2d.shape
    tb = ROUTE_TILE
    n_tiles = t // tb
    seg_spec = lambda index: pl.BlockSpec((1, SUBLANES, LANES), index, memory_space=pltpu.SMEM)
    grid_spec = pltpu.PrefetchScalarGridSpec(
        num_scalar_prefetch=1,
        grid=(n_tiles,),
        in_specs=[
            pl.BlockSpec((tb, d), lambda i, off: (i, 0)),
            pl.BlockSpec((tb, LANES), lambda i, off: (i, 0)),
            seg_spec(lambda i, off: (i, 0, 0)),
            seg_spec(lambda i, off: (jnp.minimum(i + 1, n_tiles - 1), 0, 0)),
            pl.BlockSpec(memory_space=pl.ANY),
            pl.BlockSpec((1, d), lambda i, off: (0, 0)),
        ],
        out_specs=pl.BlockSpec((tb, d), lambda i, off: (i, 0)),
        scratch_shapes=[pltpu.VMEM((2, SORT_ROWS, d), F32), pltpu.SemaphoreType.DMA((2,))],
    )
    return pl.pallas_call(
        functools.partial(_combine_kernel, apply_final_norm=apply_final_norm),
        out_shape=jax.ShapeDtypeStruct((t, d), F32),
        grid_spec=grid_spec,
        compiler_params=pltpu.CompilerParams(
            dimension_semantics=("arbitrary",), vmem_limit_bytes=VMEM_LIMIT),
        name="moe_combine",
    )(off, h2d, col, seg, seg, y, final_g.reshape(1, d))


def _moe(h2d, g, router_w, w1, w3, w2, layer, final_g, *, apply_final_norm):
    t, d = h2d.shape
    tm = EXPERT_TILE
    n_tiles = t // ROUTE_TILE
    n_rows = 2 * t + n_tiles * N_EXPERTS * SUBLANES + N_EXPERTS * tm
    n_rows = -(-n_rows // tm) * tm
    col, rowp, seg, cnt = _router(h2d, g, router_w)
    counts = cnt[0, :N_EXPERTS].astype(jnp.int32)
    blocks = (counts + tm - 1) // tm
    ends = jnp.cumsum(blocks)
    off = (ends - blocks) * tm
    n_active = ends[-1:]
    blk = jnp.arange(n_rows // tm, dtype=jnp.int32)
    blk_expert = jnp.sum(jnp.minimum(blk, n_active - 1)[:, None] >= ends[None, :], axis=1)
    blk_expert = blk_expert.astype(jnp.int32)
    valid_rows = jnp.clip(counts[blk_expert] - (blk - (ends - blocks)[blk_expert]) * tm, 0, tm)
    blk_expert = blk_expert + layer * N_EXPERTS
    xs = _dispatch(h2d, g, rowp, seg, off, (off + counts) // SUBLANES,
                   (off + blocks * tm) // SUBLANES, n_active, n_rows)
    y = _experts(xs, blk_expert, n_active, valid_rows, w1, w3, w2, tf=1792)
    return _combine(h2d, col, seg, y, off, final_g, apply_final_norm=apply_final_norm)


def kernel(x, e_norm1, e_w_in, e_conv_w, e_conv_b, e_ln_g, e_ln_b, e_pool_w, e_pool_scale,
           e_w_out, e_norm2, e_ff_w1, e_ff_w3, e_ff_w2, o_norm1, o_w_qkv, o_b_qkv, o_sinks,
           o_w_o, o_b_o, o_norm2, o_router, o_exp_w1, o_exp_w3, o_exp_w2, final_norm):
    b, s, d = x.shape
    assert DEPTH % 2 == 0, "the final norm is fused into the last (odd, MoE) layer"
    expert_w = [w.astype(BF16).reshape((-1,) + w.shape[2:]) for w in (o_exp_w1, o_exp_w3, o_exp_w2)]
    h = x
    for layer in range(DEPTH):
        i = layer // 2
        if layer % 2 == 0:
            h = _even_mixer(h, e_norm1[i], e_w_in[i], e_conv_w[i], e_conv_b[i], e_ln_g[i],
                            e_ln_b[i], e_pool_w[i], e_pool_scale[i], e_w_out[i])
            h = _ffn(h.reshape(b * s, d), e_norm2[i], e_ff_w1[i], e_ff_w3[i], e_ff_w2[i],
                     tm=1024, tf=256).reshape(b, s, d)
        else:
            h = _attn_mixer(h, o_norm1[i], o_w_qkv[i], o_b_qkv[i], o_sinks[i], o_w_o[i],
                            o_b_o[i])
            h = _moe(h.reshape(b * s, d), o_norm2[i], o_router[i], *expert_w, i, final_norm,
                     apply_final_norm=(layer == DEPTH - 1)).reshape(b, s, d)
    return h
```

```python
import functools

import jax
import jax.numpy as jnp
from jax import lax
from jax.experimental import pallas as pl
from jax.experimental.pallas import tpu as pltpu

F32 = jnp.float32
BF16 = jnp.bfloat16

D_MODEL = 1024
DEPTH = 4
RMS_EPS = 1e-5
LN_EPS = 1e-5

D_CONV = 512
D_POOL = 512
CONV_K = 31
POOL_WINDOWS = (2, 4, 8, 16)
POOL_GROUP = 128
D_IN_EVEN = 2 * D_CONV + D_POOL

HEAD_DIM = 64
N_Q_HEADS = 16
N_KV_HEADS = 4
ATTN_BLOCK = 128
D_Q = N_Q_HEADS * HEAD_DIM
D_KV = N_KV_HEADS * HEAD_DIM
D_QKV = D_Q + 2 * D_KV

N_EXPERTS = 8

LANES = 128
SUBLANES = 8
SEQ_TILE = 512
HALO = 32
CONV_ROWS = 32
SCORE_LOOKAHEAD = 2
FFN_LOOKAHEAD = 2
VMEM_LIMIT = 56 * 1024 * 1024


def _rms(x, g):
    ms = jnp.mean(x * x, axis=-1, keepdims=True)
    return x * lax.rsqrt(ms + RMS_EPS) * g


def _silu(x):
    return x * jax.nn.sigmoid(x)


def _even_mixer_kernel(x_ref, g_ref, win_ref, cw_ref, cb_ref, lg_ref, lb_ref,
                       pw_ref, ps_ref, wout_ref, o_ref,
                       abuf, ashift, cwb, bbuf, p1, p2, p3, cat):
    s = pl.program_id(1)
    ts = SEQ_TILE
    rows = HALO + ts

    @pl.when(s == 0)
    def _():
        abuf[0:HALO, :] = jnp.zeros((HALO, D_CONV), F32)
        bbuf[0:HALO, :] = jnp.zeros((HALO, D_POOL), F32)

    x = x_ref[0]
    hn = _rms(x, g_ref[...]).astype(BF16)
    u = jnp.dot(hn, win_ref[...], preferred_element_type=F32)
    abuf[HALO:rows, :] = u[:, :D_CONV] * jax.nn.sigmoid(u[:, D_CONV:2 * D_CONV])
    bbuf[HALO:rows, :] = u[:, 2 * D_CONV:]

    a_all = abuf[...]
    for r in range(1, SUBLANES):
        ashift[r - 1] = pltpu.roll(a_all, rows - r, axis=0)
    for k in range(CONV_K):
        cwb[k] = jnp.broadcast_to(cw_ref[k:k + 1, :], (SUBLANES, D_CONV))
    for c in range(ts // CONV_ROWS):
        r0 = c * CONV_ROWS
        acc = jnp.broadcast_to(cb_ref[...], (CONV_ROWS, D_CONV))
        for k in range(CONV_K):
            q8, r = divmod(HALO - (CONV_K - 1) + k, SUBLANES)
            src = abuf if r == 0 else ashift.at[r - 1]
            start = r0 + q8 * SUBLANES
            tap = jnp.concatenate([cwb[k]] * (CONV_ROWS // SUBLANES), axis=0)
            acc = acc + tap * src[start:start + CONV_ROWS, :]
        mu = jnp.mean(acc, axis=-1, keepdims=True)
        d = acc - mu
        var = jnp.mean(d * d, axis=-1, keepdims=True)
        y = d * lax.rsqrt(var + LN_EPS) * lg_ref[...] + lb_ref[...]
        cat[r0:r0 + CONV_ROWS, 0:D_CONV] = _silu(y).astype(BF16)

    p1[8:rows, :] = bbuf[8:rows, :] + bbuf[7:rows - 1, :]
    p2[16:rows, 0:384] = p1[16:rows, 128:512] + p1[14:rows - 2, 128:512]
    p3[24:rows, 0:256] = p2[24:rows, 128:384] + p2[20:rows - 4, 128:384]
    s16 = p3[HALO:rows, 128:256] + p3[HALO - 8:rows - 8, 128:256]
    sums = (p1[HALO:rows, 0:128], p2[HALO:rows, 0:128], p3[HALO:rows, 0:128], s16)
    pos1 = (s * ts + 1 + lax.broadcasted_iota(jnp.int32, (ts, 1), 0)).astype(F32)
    for g, w in enumerate(POOL_WINDOWS):
        cnt = jnp.minimum(pos1, float(w))
        pg = sums[g] / cnt - bbuf[HALO:rows, g * POOL_GROUP:(g + 1) * POOL_GROUP]
        pm = jnp.dot(pg.astype(BF16), pw_ref[g], preferred_element_type=F32)
        pm = pm * ps_ref[:, g * POOL_GROUP:(g + 1) * POOL_GROUP]
        cat[:, D_CONV + g * POOL_GROUP:D_CONV + (g + 1) * POOL_GROUP] = pm.astype(BF16)

    o_ref[0] = x + jnp.dot(cat[...], wout_ref[...], preferred_element_type=F32)

    abuf[0:HALO, :] = abuf[ts:rows, :]
    bbuf[0:HALO, :] = bbuf[ts:rows, :]


def _even_mixer(h, g, w_in, conv_w, conv_b, ln_g, ln_b, pool_w, pool_scale, w_out):
    b, s, d = h.shape
    ts = SEQ_TILE
    rows = HALO + ts
    const = lambda shape: pl.BlockSpec(shape, lambda i, j: (0,) * len(shape))
    return pl.pallas_call(
        _even_mixer_kernel,
        out_shape=jax.ShapeDtypeStruct(h.shape, F32),
        grid=(b, s // ts),
        in_specs=[
            pl.BlockSpec((1, ts, d), lambda i, j: (i, j, 0)),
            const((1, d)),
            const((d, D_IN_EVEN)),
            const((CONV_K, D_CONV)),
            const((1, D_CONV)),
            const((1, D_CONV)),
            const((1, D_CONV)),
            const((len(POOL_WINDOWS), POOL_GROUP, POOL_GROUP)),
            const((1, D_POOL)),
            const((d, d)),
        ],
        out_specs=pl.BlockSpec((1, ts, d), lambda i, j: (i, j, 0)),
        scratch_shapes=[
            pltpu.VMEM((rows, D_CONV), F32),
            pltpu.VMEM((SUBLANES - 1, rows, D_CONV), F32),
            pltpu.VMEM((CONV_K, SUBLANES, D_CONV), F32),
            pltpu.VMEM((rows, D_POOL), F32),
            pltpu.VMEM((rows, D_POOL), F32),
            pltpu.VMEM((rows, 384), F32),
            pltpu.VMEM((rows, 256), F32),
            pltpu.VMEM((ts, d), BF16),
        ],
        compiler_params=pltpu.CompilerParams(
            dimension_semantics=("arbitrary", "arbitrary"),
            vmem_limit_bytes=VMEM_LIMIT),
        name="even_mixer",
    )(h, g.reshape(1, d), w_in.astype(BF16), conv_w, conv_b.reshape(1, -1),
      ln_g.reshape(1, -1), ln_b.reshape(1, -1), pool_w.astype(BF16),
      pool_scale.reshape(1, -1), w_out.astype(BF16))


def _attn_kernel(sink_ref, x_ref, g_ref, wqkv_ref, bqkv_ref, wot_ref, bo_ref, o_ref,
                 kbuf, vtbuf, otbuf):
    s = pl.program_id(1)
    ts = SEQ_TILE
    blk = ATTN_BLOCK
    group = N_Q_HEADS // N_KV_HEADS

    @pl.when(s == 0)
    def _():
        kbuf[0:blk, :] = jnp.zeros((blk, N_KV_HEADS * LANES), BF16)
        vtbuf[:, 0:blk] = jnp.zeros((D_KV, blk), BF16)

    x = x_ref[0]
    hn = _rms(x, g_ref[...]).astype(BF16)
    u = jnp.dot(hn, wqkv_ref[...], preferred_element_type=F32) + bqkv_ref[...]
    q = (u[:, :D_Q] * (HEAD_DIM ** -0.5)).astype(BF16)

    lo_t = lax.broadcasted_iota(jnp.int32, (ts, LANES), 1) < HEAD_DIM
    for p in range(N_KV_HEADS // 2):
        t = u[:, D_Q + p * LANES:D_Q + (p + 1) * LANES]
        r = pltpu.roll(t, HEAD_DIM, axis=1)
        kbuf[blk:blk + ts, (2 * p) * LANES:(2 * p + 1) * LANES] = jnp.where(lo_t, t, r).astype(BF16)
        kbuf[blk:blk + ts, (2 * p + 1) * LANES:(2 * p + 2) * LANES] = jnp.where(lo_t, r, t).astype(BF16)
    vtbuf[:, blk:blk + ts] = jnp.transpose(u[:, D_Q + D_KV:]).astype(BF16)

    lo_b = lax.broadcasted_iota(jnp.int32, (blk, LANES), 1) < HEAD_DIM
    zero_b = jnp.zeros((blk, LANES), BF16)
    key = lax.broadcasted_iota(jnp.int32, (2 * blk, blk), 0)
    qry = lax.broadcasted_iota(jnp.int32, (2 * blk, blk), 1)
    band = (key > qry) & (key <= qry + blk)
    band0 = band & ((key >= blk) | (s > 0))
    neg = jnp.concatenate([jnp.where(band, 0.0, -jnp.inf)] * group, axis=1)
    neg0 = jnp.concatenate([jnp.where(band0, 0.0, -jnp.inf)] * group, axis=1)

    def scores(j, g):
        kk = kbuf[j * blk:(j + 2) * blk, g * LANES:(g + 1) * LANES]
        qs = []
        sk = []
        for i in range(group):
            h = g * group + i
            qt = q[j * blk:(j + 1) * blk, (h // 2) * LANES:(h // 2 + 1) * LANES]
            qs.append(jnp.where(lo_b, qt, zero_b) if h % 2 == 0 else jnp.where(lo_b, zero_b, qt))
            sk.append(jnp.full((1, blk), sink_ref[h], F32))
        q4 = jnp.concatenate(qs, axis=0)
        st = lax.dot_general(kk, q4, (((1,), (1,)), ((), ())), preferred_element_type=F32)
        return st + (neg0 if j == 0 else neg), jnp.concatenate(sk, axis=1)

    pairs = [(j, g) for j in range(ts // blk) for g in range(N_KV_HEADS)]
    ahead = [scores(*p) for p in pairs[:SCORE_LOOKAHEAD]]
    for idx, (j, g) in enumerate(pairs):
        st, sink = ahead.pop(0)
        if idx + SCORE_LOOKAHEAD < len(pairs):
            ahead.append(scores(*pairs[idx + SCORE_LOOKAHEAD]))
        vt = vtbuf[g * HEAD_DIM:(g + 1) * HEAD_DIM, j * blk:(j + 2) * blk]
        m = jnp.maximum(jnp.max(st, axis=0, keepdims=True), sink)
        e = jnp.exp(st - m)
        denom = jnp.sum(e, axis=0, keepdims=True) + jnp.exp(sink - m)
        ot = jnp.dot(vt, e.astype(BF16), preferred_element_type=F32) * (1.0 / denom)
        for i in range(group):
            h = g * group + i
            otbuf[h * HEAD_DIM:(h + 1) * HEAD_DIM, j * blk:(j + 1) * blk] = (
                ot[:, i * blk:(i + 1) * blk].astype(BF16))

    proj_t = jnp.dot(wot_ref[...], otbuf[...], preferred_element_type=F32)
    o_ref[0] = x + jnp.transpose(proj_t) + bo_ref[...]

    kbuf[0:blk, :] = kbuf[ts:ts + blk, :]
    vtbuf[:, 0:blk] = vtbuf[:, ts:ts + blk]


def _attn_mixer(h, g, w_qkv, b_qkv, sinks, w_o, b_o):
    b, s, d = h.shape
    ts = SEQ_TILE
    const = lambda shape: pl.BlockSpec(shape, lambda i, j, sk: (0,) * len(shape))
    grid_spec = pltpu.PrefetchScalarGridSpec(
        num_scalar_prefetch=1,
        grid=(b, s // ts),
        in_specs=[
            pl.BlockSpec((1, ts, d), lambda i, j, sk: (i, j, 0)),
            const((1, d)),
            const((d, D_QKV)),
            const((1, D_QKV)),
            const((d, D_Q)),
            const((1, d)),
        ],
        out_specs=pl.BlockSpec((1, ts, d), lambda i, j, sk: (i, j, 0)),
        scratch_shapes=[
            pltpu.VMEM((ATTN_BLOCK + ts, N_KV_HEADS * LANES), BF16),
            pltpu.VMEM((D_KV, ATTN_BLOCK + ts), BF16),
            pltpu.VMEM((D_Q, ts), BF16),
        ],
    )
    return pl.pallas_call(
        _attn_kernel,
        out_shape=jax.ShapeDtypeStruct(h.shape, F32),
        grid_spec=grid_spec,
        compiler_params=pltpu.CompilerParams(
            dimension_semantics=("arbitrary", "arbitrary"),
            vmem_limit_bytes=VMEM_LIMIT),
        name="attn_mixer",
    )(sinks, h, g.reshape(1, d), w_qkv.astype(BF16), b_qkv.reshape(1, -1),
      jnp.transpose(w_o).astype(BF16), b_o.reshape(1, d))


CAST_JOBS = 8


def _ffn_kernel(x_ref, g_ref, w1_ref, w3_ref, w2_ref, s1_ref, s3_ref, s2_ref,
                o_ref, d1_ref, d3_ref, d2_ref,
                hid, cin1, cin3, cin2, cout1, cout3, cout2, sem_in, sem_out, *, tf):
    step = pl.program_id(0)
    srcs, dsts = (s1_ref, s3_ref, s2_ref), (d1_ref, d3_ref, d2_ref)
    cins, couts = (cin1, cin3, cin2), (cout1, cout3, cout2)

    def rows_at(a, j):
        r = cins[a].shape[1]
        return pl.ds(pl.multiple_of((step * CAST_JOBS + j) * r, r), r)

    def copy_in(a, j):
        return pltpu.make_async_copy(srcs[a].at[rows_at(a, j)], cins[a].at[j % 2],
                                     sem_in.at[2 * a + j % 2])

    def copy_out(a, j):
        return pltpu.make_async_copy(couts[a].at[j % 2], dsts[a].at[rows_at(a, j)],
                                     sem_out.at[2 * a + j % 2])

    def cast_job(a, j):
        copy_in(a, j).wait()
        if j >= 2:
            copy_out(a, j - 2).wait()
        couts[a][j % 2] = cins[a][j % 2].astype(BF16)
        copy_out(a, j).start()
        if j + 2 < CAST_JOBS:
            copy_in(a, j + 2).start()

    jobs = [(a, j) for j in range(CAST_JOBS) for a in range(3)]
    for a in range(3):
        copy_in(a, 0).start()
        copy_in(a, 1).start()

    x = x_ref[...]
    hn = _rms(x, g_ref[...]).astype(BF16)

    def up(f0):
        return (jnp.dot(hn, w1_ref[:, f0:f0 + tf], preferred_element_type=F32),
                jnp.dot(hn, w3_ref[:, f0:f0 + tf], preferred_element_type=F32))

    starts = list(range(0, w1_ref.shape[1], tf))
    per_iter = -(-len(jobs) // len(starts))
    ahead = [up(f0) for f0 in starts[:FFN_LOOKAHEAD]]
    for idx, f0 in enumerate(starts):
        h1, h3 = ahead.pop(0)
        if idx + FFN_LOOKAHEAD < len(starts):
            ahead.append(up(starts[idx + FFN_LOOKAHEAD]))
        hid[:, f0:f0 + tf] = (_silu(h1) * h3).astype(BF16)
        for job in jobs[idx * per_iter:(idx + 1) * per_iter]:
            cast_job(*job)
    o_ref[...] = x + jnp.dot(hid[...], w2_ref[...], preferred_element_type=F32)

    for a in range(3):
        copy_out(a, CAST_JOBS - 2).wait()
        copy_out(a, CAST_JOBS - 1).wait()


def _ffn(h2d, g, w1, w3, w2, cast_src, *, tm, tf):
    t, d = h2d.shape
    ff = w1.shape[1]
    n_steps = t // tm
    flat = [w.reshape(-1, w.shape[-1]) for w in cast_src]
    chunk_rows = [w.shape[0] // (n_steps * CAST_JOBS) for w in flat]
    assert all(w.shape[0] == r * n_steps * CAST_JOBS and r % 16 == 0 for w, r in zip(flat, chunk_rows))
    resident = lambda shape: pl.BlockSpec(shape, lambda i: (0, 0), pipeline_mode=pl.Buffered(1))
    any_spec = pl.BlockSpec(memory_space=pl.ANY)
    outs = pl.pallas_call(
        functools.partial(_ffn_kernel, tf=tf),
        out_shape=(jax.ShapeDtypeStruct((t, d), F32),
                   *[jax.ShapeDtypeStruct(w.shape, BF16) for w in flat]),
        grid=(n_steps,),
        in_specs=[
            pl.BlockSpec((tm, d), lambda i: (i, 0)),
            pl.BlockSpec((1, d), lambda i: (0, 0)),
            resident((d, ff)),
            resident((d, ff)),
            resident((ff, d)),
            any_spec, any_spec, any_spec,
        ],
        out_specs=(pl.BlockSpec((tm, d), lambda i: (i, 0)), any_spec, any_spec, any_spec),
        scratch_shapes=[pltpu.VMEM((tm, ff), BF16)]
        + [pltpu.VMEM((2, r, w.shape[1]), F32) for w, r in zip(flat, chunk_rows)]
        + [pltpu.VMEM((2, r, w.shape[1]), BF16) for w, r in zip(flat, chunk_rows)]
        + [pltpu.SemaphoreType.DMA((6,)), pltpu.SemaphoreType.DMA((6,))],
        compiler_params=pltpu.CompilerParams(
            dimension_semantics=("arbitrary",), vmem_limit_bytes=VMEM_LIMIT),
        name="dense_ffn",
    )(h2d, g.reshape(1, d), w1.astype(BF16), w3.astype(BF16), w2.astype(BF16), *flat)
    return outs[0], [o.reshape(w.shape) for o, w in zip(outs[1:], cast_src)]


ROUTE_TILE = 512
EXPERT_TILE = 512
EXPERT_UP_CHUNK = 256
SORT_ROWS = 2 * ROUTE_TILE + N_EXPERTS * SUBLANES
SEG_CHUNKS = tuple(SUBLANES << j for j in range((ROUTE_TILE // SUBLANES).bit_length() - 1, -1, -1))


def _router_kernel(x_ref, g_ref, rwt_ref, col_ref, rowp_ref, seg_ref, cnt_ref, carry, upper):
    i = pl.program_id(0)
    tb = ROUTE_TILE
    ne = N_EXPERTS

    @pl.when(i == 0)
    def _():
        carry[...] = jnp.zeros(carry.shape, F32)
        rr = lax.broadcasted_iota(jnp.int32, (tb, tb), 0)
        cc = lax.broadcasted_iota(jnp.int32, (tb, tb), 1)
        upper[...] = jnp.where(rr < cc, 1.0, 0.0).astype(BF16)

    def nt(a, bmat):
        return lax.dot_general(a, bmat, (((1,), (1,)), ((), ())), preferred_element_type=F32)

    hn = _rms(x_ref[...], g_ref[...])
    rwt = rwt_ref[...]
    hn_hi = hn.astype(BF16)
    hn_lo = (hn - hn_hi.astype(F32)).astype(BF16)
    rw_hi = rwt.astype(BF16)
    rw_lo = (rwt - rw_hi.astype(F32)).astype(BF16)
    logits_t = nt(rw_hi, hn_hi) + (nt(rw_hi, hn_lo) + nt(rw_lo, hn_hi))
    lg = logits_t[0:ne, :]
    sub = lax.broadcasted_iota(jnp.int32, (ne, tb), 0)
    m1 = jnp.max(lg, axis=0, keepdims=True)
    i1 = jnp.min(jnp.where(lg == m1, sub, ne), axis=0, keepdims=True)
    lg2 = jnp.where(sub == i1, -jnp.inf, lg)
    m2 = jnp.max(lg2, axis=0, keepdims=True)
    i2 = jnp.min(jnp.where(lg2 == m2, sub, ne), axis=0, keepdims=True)
    e2 = jnp.exp(m2 - m1)
    den = 1.0 + e2

    mem = jnp.where((sub == i1) | (sub == i2), 1.0, 0.0)
    mem_rows = jnp.concatenate([mem, jnp.zeros((LANES - ne, tb), F32)], axis=0).astype(BF16)
    before = jnp.dot(mem_rows[0:2 * ne, :], upper[...], preferred_element_type=F32)[0:ne, :]

    n = nt(jnp.ones((2 * SUBLANES, tb), BF16), mem_rows)[0:SUBLANES, :]
    n8 = jnp.floor((n + (SUBLANES - 1)) / SUBLANES) * SUBLANES
    incl = n8
    for sh in (1, 2, 4):
        incl = incl + pltpu.roll(incl, sh, axis=1)
    seg8 = incl - n8
    seg8_t = jnp.transpose(jnp.broadcast_to(seg8[0:1, :], (LANES, LANES)))[0:ne, :]
    pos = before + jnp.concatenate([seg8_t] * (tb // LANES), axis=1)
    rp1 = jnp.sum(jnp.where(sub == i1, pos, 0.0), axis=0, keepdims=True)
    rp2 = jnp.sum(jnp.where(sub == i2, pos, 0.0), axis=0, keepdims=True)

    rowp_ref[0] = jnp.where(sub == 0, rp1, jnp.where(sub == 1, rp2, 0.0)).astype(jnp.int32)
    table = jnp.where(sub == 0, 1.0 / den,
                      jnp.where(sub == 1, e2 / den,
                                jnp.where(sub == 2, rp1, jnp.where(sub == 3, rp2, 0.0))))
    col_ref[...] = jnp.transpose(
        jnp.concatenate([table, jnp.zeros((LANES - ne, tb), F32)], axis=0))
    fld = lax.broadcasted_iota(jnp.int32, (SUBLANES, LANES), 0)
    seg_ref[0] = jnp.where(fld == 0, n8, jnp.where(fld == 1, seg8,
                                                   jnp.where(fld == 2, carry[...], 0.0))).astype(jnp.int32)
    carry[...] = carry[...] + n8
    cnt_ref[...] = carry[...]


def _router(h2d, g, router_w):
    t, d = h2d.shape
    tb = ROUTE_TILE
    rwt = jnp.zeros((LANES, d), F32).at[:N_EXPERTS, :].set(jnp.transpose(router_w))
    return pl.pallas_call(
        _router_kernel,
        out_shape=(
            jax.ShapeDtypeStruct((t, LANES), F32),
            jax.ShapeDtypeStruct((t // tb, SUBLANES, tb), jnp.int32),
            jax.ShapeDtypeStruct((t // tb, SUBLANES, LANES), jnp.int32),
            jax.ShapeDtypeStruct((SUBLANES, LANES), F32),
        ),
        grid=(t // tb,),
        in_specs=[
            pl.BlockSpec((tb, d), lambda i: (i, 0)),
            pl.BlockSpec((1, d), lambda i: (0, 0)),
            pl.BlockSpec((LANES, d), lambda i: (0, 0)),
        ],
        out_specs=(
            pl.BlockSpec((tb, LANES), lambda i: (i, 0)),
            pl.BlockSpec((1, SUBLANES, tb), lambda i: (i, 0, 0)),
            pl.BlockSpec((1, SUBLANES, LANES), lambda i: (i, 0, 0)),
            pl.BlockSpec((SUBLANES, LANES), lambda i: (0, 0)),
        ),
        scratch_shapes=[pltpu.VMEM((SUBLANES, LANES), F32), pltpu.VMEM((tb, tb), BF16)],
        compiler_params=pltpu.CompilerParams(
            dimension_semantics=("arbitrary",), vmem_limit_bytes=VMEM_LIMIT),
        name="moe_router",
    )(h2d, g.reshape(1, d), rwt)


def _segment_copies(seg_ref, off_ref, make_copy, do_start):
    for e in range(N_EXPERTS):
        n8 = seg_ref[0, 0, e]
        tile_row = seg_ref[0, 1, e]
        group_row = off_ref[e] + seg_ref[0, 2, e]
        for c in SEG_CHUNKS:
            done = n8 & ~(2 * c - 1)

            @pl.when((n8 & c) != 0)
            def _():
                cp = make_copy(pl.multiple_of(tile_row + done, SUBLANES),
                               pl.multiple_of(group_row + done, SUBLANES), c)
                cp.start() if do_start else cp.wait()


def _dispatch_kernel(off_ref, pad_lo_ref, pad_hi_ref, na_ref, h_ref, g_ref, rowp_ref, seg_ref,
                     seg_prev_ref, xs_ref, sorted_s, zblk, sem, zsem):
    tb = ROUTE_TILE
    tm = EXPERT_TILE
    i = pl.program_id(0)
    slot = i % 2

    @pl.when(i == 0)
    def _():
        zblk[...] = jnp.zeros(zblk.shape, F32)
        n_blocks = xs_ref.shape[0] // tm

        def fill(do_start):
            def pad_group(r, c):
                cp = pltpu.make_async_copy(
                    zblk.at[pl.ds(0, SUBLANES)],
                    xs_ref.at[pl.ds(pl.multiple_of(r * SUBLANES, SUBLANES), SUBLANES)], zsem)
                cp.start() if do_start else cp.wait()
                return c

            def tail_blk(b, c):
                cp = pltpu.make_async_copy(
                    zblk, xs_ref.at[pl.ds(pl.multiple_of(b * tm, tm), tm)], zsem)
                cp.start() if do_start else cp.wait()
                return c

            for e in range(N_EXPERTS):
                lax.fori_loop(pad_lo_ref[e], pad_hi_ref[e], pad_group, 0)
            lax.fori_loop(na_ref[0], n_blocks, tail_blk, 0)

        fill(True)
        fill(False)

    hn = _rms(h_ref[...], g_ref[...]).astype(BF16)
    row = lax.broadcasted_iota(jnp.int32, (SORT_ROWS, tb), 0)
    hit = (row == rowp_ref[0, 0:1, :]) | (row == rowp_ref[0, 1:2, :])
    sorted_s[slot] = jnp.dot(jnp.where(hit, 1.0, 0.0).astype(BF16), hn, preferred_element_type=F32)

    def copier(buf):
        def make_copy(tile_row, group_row, rows):
            return pltpu.make_async_copy(sorted_s.at[buf, pl.ds(tile_row, rows)],
                                         xs_ref.at[pl.ds(group_row, rows)], sem.at[buf])
        return make_copy

    _segment_copies(seg_ref, off_ref, copier(slot), True)

    @pl.when(i > 0)
    def _():
        _segment_copies(seg_prev_ref, off_ref, copier(1 - slot), False)

    @pl.when(i == pl.num_programs(0) - 1)
    def _():
        _segment_copies(seg_ref, off_ref, copier(slot), False)


def _dispatch(h2d, g, rowp, seg, off, pad_lo, pad_hi, n_active, n_rows):
    t, d = h2d.shape
    tb = ROUTE_TILE
    seg_spec = lambda index: pl.BlockSpec((1, SUBLANES, LANES), index, memory_space=pltpu.SMEM)
    grid_spec = pltpu.PrefetchScalarGridSpec(
        num_scalar_prefetch=4,
        grid=(t // tb,),
        in_specs=[
            pl.BlockSpec((tb, d), lambda i, *_: (i, 0)),
            pl.BlockSpec((1, d), lambda i, *_: (0, 0)),
            pl.BlockSpec((1, SUBLANES, tb), lambda i, *_: (i, 0, 0)),
            seg_spec(lambda i, *_: (i, 0, 0)),
            seg_spec(lambda i, *_: (jnp.maximum(i - 1, 0), 0, 0)),
        ],
        out_specs=pl.BlockSpec(memory_space=pl.ANY),
        scratch_shapes=[pltpu.VMEM((2, SORT_ROWS, d), F32),
                        pltpu.VMEM((EXPERT_TILE, d), F32),
                        pltpu.SemaphoreType.DMA((2,)),
                        pltpu.SemaphoreType.DMA(())],
    )
    return pl.pallas_call(
        _dispatch_kernel,
        out_shape=jax.ShapeDtypeStruct((n_rows, d), F32),
        grid_spec=grid_spec,
        compiler_params=pltpu.CompilerParams(
            dimension_semantics=("arbitrary",), has_side_effects=True,
            vmem_limit_bytes=VMEM_LIMIT),
        name="moe_dispatch",
    )(off, pad_lo, pad_hi, n_active, h2d, g.reshape(1, d), rowp, seg, seg)


def _expert_kernel(be_ref, na_ref, vr_ref, x_ref, w1_ref, w3_ref, w2_ref, y_ref, hid):
    i = pl.program_id(0)
    f = pl.program_id(1)
    tm = y_ref.shape[0]

    def swiglu_rows(rows):
        x = x_ref[0:rows, :].astype(BF16)

        def up(f0):
            return (jnp.dot(x, w1_ref[0, :, f0:f0 + EXPERT_UP_CHUNK], preferred_element_type=F32),
                    jnp.dot(x, w3_ref[0, :, f0:f0 + EXPERT_UP_CHUNK], preferred_element_type=F32))

        starts = list(range(0, w1_ref.shape[2], EXPERT_UP_CHUNK))
        ahead = [up(f0) for f0 in starts[:FFN_LOOKAHEAD]]
        for idx, f0 in enumerate(starts):
            h1, h3 = ahead.pop(0)
            if idx + FFN_LOOKAHEAD < len(starts):
                ahead.append(up(starts[idx + FFN_LOOKAHEAD]))
            hid[0:rows, f0:f0 + EXPERT_UP_CHUNK] = (_silu(h1) * h3).astype(BF16)
        part = jnp.dot(hid[0:rows, :], w2_ref[0], preferred_element_type=F32)

        @pl.when(f == 0)
        def _():
            y_ref[0:rows, :] = part
            if rows < tm:
                y_ref[rows:tm, :] = jnp.zeros((tm - rows, y_ref.shape[1]), F32)

        @pl.when(f > 0)
        def _():
            y_ref[0:rows, :] += part

    active = i < na_ref[0]
    half_full = vr_ref[i] <= tm // 2

    @pl.when(active & jnp.logical_not(half_full))
    def _():
        swiglu_rows(tm)

    @pl.when(active & half_full)
    def _():
        swiglu_rows(tm // 2)

    @pl.when(jnp.logical_not(active) & (f == 0))
    def _():
        y_ref[...] = jnp.zeros(y_ref.shape, F32)


def _experts(xs, blk_expert, n_active, valid_rows, w1, w3, w2, *, tf):
    r, d = xs.shape
    tm = EXPERT_TILE
    ff = w1.shape[2]
    nf = ff // tf

    def row_blk(i, f, be, na, vr):
        return (jnp.maximum(jnp.minimum(i, na[0] - 1), 0), 0)

    def f_blk(i, f, na):
        return jnp.where(i < na[0], f, nf - 1)

    grid_spec = pltpu.PrefetchScalarGridSpec(
        num_scalar_prefetch=3,
        grid=(r // tm, nf),
        in_specs=[
            pl.BlockSpec((tm, d), row_blk),
            pl.BlockSpec((1, d, tf), lambda i, f, be, na, vr: (be[i], 0, f_blk(i, f, na))),
            pl.BlockSpec((1, d, tf), lambda i, f, be, na, vr: (be[i], 0, f_blk(i, f, na))),
            pl.BlockSpec((1, tf, d), lambda i, f, be, na, vr: (be[i], f_blk(i, f, na), 0)),
        ],
        out_specs=pl.BlockSpec((tm, d), lambda i, f, be, na, vr: (i, 0)),
        scratch_shapes=[pltpu.VMEM((tm, tf), BF16)],
    )
    return pl.pallas_call(
        _expert_kernel,
        out_shape=jax.ShapeDtypeStruct((r, d), F32),
        grid_spec=grid_spec,
        compiler_params=pltpu.CompilerParams(
            dimension_semantics=("arbitrary", "arbitrary"),
            vmem_limit_bytes=VMEM_LIMIT),
        name="moe_experts",
    )(blk_expert, n_active, valid_rows, xs, w1, w3, w2)


def _combine_kernel(off_ref, h_ref, col_ref, seg_ref, seg_next_ref, y_ref, gf_ref, o_ref,
                    ysort, sem, *, apply_final_norm):
    tb = ROUTE_TILE
    i = pl.program_id(0)
    slot = i % 2

    def copier(buf):
        def make_copy(tile_row, group_row, rows):
            return pltpu.make_async_copy(y_ref.at[pl.ds(group_row, rows)],
                                         ysort.at[buf, pl.ds(tile_row, rows)], sem.at[buf])
        return make_copy

    @pl.when(i == 0)
    def _():
        ysort[...] = jnp.zeros(ysort.shape, F32)
        _segment_copies(seg_ref, off_ref, copier(0), True)

    @pl.when(i + 1 < pl.num_programs(0))
    def _():
        _segment_copies(seg_next_ref, off_ref, copier(1 - slot), True)

    _segment_copies(seg_ref, off_ref, copier(slot), False)

    col = col_ref[...]
    row = lax.broadcasted_iota(jnp.int32, (tb, SORT_ROWS), 1)
    yb = ysort[slot].astype(BF16)
    acc = None
    for k in range(2):
        pick = jnp.where(row == col[:, 2 + k:3 + k].astype(jnp.int32), 1.0, 0.0).astype(BF16)
        term = col[:, k:k + 1] * jnp.dot(pick, yb, preferred_element_type=F32)
        acc = term if acc is None else acc + term
    out = h_ref[...] + acc
    o_ref[...] = _rms(out, gf_ref[...]) if apply_final_norm else out


def _combine(h2d, col, seg, y, off, final_g, *, apply_final_norm):
    t, d = h2d.shape
    tb = ROUTE_TILE
    n_tiles = t // tb
    seg_spec = lambda index: pl.BlockSpec((1, SUBLANES, LANES), index, memory_space=pltpu.SMEM)
    grid_spec = pltpu.PrefetchScalarGridSpec(
        num_scalar_prefetch=1,
        grid=(n_tiles,),
        in_specs=[
            pl.BlockSpec((tb, d), lambda i, off: (i, 0)),
            pl.BlockSpec((tb, LANES), lambda i, off: (i, 0)),
            seg_spec(lambda i, off: (i, 0, 0)),
            seg_spec(lambda i, off: (jnp.minimum(i + 1, n_tiles - 1), 0, 0)),
            pl.BlockSpec(memory_space=pl.ANY),
            pl.BlockSpec((1, d), lambda i, off: (0, 0)),
        ],
        out_specs=pl.BlockSpec((tb, d), lambda i, off: (i, 0)),
        scratch_shapes=[pltpu.VMEM((2, SORT_ROWS, d), F32), pltpu.SemaphoreType.DMA((2,))],
    )
    return pl.pallas_call(
        functools.partial(_combine_kernel, apply_final_norm=apply_final_norm),
        out_shape=jax.ShapeDtypeStruct((t, d), F32),
        grid_spec=grid_spec,
        compiler_params=pltpu.CompilerParams(
            dimension_semantics=("arbitrary",), vmem_limit_bytes=VMEM_LIMIT),
        name="moe_combine",
    )(off, h2d, col, seg, seg, y, final_g.reshape(1, d))


def _moe(h2d, g, router_w, w1, w3, w2, final_g, *, apply_final_norm):
    t, d = h2d.shape
    tm = EXPERT_TILE
    n_tiles = t // ROUTE_TILE
    n_rows = 2 * t + n_tiles * N_EXPERTS * SUBLANES + N_EXPERTS * tm
    n_rows = -(-n_rows // tm) * tm
    col, rowp, seg, cnt = _router(h2d, g, router_w)
    counts = cnt[0, :N_EXPERTS].astype(jnp.int32)
    blocks = (counts + tm - 1) // tm
    ends = jnp.cumsum(blocks)
    off = (ends - blocks) * tm
    n_active = ends[-1:]
    blk = jnp.arange(n_rows // tm, dtype=jnp.int32)
    blk_expert = jnp.sum(jnp.minimum(blk, n_active - 1)[:, None] >= ends[None, :], axis=1)
    blk_expert = blk_expert.astype(jnp.int32)
    valid_rows = jnp.clip(counts[blk_expert] - (blk - (ends - blocks)[blk_expert]) * tm, 0, tm)
    xs = _dispatch(h2d, g, rowp, seg, off, (off + counts) // SUBLANES,
                   (off + blocks * tm) // SUBLANES, n_active, n_rows)
    y = _experts(xs, blk_expert, n_active, valid_rows, w1, w3, w2, tf=1792)
    return _combine(h2d, col, seg, y, off, final_g, apply_final_norm=apply_final_norm)


def kernel(x, e_norm1, e_w_in, e_conv_w, e_conv_b, e_ln_g, e_ln_b, e_pool_w, e_pool_scale,
           e_w_out, e_norm2, e_ff_w1, e_ff_w3, e_ff_w2, o_norm1, o_w_qkv, o_b_qkv, o_sinks,
           o_w_o, o_b_o, o_norm2, o_router, o_exp_w1, o_exp_w3, o_exp_w2, final_norm):
    b, s, d = x.shape
    assert DEPTH % 2 == 0, "the final norm is fused into the last (odd, MoE) layer"
    h = x
    for layer in range(DEPTH):
        i = layer // 2
        if layer % 2 == 0:
            h = _even_mixer(h, e_norm1[i], e_w_in[i], e_conv_w[i], e_conv_b[i], e_ln_g[i],
                            e_ln_b[i], e_pool_w[i], e_pool_scale[i], e_w_out[i])
            h, expert_w = _ffn(h.reshape(b * s, d), e_norm2[i], e_ff_w1[i], e_ff_w3[i], e_ff_w2[i],
                               (o_exp_w1[i], o_exp_w3[i], o_exp_w2[i]), tm=512, tf=256)
            h = h.reshape(b, s, d)
        else:
            h = _attn_mixer(h, o_norm1[i], o_w_qkv[i], o_b_qkv[i], o_sinks[i], o_w_o[i],
                            o_b_o[i])
            h = _moe(h.reshape(b * s, d), o_norm2[i], o_router[i], *expert_w, final_norm,
                     apply_final_norm=(layer == DEPTH - 1)).reshape(b, s, d)
    return h
```

```python
import functools

import jax
import jax.numpy as jnp
from jax import lax
from jax.experimental import pallas as pl
from jax.experimental.pallas import tpu as pltpu

F32 = jnp.float32
BF16 = jnp.bfloat16

D_MODEL = 1024
DEPTH = 4
RMS_EPS = 1e-5
LN_EPS = 1e-5

D_CONV = 512
D_POOL = 512
CONV_K = 31
POOL_WINDOWS = (2, 4, 8, 16)
POOL_GROUP = 128
D_IN_EVEN = 2 * D_CONV + D_POOL

HEAD_DIM = 64
N_Q_HEADS = 16
N_KV_HEADS = 4
ATTN_BLOCK = 128
D_Q = N_Q_HEADS * HEAD_DIM
D_KV = N_KV_HEADS * HEAD_DIM
D_QKV = D_Q + 2 * D_KV

N_EXPERTS = 8

LANES = 128
SUBLANES = 8
SEQ_TILE = 512
HALO = 32
CONV_ROWS = 32
SCORE_LOOKAHEAD = 2
FFN_LOOKAHEAD = 2
VMEM_LIMIT = 56 * 1024 * 1024


def _rms(x, g):
    ms = jnp.mean(x * x, axis=-1, keepdims=True)
    return x * lax.rsqrt(ms + RMS_EPS) * g


def _silu(x):
    return x * jax.nn.sigmoid(x)


def _even_mixer_kernel(x_ref, g_ref, win_ref, cw_ref, cb_ref, lg_ref, lb_ref,
                       pw_ref, ps_ref, wout_ref, o_ref,
                       abuf, ashift, cwb, bbuf, p1, p2, p3, cat):
    s = pl.program_id(1)
    ts = SEQ_TILE
    rows = HALO + ts

    @pl.when(s == 0)
    def _():
        abuf[0:HALO, :] = jnp.zeros((HALO, D_CONV), F32)
        bbuf[0:HALO, :] = jnp.zeros((HALO, D_POOL), F32)

    x = x_ref[0]
    hn = _rms(x, g_ref[...]).astype(BF16)
    u = jnp.dot(hn, win_ref[...], preferred_element_type=F32)
    abuf[HALO:rows, :] = u[:, :D_CONV] * jax.nn.sigmoid(u[:, D_CONV:2 * D_CONV])
    bbuf[HALO:rows, :] = u[:, 2 * D_CONV:]

    a_all = abuf[...]
    for r in range(1, SUBLANES):
        ashift[r - 1] = pltpu.roll(a_all, rows - r, axis=0)
    for k in range(CONV_K):
        cwb[k] = jnp.broadcast_to(cw_ref[k:k + 1, :], (SUBLANES, D_CONV))
    for c in range(ts // CONV_ROWS):
        r0 = c * CONV_ROWS
        acc = jnp.broadcast_to(cb_ref[...], (CONV_ROWS, D_CONV))
        for k in range(CONV_K):
            q8, r = divmod(HALO - (CONV_K - 1) + k, SUBLANES)
            src = abuf if r == 0 else ashift.at[r - 1]
            start = r0 + q8 * SUBLANES
            tap = jnp.concatenate([cwb[k]] * (CONV_ROWS // SUBLANES), axis=0)
            acc = acc + tap * src[start:start + CONV_ROWS, :]
        mu = jnp.mean(acc, axis=-1, keepdims=True)
        d = acc - mu
        var = jnp.mean(d * d, axis=-1, keepdims=True)
        y = d * lax.rsqrt(var + LN_EPS) * lg_ref[...] + lb_ref[...]
        cat[r0:r0 + CONV_ROWS, 0:D_CONV] = _silu(y).astype(BF16)

    p1[8:rows, :] = bbuf[8:rows, :] + bbuf[7:rows - 1, :]
    p2[16:rows, 0:384] = p1[16:rows, 128:512] + p1[14:rows - 2, 128:512]
    p3[24:rows, 0:256] = p2[24:rows, 128:384] + p2[20:rows - 4, 128:384]
    s16 = p3[HALO:rows, 128:256] + p3[HALO - 8:rows - 8, 128:256]
    sums = (p1[HALO:rows, 0:128], p2[HALO:rows, 0:128], p3[HALO:rows, 0:128], s16)
    pos1 = (s * ts + 1 + lax.broadcasted_iota(jnp.int32, (ts, 1), 0)).astype(F32)
    for g, w in enumerate(POOL_WINDOWS):
        cnt = jnp.minimum(pos1, float(w))
        pg = sums[g] / cnt - bbuf[HALO:rows, g * POOL_GROUP:(g + 1) * POOL_GROUP]
        pm = jnp.dot(pg.astype(BF16), pw_ref[g], preferred_element_type=F32)
        pm = pm * ps_ref[:, g * POOL_GROUP:(g + 1) * POOL_GROUP]
        cat[:, D_CONV + g * POOL_GROUP:D_CONV + (g + 1) * POOL_GROUP] = pm.astype(BF16)

    o_ref[0] = x + jnp.dot(cat[...], wout_ref[...], preferred_element_type=F32)

    abuf[0:HALO, :] = abuf[ts:rows, :]
    bbuf[0:HALO, :] = bbuf[ts:rows, :]


def _even_mixer(h, g, w_in, conv_w, conv_b, ln_g, ln_b, pool_w, pool_scale, w_out):
    b, s, d = h.shape
    ts = SEQ_TILE
    rows = HALO + ts
    const = lambda shape: pl.BlockSpec(shape, lambda i, j: (0,) * len(shape))
    return pl.pallas_call(
        _even_mixer_kernel,
        out_shape=jax.ShapeDtypeStruct(h.shape, F32),
        grid=(b, s // ts),
        in_specs=[
            pl.BlockSpec((1, ts, d), lambda i, j: (i, j, 0)),
            const((1, d)),
            const((d, D_IN_EVEN)),
            const((CONV_K, D_CONV)),
            const((1, D_CONV)),
            const((1, D_CONV)),
            const((1, D_CONV)),
            const((len(POOL_WINDOWS), POOL_GROUP, POOL_GROUP)),
            const((1, D_POOL)),
            const((d, d)),
        ],
        out_specs=pl.BlockSpec((1, ts, d), lambda i, j: (i, j, 0)),
        scratch_shapes=[
            pltpu.VMEM((rows, D_CONV), F32),
            pltpu.VMEM((SUBLANES - 1, rows, D_CONV), F32),
            pltpu.VMEM((CONV_K, SUBLANES, D_CONV), F32),
            pltpu.VMEM((rows, D_POOL), F32),
            pltpu.VMEM((rows, D_POOL), F32),
            pltpu.VMEM((rows, 384), F32),
            pltpu.VMEM((rows, 256), F32),
            pltpu.VMEM((ts, d), BF16),
        ],
        compiler_params=pltpu.CompilerParams(
            dimension_semantics=("arbitrary", "arbitrary"),
            vmem_limit_bytes=VMEM_LIMIT),
        name="even_mixer",
    )(h, g.reshape(1, d), w_in.astype(BF16), conv_w, conv_b.reshape(1, -1),
      ln_g.reshape(1, -1), ln_b.reshape(1, -1), pool_w.astype(BF16),
      pool_scale.reshape(1, -1), w_out.astype(BF16))


def _attn_kernel(sink_ref, x_ref, g_ref, wqkv_ref, bqkv_ref, wot_ref, bo_ref, o_ref,
                 kbuf, vtbuf, otbuf):
    s = pl.program_id(1)
    ts = SEQ_TILE
    blk = ATTN_BLOCK
    group = N_Q_HEADS // N_KV_HEADS

    @pl.when(s == 0)
    def _():
        kbuf[0:blk, :] = jnp.zeros((blk, N_KV_HEADS * LANES), BF16)
        vtbuf[:, 0:blk] = jnp.zeros((D_KV, blk), BF16)

    x = x_ref[0]
    hn = _rms(x, g_ref[...]).astype(BF16)
    u = jnp.dot(hn, wqkv_ref[...], preferred_element_type=F32) + bqkv_ref[...]
    q = (u[:, :D_Q] * (HEAD_DIM ** -0.5)).astype(BF16)

    lo_t = lax.broadcasted_iota(jnp.int32, (ts, LANES), 1) < HEAD_DIM
    for p in range(N_KV_HEADS // 2):
        t = u[:, D_Q + p * LANES:D_Q + (p + 1) * LANES]
        r = pltpu.roll(t, HEAD_DIM, axis=1)
        kbuf[blk:blk + ts, (2 * p) * LANES:(2 * p + 1) * LANES] = jnp.where(lo_t, t, r).astype(BF16)
        kbuf[blk:blk + ts, (2 * p + 1) * LANES:(2 * p + 2) * LANES] = jnp.where(lo_t, r, t).astype(BF16)
    vtbuf[:, blk:blk + ts] = jnp.transpose(u[:, D_Q + D_KV:]).astype(BF16)

    lo_b = lax.broadcasted_iota(jnp.int32, (blk, LANES), 1) < HEAD_DIM
    zero_b = jnp.zeros((blk, LANES), BF16)
    key = lax.broadcasted_iota(jnp.int32, (2 * blk, blk), 0)
    qry = lax.broadcasted_iota(jnp.int32, (2 * blk, blk), 1)
    band = (key > qry) & (key <= qry + blk)
    band0 = band & ((key >= blk) | (s > 0))
    neg = jnp.concatenate([jnp.where(band, 0.0, -jnp.inf)] * group, axis=1)
    neg0 = jnp.concatenate([jnp.where(band0, 0.0, -jnp.inf)] * group, axis=1)

    def scores(j, g):
        kk = kbuf[j * blk:(j + 2) * blk, g * LANES:(g + 1) * LANES]
        qs = []
        sk = []
        for i in range(group):
            h = g * group + i
            qt = q[j * blk:(j + 1) * blk, (h // 2) * LANES:(h // 2 + 1) * LANES]
            qs.append(jnp.where(lo_b, qt, zero_b) if h % 2 == 0 else jnp.where(lo_b, zero_b, qt))
            sk.append(jnp.full((1, blk), sink_ref[h], F32))
        q4 = jnp.concatenate(qs, axis=0)
        st = lax.dot_general(kk, q4, (((1,), (1,)), ((), ())), preferred_element_type=F32)
        return st + (neg0 if j == 0 else neg), jnp.concatenate(sk, axis=1)

    pairs = [(j, g) for j in range(ts // blk) for g in range(N_KV_HEADS)]
    ahead = [scores(*p) for p in pairs[:SCORE_LOOKAHEAD]]
    for idx, (j, g) in enumerate(pairs):
        st, sink = ahead.pop(0)
        if idx + SCORE_LOOKAHEAD < len(pairs):
            ahead.append(scores(*pairs[idx + SCORE_LOOKAHEAD]))
        vt = vtbuf[g * HEAD_DIM:(g + 1) * HEAD_DIM, j * blk:(j + 2) * blk]
        m = jnp.maximum(jnp.max(st, axis=0, keepdims=True), sink)
        e = jnp.exp(st - m)
        denom = jnp.sum(e, axis=0, keepdims=True) + jnp.exp(sink - m)
        ot = jnp.dot(vt, e.astype(BF16), preferred_element_type=F32) * (1.0 / denom)
        for i in range(group):
            h = g * group + i
            otbuf[h * HEAD_DIM:(h + 1) * HEAD_DIM, j * blk:(j + 1) * blk] = (
                ot[:, i * blk:(i + 1) * blk].astype(BF16))

    proj_t = jnp.dot(wot_ref[...], otbuf[...], preferred_element_type=F32)
    o_ref[0] = x + jnp.transpose(proj_t) + bo_ref[...]

    kbuf[0:blk, :] = kbuf[ts:ts + blk, :]
    vtbuf[:, 0:blk] = vtbuf[:, ts:ts + blk]


def _attn_mixer(h, g, w_qkv, b_qkv, sinks, w_o, b_o):
    b, s, d = h.shape
    ts = SEQ_TILE
    const = lambda shape: pl.BlockSpec(shape, lambda i, j, sk: (0,) * len(shape))
    grid_spec = pltpu.PrefetchScalarGridSpec(
        num_scalar_prefetch=1,
        grid=(b, s // ts),
        in_specs=[
            pl.BlockSpec((1, ts, d), lambda i, j, sk: (i, j, 0)),
            const((1, d)),
            const((d, D_QKV)),
            const((1, D_QKV)),
            const((d, D_Q)),
            const((1, d)),
        ],
        out_specs=pl.BlockSpec((1, ts, d), lambda i, j, sk: (i, j, 0)),
        scratch_shapes=[
            pltpu.VMEM((ATTN_BLOCK + ts, N_KV_HEADS * LANES), BF16),
            pltpu.VMEM((D_KV, ATTN_BLOCK + ts), BF16),
            pltpu.VMEM((D_Q, ts), BF16),
        ],
    )
    return pl.pallas_call(
        _attn_kernel,
        out_shape=jax.ShapeDtypeStruct(h.shape, F32),
        grid_spec=grid_spec,
        compiler_params=pltpu.CompilerParams(
            dimension_semantics=("arbitrary", "arbitrary"),
            vmem_limit_bytes=VMEM_LIMIT),
        name="attn_mixer",
    )(sinks, h, g.reshape(1, d), w_qkv.astype(BF16), b_qkv.reshape(1, -1),
      jnp.transpose(w_o).astype(BF16), b_o.reshape(1, d))


def _ffn_kernel(x_ref, g_ref, w1_ref, w3_ref, w2_ref, o_ref, hid, *, tf):
    x = x_ref[...]
    hn = _rms(x, g_ref[...]).astype(BF16)

    def up(f0):
        return (jnp.dot(hn, w1_ref[:, f0:f0 + tf], preferred_element_type=F32),
                jnp.dot(hn, w3_ref[:, f0:f0 + tf], preferred_element_type=F32))

    starts = list(range(0, w1_ref.shape[1], tf))
    ahead = [up(f0) for f0 in starts[:FFN_LOOKAHEAD]]
    for idx, f0 in enumerate(starts):
        h1, h3 = ahead.pop(0)
        if idx + FFN_LOOKAHEAD < len(starts):
            ahead.append(up(starts[idx + FFN_LOOKAHEAD]))
        hid[:, f0:f0 + tf] = (_silu(h1) * h3).astype(BF16)
    o_ref[...] = x + jnp.dot(hid[...], w2_ref[...], preferred_element_type=F32)


def _ffn(h2d, g, w1, w3, w2, *, tm, tf):
    t, d = h2d.shape
    ff = w1.shape[1]
    resident = lambda shape: pl.BlockSpec(shape, lambda i: (0, 0), pipeline_mode=pl.Buffered(1))
    return pl.pallas_call(
        functools.partial(_ffn_kernel, tf=tf),
        out_shape=jax.ShapeDtypeStruct((t, d), F32),
        grid=(t // tm,),
        in_specs=[
            pl.BlockSpec((tm, d), lambda i: (i, 0)),
            pl.BlockSpec((1, d), lambda i: (0, 0)),
            resident((d, ff)),
            resident((d, ff)),
            resident((ff, d)),
        ],
        out_specs=pl.BlockSpec((tm, d), lambda i: (i, 0)),
        scratch_shapes=[pltpu.VMEM((tm, ff), BF16)],
        compiler_params=pltpu.CompilerParams(
            dimension_semantics=("arbitrary",), vmem_limit_bytes=VMEM_LIMIT),
        name="dense_ffn",
    )(h2d, g.reshape(1, d), w1.astype(BF16), w3.astype(BF16), w2.astype(BF16))


ROUTE_TILE = 512
EXPERT_TILE = 512
EXPERT_UP_CHUNK = 256
SORT_ROWS = 2 * ROUTE_TILE + N_EXPERTS * SUBLANES
SEG_CHUNKS = tuple(SUBLANES << j for j in range((ROUTE_TILE // SUBLANES).bit_length() - 1, -1, -1))


def _router_kernel(x_ref, g_ref, rwt_ref, col_ref, rowp_ref, seg_ref, cnt_ref, carry, upper):
    i = pl.program_id(0)
    tb = ROUTE_TILE
    ne = N_EXPERTS

    @pl.when(i == 0)
    def _():
        carry[...] = jnp.zeros(carry.shape, F32)
        rr = lax.broadcasted_iota(jnp.int32, (tb, tb), 0)
        cc = lax.broadcasted_iota(jnp.int32, (tb, tb), 1)
        upper[...] = jnp.where(rr < cc, 1.0, 0.0).astype(BF16)

    def nt(a, bmat):
        return lax.dot_general(a, bmat, (((1,), (1,)), ((), ())), preferred_element_type=F32)

    hn = _rms(x_ref[...], g_ref[...])
    rwt = rwt_ref[...]
    hn_hi = hn.astype(BF16)
    hn_lo = (hn - hn_hi.astype(F32)).astype(BF16)
    rw_hi = rwt.astype(BF16)
    rw_lo = (rwt - rw_hi.astype(F32)).astype(BF16)
    logits_t = nt(rw_hi, hn_hi) + (nt(rw_hi, hn_lo) + nt(rw_lo, hn_hi))
    lg = logits_t[0:ne, :]
    sub = lax.broadcasted_iota(jnp.int32, (ne, tb), 0)
    m1 = jnp.max(lg, axis=0, keepdims=True)
    i1 = jnp.min(jnp.where(lg == m1, sub, ne), axis=0, keepdims=True)
    lg2 = jnp.where(sub == i1, -jnp.inf, lg)
    m2 = jnp.max(lg2, axis=0, keepdims=True)
    i2 = jnp.min(jnp.where(lg2 == m2, sub, ne), axis=0, keepdims=True)
    e2 = jnp.exp(m2 - m1)
    den = 1.0 + e2

    mem = jnp.where((sub == i1) | (sub == i2), 1.0, 0.0)
    mem_rows = jnp.concatenate([mem, jnp.zeros((LANES - ne, tb), F32)], axis=0).astype(BF16)
    before = jnp.dot(mem_rows[0:2 * ne, :], upper[...], preferred_element_type=F32)[0:ne, :]

    n = nt(jnp.ones((2 * SUBLANES, tb), BF16), mem_rows)[0:SUBLANES, :]
    n8 = jnp.floor((n + (SUBLANES - 1)) / SUBLANES) * SUBLANES
    incl = n8
    for sh in (1, 2, 4):
        incl = incl + pltpu.roll(incl, sh, axis=1)
    seg8 = incl - n8
    seg8_t = jnp.transpose(jnp.broadcast_to(seg8[0:1, :], (LANES, LANES)))[0:ne, :]
    pos = before + jnp.concatenate([seg8_t] * (tb // LANES), axis=1)
    rp1 = jnp.sum(jnp.where(sub == i1, pos, 0.0), axis=0, keepdims=True)
    rp2 = jnp.sum(jnp.where(sub == i2, pos, 0.0), axis=0, keepdims=True)

    rowp_ref[0] = jnp.where(sub == 0, rp1, jnp.where(sub == 1, rp2, 0.0)).astype(jnp.int32)
    table = jnp.where(sub == 0, 1.0 / den,
                      jnp.where(sub == 1, e2 / den,
                                jnp.where(sub == 2, rp1, jnp.where(sub == 3, rp2, 0.0))))
    col_ref[...] = jnp.transpose(
        jnp.concatenate([table, jnp.zeros((LANES - ne, tb), F32)], axis=0))
    fld = lax.broadcasted_iota(jnp.int32, (SUBLANES, LANES), 0)
    seg_ref[0] = jnp.where(fld == 0, n8, jnp.where(fld == 1, seg8,
                                                   jnp.where(fld == 2, carry[...], 0.0))).astype(jnp.int32)
    carry[...] = carry[...] + n8
    cnt_ref[...] = carry[...]


def _router(h2d, g, router_w):
    t, d = h2d.shape
    tb = ROUTE_TILE
    rwt = jnp.zeros((LANES, d), F32).at[:N_EXPERTS, :].set(jnp.transpose(router_w))
    return pl.pallas_call(
        _router_kernel,
        out_shape=(
            jax.ShapeDtypeStruct((t, LANES), F32),
            jax.ShapeDtypeStruct((t // tb, SUBLANES, tb), jnp.int32),
            jax.ShapeDtypeStruct((t // tb, SUBLANES, LANES), jnp.int32),
            jax.ShapeDtypeStruct((SUBLANES, LANES), F32),
        ),
        grid=(t // tb,),
        in_specs=[
            pl.BlockSpec((tb, d), lambda i: (i, 0)),
            pl.BlockSpec((1, d), lambda i: (0, 0)),
            pl.BlockSpec((LANES, d), lambda i: (0, 0)),
        ],
        out_specs=(
            pl.BlockSpec((tb, LANES), lambda i: (i, 0)),
            pl.BlockSpec((1, SUBLANES, tb), lambda i: (i, 0, 0)),
            pl.BlockSpec((1, SUBLANES, LANES), lambda i: (i, 0, 0)),
            pl.BlockSpec((SUBLANES, LANES), lambda i: (0, 0)),
        ),
        scratch_shapes=[pltpu.VMEM((SUBLANES, LANES), F32), pltpu.VMEM((tb, tb), BF16)],
        compiler_params=pltpu.CompilerParams(
            dimension_semantics=("arbitrary",), vmem_limit_bytes=VMEM_LIMIT),
        name="moe_router",
    )(h2d, g.reshape(1, d), rwt)


def _segment_copies(seg_ref, off_ref, make_copy, do_start):
    for e in range(N_EXPERTS):
        n8 = seg_ref[0, 0, e]
        tile_row = seg_ref[0, 1, e]
        group_row = off_ref[e] + seg_ref[0, 2, e]
        for c in SEG_CHUNKS:
            done = n8 & ~(2 * c - 1)

            @pl.when((n8 & c) != 0)
            def _():
                cp = make_copy(pl.multiple_of(tile_row + done, SUBLANES),
                               pl.multiple_of(group_row + done, SUBLANES), c)
                cp.start() if do_start else cp.wait()


def _dispatch_kernel(off_ref, pad_lo_ref, pad_hi_ref, na_ref, h_ref, g_ref, rowp_ref, seg_ref,
                     seg_prev_ref, xs_ref, sorted_s, zblk, sem, zsem):
    tb = ROUTE_TILE
    tm = EXPERT_TILE
    i = pl.program_id(0)
    slot = i % 2

    @pl.when(i == 0)
    def _():
        zblk[...] = jnp.zeros(zblk.shape, F32)
        n_blocks = xs_ref.shape[0] // tm

        def fill(do_start):
            def pad_group(r, c):
                cp = pltpu.make_async_copy(
                    zblk.at[pl.ds(0, SUBLANES)],
                    xs_ref.at[pl.ds(pl.multiple_of(r * SUBLANES, SUBLANES), SUBLANES)], zsem)
                cp.start() if do_start else cp.wait()
                return c

            def tail_blk(b, c):
                cp = pltpu.make_async_copy(
                    zblk, xs_ref.at[pl.ds(pl.multiple_of(b * tm, tm), tm)], zsem)
                cp.start() if do_start else cp.wait()
                return c

            for e in range(N_EXPERTS):
                lax.fori_loop(pad_lo_ref[e], pad_hi_ref[e], pad_group, 0)
            lax.fori_loop(na_ref[0], n_blocks, tail_blk, 0)

        fill(True)
        fill(False)

    hn = _rms(h_ref[...], g_ref[...]).astype(BF16)
    row = lax.broadcasted_iota(jnp.int32, (SORT_ROWS, tb), 0)
    hit = (row == rowp_ref[0, 0:1, :]) | (row == rowp_ref[0, 1:2, :])
    sorted_s[slot] = jnp.dot(jnp.where(hit, 1.0, 0.0).astype(BF16), hn, preferred_element_type=F32)

    def copier(buf):
        def make_copy(tile_row, group_row, rows):
            return pltpu.make_async_copy(sorted_s.at[buf, pl.ds(tile_row, rows)],
                                         xs_ref.at[pl.ds(group_row, rows)], sem.at[buf])
        return make_copy

    _segment_copies(seg_ref, off_ref, copier(slot), True)

    @pl.when(i > 0)
    def _():
        _segment_copies(seg_prev_ref, off_ref, copier(1 - slot), False)

    @pl.when(i == pl.num_programs(0) - 1)
    def _():
        _segment_copies(seg_ref, off_ref, copier(slot), False)


def _dispatch(h2d, g, rowp, seg, off, pad_lo, pad_hi, n_active, n_rows):
    t, d = h2d.shape
    tb = ROUTE_TILE
    seg_spec = lambda index: pl.BlockSpec((1, SUBLANES, LANES), index, memory_space=pltpu.SMEM)
    grid_spec = pltpu.PrefetchScalarGridSpec(
        num_scalar_prefetch=4,
        grid=(t // tb,),
        in_specs=[
            pl.BlockSpec((tb, d), lambda i, *_: (i, 0)),
            pl.BlockSpec((1, d), lambda i, *_: (0, 0)),
            pl.BlockSpec((1, SUBLANES, tb), lambda i, *_: (i, 0, 0)),
            seg_spec(lambda i, *_: (i, 0, 0)),
            seg_spec(lambda i, *_: (jnp.maximum(i - 1, 0), 0, 0)),
        ],
        out_specs=pl.BlockSpec(memory_space=pl.ANY),
        scratch_shapes=[pltpu.VMEM((2, SORT_ROWS, d), F32),
                        pltpu.VMEM((EXPERT_TILE, d), F32),
                        pltpu.SemaphoreType.DMA((2,)),
                        pltpu.SemaphoreType.DMA(())],
    )
    return pl.pallas_call(
        _dispatch_kernel,
        out_shape=jax.ShapeDtypeStruct((n_rows, d), F32),
        grid_spec=grid_spec,
        compiler_params=pltpu.CompilerParams(
            dimension_semantics=("arbitrary",), has_side_effects=True,
            vmem_limit_bytes=VMEM_LIMIT),
        name="moe_dispatch",
    )(off, pad_lo, pad_hi, n_active, h2d, g.reshape(1, d), rowp, seg, seg)


def _expert_kernel(be_ref, na_ref, vr_ref, x_ref, w1_ref, w3_ref, w2_ref, y_ref, hid):
    i = pl.program_id(0)
    f = pl.program_id(1)
    tm = y_ref.shape[0]

    def swiglu_rows(rows):
        x = x_ref[0:rows, :].astype(BF16)

        def up(f0):
            return (jnp.dot(x, w1_ref[0, :, f0:f0 + EXPERT_UP_CHUNK], preferred_element_type=F32),
                    jnp.dot(x, w3_ref[0, :, f0:f0 + EXPERT_UP_CHUNK], preferred_element_type=F32))

        starts = list(range(0, w1_ref.shape[2], EXPERT_UP_CHUNK))
        ahead = [up(f0) for f0 in starts[:FFN_LOOKAHEAD]]
        for idx, f0 in enumerate(starts):
            h1, h3 = ahead.pop(0)
            if idx + FFN_LOOKAHEAD < len(starts):
                ahead.append(up(starts[idx + FFN_LOOKAHEAD]))
            hid[0:rows, f0:f0 + EXPERT_UP_CHUNK] = (_silu(h1) * h3).astype(BF16)
        part = jnp.dot(hid[0:rows, :], w2_ref[0], preferred_element_type=F32)

        @pl.when(f == 0)
        def _():
            y_ref[0:rows, :] = part
            if rows < tm:
                y_ref[rows:tm, :] = jnp.zeros((tm - rows, y_ref.shape[1]), F32)

        @pl.when(f > 0)
        def _():
            y_ref[0:rows, :] += part

    active = i < na_ref[0]
    half_full = vr_ref[i] <= tm // 2

    @pl.when(active & jnp.logical_not(half_full))
    def _():
        swiglu_rows(tm)

    @pl.when(active & half_full)
    def _():
        swiglu_rows(tm // 2)

    @pl.when(jnp.logical_not(active) & (f == 0))
    def _():
        y_ref[...] = jnp.zeros(y_ref.shape, F32)


def _experts(xs, blk_expert, n_active, valid_rows, w1, w3, w2, *, tf):
    r, d = xs.shape
    tm = EXPERT_TILE
    ff = w1.shape[2]
    nf = ff // tf

    def row_blk(i, f, be, na, vr):
        return (jnp.maximum(jnp.minimum(i, na[0] - 1), 0), 0)

    def f_blk(i, f, na):
        return jnp.where(i < na[0], f, nf - 1)

    grid_spec = pltpu.PrefetchScalarGridSpec(
        num_scalar_prefetch=3,
        grid=(r // tm, nf),
        in_specs=[
            pl.BlockSpec((tm, d), row_blk),
            pl.BlockSpec((1, d, tf), lambda i, f, be, na, vr: (be[i], 0, f_blk(i, f, na))),
            pl.BlockSpec((1, d, tf), lambda i, f, be, na, vr: (be[i], 0, f_blk(i, f, na))),
            pl.BlockSpec((1, tf, d), lambda i, f, be, na, vr: (be[i], f_blk(i, f, na), 0)),
        ],
        out_specs=pl.BlockSpec((tm, d), lambda i, f, be, na, vr: (i, 0)),
        scratch_shapes=[pltpu.VMEM((tm, tf), BF16)],
    )
    return pl.pallas_call(
        _expert_kernel,
        out_shape=jax.ShapeDtypeStruct((r, d), F32),
        grid_spec=grid_spec,
        compiler_params=pltpu.CompilerParams(
            dimension_semantics=("arbitrary", "arbitrary"),
            vmem_limit_bytes=VMEM_LIMIT),
        name="moe_experts",
    )(blk_expert, n_active, valid_rows, xs, w1, w3, w2)


def _combine_kernel(off_ref, h_ref, col_ref, seg_ref, seg_next_ref, y_ref, gf_ref, o_ref,
                    ysort, sem, *, apply_final_norm):
    tb = ROUTE_TILE
    i = pl.program_id(0)
    slot = i % 2

    def copier(buf):
        def make_copy(tile_row, group_row, rows):
            return pltpu.make_async_copy(y_ref.at[pl.ds(group_row, rows)],
                                         ysort.at[buf, pl.ds(tile_row, rows)], sem.at[buf])
        return make_copy

    @pl.when(i == 0)
    def _():
        ysort[...] = jnp.zeros(ysort.shape, F32)
        _segment_copies(seg_ref, off_ref, copier(0), True)

    @pl.when(i + 1 < pl.num_programs(0))
    def _():
        _segment_copies(seg_next_ref, off_ref, copier(1 - slot), True)

    _segment_copies(seg_ref, off_ref, copier(slot), False)

    col = col_ref[...]
    row = lax.broadcasted_iota(jnp.int32, (tb, SORT_ROWS), 1)
    yb = ysort[slot].astype(BF16)
    weights = (jnp.where(row == col[:, 2:3].astype(jnp.int32), col[:, 0:1], 0.0)
               + jnp.where(row == col[:, 3:4].astype(jnp.int32), col[:, 1:2], 0.0)).astype(BF16)
    out = h_ref[...] + jnp.dot(weights, yb, preferred_element_type=F32)
    o_ref[...] = _rms(out, gf_ref[...]) if apply_final_norm else out


def _combine(h2d, col, seg, y, off, final_g, *, apply_final_norm):
    t, d = h2d.shape
    tb = ROUTE_TILE
    n_tiles = t // tb
    seg_spec = lambda index: pl.BlockSpec((1, SUBLANES, LANES), index, memory_space=pltpu.SMEM)
    grid_spec = pltpu.PrefetchScalarGridSpec(
        num_scalar_prefetch=1,
        grid=(n_tiles,),
        in_specs=[
            pl.BlockSpec((tb, d), lambda i, off: (i, 0)),
            pl.BlockSpec((tb, LANES), lambda i, off: (i, 0)),
            seg_spec(lambda i, off: (i, 0, 0)),
            seg_spec(lambda i, off: (jnp.minimum(i + 1, n_tiles - 1), 0, 0)),
            pl.BlockSpec(memory_space=pl.ANY),
            pl.BlockSpec((1, d), lambda i, off: (0, 0)),
        ],
        out_specs=pl.BlockSpec((tb, d), lambda i, off: (i, 0)),
        scratch_shapes=[pltpu.VMEM((2, SORT_ROWS, d), F32), pltpu.SemaphoreType.DMA((2,))],
    )
    return pl.pallas_call(
        functools.partial(_combine_kernel, apply_final_norm=apply_final_norm),
        out_shape=jax.ShapeDtypeStruct((t, d), F32),
        grid_spec=grid_spec,
        compiler_params=pltpu.CompilerParams(
            dimension_semantics=("arbitrary",), vmem_limit_bytes=VMEM_LIMIT),
        name="moe_combine",
    )(off, h2d, col, seg, seg, y, final_g.reshape(1, d))


def _moe(h2d, g, router_w, w1, w3, w2, layer, final_g, *, apply_final_norm):
    t, d = h2d.shape
    tm = EXPERT_TILE
    n_tiles = t // ROUTE_TILE
    n_rows = 2 * t + n_tiles * N_EXPERTS * SUBLANES + N_EXPERTS * tm
    n_rows = -(-n_rows // tm) * tm
    col, rowp, seg, cnt = _router(h2d, g, router_w)
    counts = cnt[0, :N_EXPERTS].astype(jnp.int32)
    blocks = (counts + tm - 1) // tm
    ends = jnp.cumsum(blocks)
    off = (ends - blocks) * tm
    n_active = ends[-1:]
    blk = jnp.arange(n_rows // tm, dtype=jnp.int32)
    blk_expert = jnp.sum(jnp.minimum(blk, n_active - 1)[:, None] >= ends[None, :], axis=1)
    blk_expert = blk_expert.astype(jnp.int32)
    valid_rows = jnp.clip(counts[blk_expert] - (blk - (ends - blocks)[blk_expert]) * tm, 0, tm)
    blk_expert = blk_expert + layer * N_EXPERTS
    xs = _dispatch(h2d, g, rowp, seg, off, (off + counts) // SUBLANES,
                   (off + blocks * tm) // SUBLANES, n_active, n_rows)
    y = _experts(xs, blk_expert, n_active, valid_rows, w1, w3, w2, tf=1792)
    return _combine(h2d, col, seg, y, off, final_g, apply_final_norm=apply_final_norm)


def kernel(x, e_norm1, e_w_in, e_conv_w, e_conv_b, e_ln_g, e_ln_b, e_pool_w, e_pool_scale,
           e_w_out, e_norm2, e_ff_w1, e_ff_w3, e_ff_w2, o_norm1, o_w_qkv, o_b_qkv, o_sinks,
           o_w_o, o_b_o, o_norm2, o_router, o_exp_w1, o_exp_w3, o_exp_w2, final_norm):
    b, s, d = x.shape
    assert DEPTH % 2 == 0, "the final norm is fused into the last (odd, MoE) layer"
    expert_w = [w.astype(BF16).reshape((-1,) + w.shape[2:]) for w in (o_exp_w1, o_exp_w3, o_exp_w2)]
    h = x
    for layer in range(DEPTH):
        i = layer // 2
        if layer % 2 == 0:
            h = _even_mixer(h, e_norm1[i], e_w_in[i], e_conv_w[i], e_conv_b[i], e_ln_g[i],
                            e_ln_b[i], e_pool_w[i], e_pool_scale[i], e_w_out[i])
            h = _ffn(h.reshape(b * s, d), e_norm2[i], e_ff_w1[i], e_ff_w3[i], e_ff_w2[i],
                     tm=1024, tf=256).reshape(b, s, d)
        else:
            h = _attn_mixer(h, o_norm1[i], o_w_qkv[i], o_b_qkv[i], o_sinks[i], o_w_o[i],
                            o_b_o[i])
            h = _moe(h.reshape(b * s, d), o_norm2[i], o_router[i], *expert_w, i, final_norm,
                     apply_final_norm=(layer == DEPTH - 1)).reshape(b, s, d)
    return h
```

```python
import functools

import jax
import jax.numpy as jnp
from jax import lax
from jax.experimental import pallas as pl
from jax.experimental.pallas import tpu as pltpu

F32 = jnp.float32
BF16 = jnp.bfloat16

D_MODEL = 1024
DEPTH = 4
RMS_EPS = 1e-5
LN_EPS = 1e-5

D_CONV = 512
D_POOL = 512
CONV_K = 31
POOL_WINDOWS = (2, 4, 8, 16)
POOL_GROUP = 128
D_IN_EVEN = 2 * D_CONV + D_POOL

HEAD_DIM = 64
N_Q_HEADS = 16
N_KV_HEADS = 4
ATTN_BLOCK = 128
D_Q = N_Q_HEADS * HEAD_DIM
D_KV = N_KV_HEADS * HEAD_DIM
D_QKV = D_Q + 2 * D_KV

N_EXPERTS = 8

LANES = 128
SUBLANES = 8
SEQ_TILE = 512
HALO = 32
CONV_ROWS = 32
SCORE_LOOKAHEAD = 2
FFN_LOOKAHEAD = 2
VMEM_LIMIT = 56 * 1024 * 1024


def _rms(x, g):
    ms = jnp.mean(x * x, axis=-1, keepdims=True)
    return x * lax.rsqrt(ms + RMS_EPS) * g


def _silu(x):
    return x * jax.nn.sigmoid(x)


def _even_mixer_kernel(x_ref, g_ref, win_ref, cw_ref, cb_ref, lg_ref, lb_ref,
                       pw_ref, ps_ref, wout_ref, o_ref,
                       abuf, ashift, cwb, bbuf, p1, p2, p3, cat):
    s = pl.program_id(1)
    ts = SEQ_TILE
    rows = HALO + ts

    @pl.when(s == 0)
    def _():
        abuf[0:HALO, :] = jnp.zeros((HALO, D_CONV), F32)
        bbuf[0:HALO, :] = jnp.zeros((HALO, D_POOL), F32)

    x = x_ref[0]
    hn = _rms(x, g_ref[...]).astype(BF16)
    u = jnp.dot(hn, win_ref[...], preferred_element_type=F32)
    abuf[HALO:rows, :] = u[:, :D_CONV] * jax.nn.sigmoid(u[:, D_CONV:2 * D_CONV])
    bbuf[HALO:rows, :] = u[:, 2 * D_CONV:]

    a_all = abuf[...]
    for r in range(1, SUBLANES):
        ashift[r - 1] = pltpu.roll(a_all, rows - r, axis=0)
    for k in range(CONV_K):
        cwb[k] = jnp.broadcast_to(cw_ref[k:k + 1, :], (SUBLANES, D_CONV))
    for c in range(ts // CONV_ROWS):
        r0 = c * CONV_ROWS
        acc = jnp.broadcast_to(cb_ref[...], (CONV_ROWS, D_CONV))
        for k in range(CONV_K):
            q8, r = divmod(HALO - (CONV_K - 1) + k, SUBLANES)
            src = abuf if r == 0 else ashift.at[r - 1]
            start = r0 + q8 * SUBLANES
            tap = jnp.concatenate([cwb[k]] * (CONV_ROWS // SUBLANES), axis=0)
            acc = acc + tap * src[start:start + CONV_ROWS, :]
        mu = jnp.mean(acc, axis=-1, keepdims=True)
        d = acc - mu
        var = jnp.mean(d * d, axis=-1, keepdims=True)
        y = d * lax.rsqrt(var + LN_EPS) * lg_ref[...] + lb_ref[...]
        cat[r0:r0 + CONV_ROWS, 0:D_CONV] = _silu(y).astype(BF16)

    p1[8:rows, :] = bbuf[8:rows, :] + bbuf[7:rows - 1, :]
    p2[16:rows, 0:384] = p1[16:rows, 128:512] + p1[14:rows - 2, 128:512]
    p3[24:rows, 0:256] = p2[24:rows, 128:384] + p2[20:rows - 4, 128:384]
    s16 = p3[HALO:rows, 128:256] + p3[HALO - 8:rows - 8, 128:256]
    sums = (p1[HALO:rows, 0:128], p2[HALO:rows, 0:128], p3[HALO:rows, 0:128], s16)
    pos1 = (s * ts + 1 + lax.broadcasted_iota(jnp.int32, (ts, 1), 0)).astype(F32)
    for g, w in enumerate(POOL_WINDOWS):
        cnt = jnp.minimum(pos1, float(w))
        pg = sums[g] / cnt - bbuf[HALO:rows, g * POOL_GROUP:(g + 1) * POOL_GROUP]
        pm = jnp.dot(pg.astype(BF16), pw_ref[g], preferred_element_type=F32)
        pm = pm * ps_ref[:, g * POOL_GROUP:(g + 1) * POOL_GROUP]
        cat[:, D_CONV + g * POOL_GROUP:D_CONV + (g + 1) * POOL_GROUP] = pm.astype(BF16)

    o_ref[0] = x + jnp.dot(cat[...], wout_ref[...], preferred_element_type=F32)

    abuf[0:HALO, :] = abuf[ts:rows, :]
    bbuf[0:HALO, :] = bbuf[ts:rows, :]


def _even_mixer(h, g, w_in, conv_w, conv_b, ln_g, ln_b, pool_w, pool_scale, w_out):
    b, s, d = h.shape
    ts = SEQ_TILE
    rows = HALO + ts
    const = lambda shape: pl.BlockSpec(shape, lambda i, j: (0,) * len(shape))
    return pl.pallas_call(
        _even_mixer_kernel,
        out_shape=jax.ShapeDtypeStruct(h.shape, F32),
        grid=(b, s // ts),
        in_specs=[
            pl.BlockSpec((1, ts, d), lambda i, j: (i, j, 0)),
            const((1, d)),
            const((d, D_IN_EVEN)),
            const((CONV_K, D_CONV)),
            const((1, D_CONV)),
            const((1, D_CONV)),
            const((1, D_CONV)),
            const((len(POOL_WINDOWS), POOL_GROUP, POOL_GROUP)),
            const((1, D_POOL)),
            const((d, d)),
        ],
        out_specs=pl.BlockSpec((1, ts, d), lambda i, j: (i, j, 0)),
        scratch_shapes=[
            pltpu.VMEM((rows, D_CONV), F32),
            pltpu.VMEM((SUBLANES - 1, rows, D_CONV), F32),
            pltpu.VMEM((CONV_K, SUBLANES, D_CONV), F32),
            pltpu.VMEM((rows, D_POOL), F32),
            pltpu.VMEM((rows, D_POOL), F32),
            pltpu.VMEM((rows, 384), F32),
            pltpu.VMEM((rows, 256), F32),
            pltpu.VMEM((ts, d), BF16),
        ],
        compiler_params=pltpu.CompilerParams(
            dimension_semantics=("arbitrary", "arbitrary"),
            vmem_limit_bytes=VMEM_LIMIT),
        name="even_mixer",
    )(h, g.reshape(1, d), w_in.astype(BF16), conv_w, conv_b.reshape(1, -1),
      ln_g.reshape(1, -1), ln_b.reshape(1, -1), pool_w.astype(BF16),
      pool_scale.reshape(1, -1), w_out.astype(BF16))


def _attn_kernel(sink_ref, x_ref, g_ref, wqkv_ref, bqkv_ref, wot_ref, bo_ref, o_ref,
                 kbuf, vtbuf, otbuf):
    s = pl.program_id(1)
    ts = SEQ_TILE
    blk = ATTN_BLOCK
    group = N_Q_HEADS // N_KV_HEADS

    @pl.when(s == 0)
    def _():
        kbuf[0:blk, :] = jnp.zeros((blk, N_KV_HEADS * LANES), BF16)
        vtbuf[:, 0:blk] = jnp.zeros((D_KV, blk), BF16)

    x = x_ref[0]
    hn = _rms(x, g_ref[...]).astype(BF16)
    u = jnp.dot(hn, wqkv_ref[...], preferred_element_type=F32) + bqkv_ref[...]
    q = (u[:, :D_Q] * (HEAD_DIM ** -0.5)).astype(BF16)

    lo_t = lax.broadcasted_iota(jnp.int32, (ts, LANES), 1) < HEAD_DIM
    for p in range(N_KV_HEADS // 2):
        t = u[:, D_Q + p * LANES:D_Q + (p + 1) * LANES]
        r = pltpu.roll(t, HEAD_DIM, axis=1)
        kbuf[blk:blk + ts, (2 * p) * LANES:(2 * p + 1) * LANES] = jnp.where(lo_t, t, r).astype(BF16)
        kbuf[blk:blk + ts, (2 * p + 1) * LANES:(2 * p + 2) * LANES] = jnp.where(lo_t, r, t).astype(BF16)
    vtbuf[:, blk:blk + ts] = jnp.transpose(u[:, D_Q + D_KV:]).astype(BF16)

    lo_b = lax.broadcasted_iota(jnp.int32, (blk, LANES), 1) < HEAD_DIM
    zero_b = jnp.zeros((blk, LANES), BF16)
    key = lax.broadcasted_iota(jnp.int32, (2 * blk, blk), 0)
    qry = lax.broadcasted_iota(jnp.int32, (2 * blk, blk), 1)
    band = (key > qry) & (key <= qry + blk)
    band0 = band & ((key >= blk) | (s > 0))
    neg = jnp.concatenate([jnp.where(band, 0.0, -jnp.inf)] * group, axis=1)
    neg0 = jnp.concatenate([jnp.where(band0, 0.0, -jnp.inf)] * group, axis=1)

    def scores(j, g):
        kk = kbuf[j * blk:(j + 2) * blk, g * LANES:(g + 1) * LANES]
        qs = []
        sk = []
        for i in range(group):
            h = g * group + i
            qt = q[j * blk:(j + 1) * blk, (h // 2) * LANES:(h // 2 + 1) * LANES]
            qs.append(jnp.where(lo_b, qt, zero_b) if h % 2 == 0 else jnp.where(lo_b, zero_b, qt))
            sk.append(jnp.full((1, blk), sink_ref[h], F32))
        q4 = jnp.concatenate(qs, axis=0)
        st = lax.dot_general(kk, q4, (((1,), (1,)), ((), ())), preferred_element_type=F32)
        return st + (neg0 if j == 0 else neg), jnp.concatenate(sk, axis=1)

    pairs = [(j, g) for j in range(ts // blk) for g in range(N_KV_HEADS)]
    ahead = [scores(*p) for p in pairs[:SCORE_LOOKAHEAD]]
    for idx, (j, g) in enumerate(pairs):
        st, sink = ahead.pop(0)
        if idx + SCORE_LOOKAHEAD < len(pairs):
            ahead.append(scores(*pairs[idx + SCORE_LOOKAHEAD]))
        vt = vtbuf[g * HEAD_DIM:(g + 1) * HEAD_DIM, j * blk:(j + 2) * blk]
        m = jnp.maximum(jnp.max(st, axis=0, keepdims=True), sink)
        e = jnp.exp(st - m)
        denom = jnp.sum(e, axis=0, keepdims=True) + jnp.exp(sink - m)
        ot = jnp.dot(vt, e.astype(BF16), preferred_element_type=F32) * (1.0 / denom)
        for i in range(group):
            h = g * group + i
            otbuf[h * HEAD_DIM:(h + 1) * HEAD_DIM, j * blk:(j + 1) * blk] = (
                ot[:, i * blk:(i + 1) * blk].astype(BF16))

    proj_t = jnp.dot(wot_ref[...], otbuf[...], preferred_element_type=F32)
    o_ref[0] = x + jnp.transpose(proj_t) + bo_ref[...]

    kbuf[0:blk, :] = kbuf[ts:ts + blk, :]
    vtbuf[:, 0:blk] = vtbuf[:, ts:ts + blk]


def _attn_mixer(h, g, w_qkv, b_qkv, sinks, w_o, b_o):
    b, s, d = h.shape
    ts = SEQ_TILE
    const = lambda shape: pl.BlockSpec(shape, lambda i, j, sk: (0,) * len(shape))
    grid_spec = pltpu.PrefetchScalarGridSpec(
        num_scalar_prefetch=1,
        grid=(b, s // ts),
        in_specs=[
            pl.BlockSpec((1, ts, d), lambda i, j, sk: (i, j, 0)),
            const((1, d)),
            const((d, D_QKV)),
            const((1, D_QKV)),
            const((d, D_Q)),
            const((1, d)),
        ],
        out_specs=pl.BlockSpec((1, ts, d), lambda i, j, sk: (i, j, 0)),
        scratch_shapes=[
            pltpu.VMEM((ATTN_BLOCK + ts, N_KV_HEADS * LANES), BF16),
            pltpu.VMEM((D_KV, ATTN_BLOCK + ts), BF16),
            pltpu.VMEM((D_Q, ts), BF16),
        ],
    )
    return pl.pallas_call(
        _attn_kernel,
        out_shape=jax.ShapeDtypeStruct(h.shape, F32),
        grid_spec=grid_spec,
        compiler_params=pltpu.CompilerParams(
            dimension_semantics=("arbitrary", "arbitrary"),
            vmem_limit_bytes=VMEM_LIMIT),
        name="attn_mixer",
    )(sinks, h, g.reshape(1, d), w_qkv.astype(BF16), b_qkv.reshape(1, -1),
      jnp.transpose(w_o).astype(BF16), b_o.reshape(1, d))


def _ffn_kernel(x_ref, g_ref, w1_ref, w3_ref, w2_ref, o_ref, hid, *, tf):
    x = x_ref[...]
    hn = _rms(x, g_ref[...]).astype(BF16)

    def up(f0):
        return (jnp.dot(hn, w1_ref[:, f0:f0 + tf], preferred_element_type=F32),
                jnp.dot(hn, w3_ref[:, f0:f0 + tf], preferred_element_type=F32))

    starts = list(range(0, w1_ref.shape[1], tf))
    ahead = [up(f0) for f0 in starts[:FFN_LOOKAHEAD]]
    for idx, f0 in enumerate(starts):
        h1, h3 = ahead.pop(0)
        if idx + FFN_LOOKAHEAD < len(starts):
            ahead.append(up(starts[idx + FFN_LOOKAHEAD]))
        hid[:, f0:f0 + tf] = (_silu(h1) * h3).astype(BF16)
    o_ref[...] = x + jnp.dot(hid[...], w2_ref[...], preferred_element_type=F32)


def _ffn(h2d, g, w1, w3, w2, *, tm, tf):
    t, d = h2d.shape
    ff = w1.shape[1]
    resident = lambda shape: pl.BlockSpec(shape, lambda i: (0, 0), pipeline_mode=pl.Buffered(1))
    return pl.pallas_call(
        functools.partial(_ffn_kernel, tf=tf),
        out_shape=jax.ShapeDtypeStruct((t, d), F32),
        grid=(t // tm,),
        in_specs=[
            pl.BlockSpec((tm, d), lambda i: (i, 0)),
            pl.BlockSpec((1, d), lambda i: (0, 0)),
            resident((d, ff)),
            resident((d, ff)),
            resident((ff, d)),
        ],
        out_specs=pl.BlockSpec((tm, d), lambda i: (i, 0)),
        scratch_shapes=[pltpu.VMEM((tm, ff), BF16)],
        compiler_params=pltpu.CompilerParams(
            dimension_semantics=("arbitrary",), vmem_limit_bytes=VMEM_LIMIT),
        name="dense_ffn",
    )(h2d, g.reshape(1, d), w1.astype(BF16), w3.astype(BF16), w2.astype(BF16))


ROUTE_TILE = 512
EXPERT_TILE = 512
EXPERT_UP_CHUNK = 256
SORT_ROWS = 2 * ROUTE_TILE + N_EXPERTS * SUBLANES
SEG_CHUNKS = tuple(SUBLANES << j for j in range((ROUTE_TILE // SUBLANES).bit_length() - 1, -1, -1))


def _router_kernel(x_ref, g_ref, rwt_ref, col_ref, rowp_ref, seg_ref, cnt_ref, carry, upper):
    i = pl.program_id(0)
    tb = ROUTE_TILE
    ne = N_EXPERTS

    @pl.when(i == 0)
    def _():
        carry[...] = jnp.zeros(carry.shape, F32)
        rr = lax.broadcasted_iota(jnp.int32, (tb, tb), 0)
        cc = lax.broadcasted_iota(jnp.int32, (tb, tb), 1)
        upper[...] = jnp.where(rr < cc, 1.0, 0.0).astype(BF16)

    def nt(a, bmat):
        return lax.dot_general(a, bmat, (((1,), (1,)), ((), ())), preferred_element_type=F32)

    hn = _rms(x_ref[...], g_ref[...])
    rwt = rwt_ref[...]
    hn_hi = hn.astype(BF16)
    hn_lo = (hn - hn_hi.astype(F32)).astype(BF16)
    rw_hi = rwt.astype(BF16)
    rw_lo = (rwt - rw_hi.astype(F32)).astype(BF16)
    logits_t = nt(rw_hi, hn_hi) + (nt(rw_hi, hn_lo) + nt(rw_lo, hn_hi))
    lg = logits_t[0:ne, :]
    sub = lax.broadcasted_iota(jnp.int32, (ne, tb), 0)
    m1 = jnp.max(lg, axis=0, keepdims=True)
    i1 = jnp.min(jnp.where(lg == m1, sub, ne), axis=0, keepdims=True)
    lg2 = jnp.where(sub == i1, -jnp.inf, lg)
    m2 = jnp.max(lg2, axis=0, keepdims=True)
    i2 = jnp.min(jnp.where(lg2 == m2, sub, ne), axis=0, keepdims=True)
    e2 = jnp.exp(m2 - m1)
    den = 1.0 + e2

    mem = jnp.where((sub == i1) | (sub == i2), 1.0, 0.0)
    mem_rows = jnp.concatenate([mem, jnp.zeros((LANES - ne, tb), F32)], axis=0).astype(BF16)
    before = jnp.dot(mem_rows[0:2 * ne, :], upper[...], preferred_element_type=F32)[0:ne, :]

    n = nt(jnp.ones((2 * SUBLANES, tb), BF16), mem_rows)[0:SUBLANES, :]
    n8 = jnp.floor((n + (SUBLANES - 1)) / SUBLANES) * SUBLANES
    incl = n8
    for sh in (1, 2, 4):
        incl = incl + pltpu.roll(incl, sh, axis=1)
    seg8 = incl - n8
    seg8_t = jnp.transpose(jnp.broadcast_to(seg8[0:1, :], (LANES, LANES)))[0:ne, :]
    pos = before + jnp.concatenate([seg8_t] * (tb // LANES), axis=1)
    rp1 = jnp.sum(jnp.where(sub == i1, pos, 0.0), axis=0, keepdims=True)
    rp2 = jnp.sum(jnp.where(sub == i2, pos, 0.0), axis=0, keepdims=True)

    rowp_ref[0] = jnp.where(sub == 0, rp1, jnp.where(sub == 1, rp2, 0.0)).astype(jnp.int32)
    table = jnp.where(sub == 0, 1.0 / den,
                      jnp.where(sub == 1, e2 / den,
                                jnp.where(sub == 2, rp1, jnp.where(sub == 3, rp2, 0.0))))
    col_ref[...] = jnp.transpose(
        jnp.concatenate([table, jnp.zeros((LANES - ne, tb), F32)], axis=0))
    fld = lax.broadcasted_iota(jnp.int32, (SUBLANES, LANES), 0)
    seg_ref[0] = jnp.where(fld == 0, n8, jnp.where(fld == 1, seg8,
                                                   jnp.where(fld == 2, carry[...], 0.0))).astype(jnp.int32)
    carry[...] = carry[...] + n8
    cnt_ref[...] = carry[...]


def _router(h2d, g, router_w):
    t, d = h2d.shape
    tb = ROUTE_TILE
    rwt = jnp.zeros((LANES, d), F32).at[:N_EXPERTS, :].set(jnp.transpose(router_w))
    return pl.pallas_call(
        _router_kernel,
        out_shape=(
            jax.ShapeDtypeStruct((t, LANES), F32),
            jax.ShapeDtypeStruct((t // tb, SUBLANES, tb), jnp.int32),
            jax.ShapeDtypeStruct((t // tb, SUBLANES, LANES), jnp.int32),
            jax.ShapeDtypeStruct((SUBLANES, LANES), F32),
        ),
        grid=(t // tb,),
        in_specs=[
            pl.BlockSpec((tb, d), lambda i: (i, 0)),
            pl.BlockSpec((1, d), lambda i: (0, 0)),
            pl.BlockSpec((LANES, d), lambda i: (0, 0)),
        ],
        out_specs=(
            pl.BlockSpec((tb, LANES), lambda i: (i, 0)),
            pl.BlockSpec((1, SUBLANES, tb), lambda i: (i, 0, 0)),
            pl.BlockSpec((1, SUBLANES, LANES), lambda i: (i, 0, 0)),
            pl.BlockSpec((SUBLANES, LANES), lambda i: (0, 0)),
        ),
        scratch_shapes=[pltpu.VMEM((SUBLANES, LANES), F32), pltpu.VMEM((tb, tb), BF16)],
        compiler_params=pltpu.CompilerParams(
            dimension_semantics=("arbitrary",), vmem_limit_bytes=VMEM_LIMIT),
        name="moe_router",
    )(h2d, g.reshape(1, d), rwt)


def _segment_copies(seg_ref, off_ref, make_copy, do_start):
    for e in range(N_EXPERTS):
        n8 = seg_ref[0, 0, e]
        tile_row = seg_ref[0, 1, e]
        group_row = off_ref[e] + seg_ref[0, 2, e]
        for c in SEG_CHUNKS:
            done = n8 & ~(2 * c - 1)

            @pl.when((n8 & c) != 0)
            def _():
                cp = make_copy(pl.multiple_of(tile_row + done, SUBLANES),
                               pl.multiple_of(group_row + done, SUBLANES), c)
                cp.start(priority=e % 2) if do_start else cp.wait()


def _dispatch_kernel(off_ref, pad_lo_ref, pad_hi_ref, na_ref, h_ref, g_ref, rowp_ref, seg_ref,
                     seg_prev_ref, xs_ref, sorted_s, zblk, sem, zsem):
    tb = ROUTE_TILE
    tm = EXPERT_TILE
    i = pl.program_id(0)
    slot = i % 2

    @pl.when(i == 0)
    def _():
        zblk[...] = jnp.zeros(zblk.shape, F32)
        n_blocks = xs_ref.shape[0] // tm

        def fill(do_start):
            def pad_group(r, c):
                cp = pltpu.make_async_copy(
                    zblk.at[pl.ds(0, SUBLANES)],
                    xs_ref.at[pl.ds(pl.multiple_of(r * SUBLANES, SUBLANES), SUBLANES)], zsem)
                cp.start() if do_start else cp.wait()
                return c

            def tail_blk(b, c):
                cp = pltpu.make_async_copy(
                    zblk, xs_ref.at[pl.ds(pl.multiple_of(b * tm, tm), tm)], zsem)
                cp.start() if do_start else cp.wait()
                return c

            for e in range(N_EXPERTS):
                lax.fori_loop(pad_lo_ref[e], pad_hi_ref[e], pad_group, 0)
            lax.fori_loop(na_ref[0], n_blocks, tail_blk, 0)

        fill(True)
        fill(False)

    hn = _rms(h_ref[...], g_ref[...]).astype(BF16)
    row = lax.broadcasted_iota(jnp.int32, (SORT_ROWS, tb), 0)
    hit = (row == rowp_ref[0, 0:1, :]) | (row == rowp_ref[0, 1:2, :])
    sorted_s[slot] = jnp.dot(jnp.where(hit, 1.0, 0.0).astype(BF16), hn, preferred_element_type=F32)

    def copier(buf):
        def make_copy(tile_row, group_row, rows):
            return pltpu.make_async_copy(sorted_s.at[buf, pl.ds(tile_row, rows)],
                                         xs_ref.at[pl.ds(group_row, rows)], sem.at[buf])
        return make_copy

    _segment_copies(seg_ref, off_ref, copier(slot), True)

    @pl.when(i > 0)
    def _():
        _segment_copies(seg_prev_ref, off_ref, copier(1 - slot), False)

    @pl.when(i == pl.num_programs(0) - 1)
    def _():
        _segment_copies(seg_ref, off_ref, copier(slot), False)


def _dispatch(h2d, g, rowp, seg, off, pad_lo, pad_hi, n_active, n_rows):
    t, d = h2d.shape
    tb = ROUTE_TILE
    seg_spec = lambda index: pl.BlockSpec((1, SUBLANES, LANES), index, memory_space=pltpu.SMEM)
    grid_spec = pltpu.PrefetchScalarGridSpec(
        num_scalar_prefetch=4,
        grid=(t // tb,),
        in_specs=[
            pl.BlockSpec((tb, d), lambda i, *_: (i, 0)),
            pl.BlockSpec((1, d), lambda i, *_: (0, 0)),
            pl.BlockSpec((1, SUBLANES, tb), lambda i, *_: (i, 0, 0)),
            seg_spec(lambda i, *_: (i, 0, 0)),
            seg_spec(lambda i, *_: (jnp.maximum(i - 1, 0), 0, 0)),
        ],
        out_specs=pl.BlockSpec(memory_space=pl.ANY),
        scratch_shapes=[pltpu.VMEM((2, SORT_ROWS, d), F32),
                        pltpu.VMEM((EXPERT_TILE, d), F32),
                        pltpu.SemaphoreType.DMA((2,)),
                        pltpu.SemaphoreType.DMA(())],
    )
    return pl.pallas_call(
        _dispatch_kernel,
        out_shape=jax.ShapeDtypeStruct((n_rows, d), F32),
        grid_spec=grid_spec,
        compiler_params=pltpu.CompilerParams(
            dimension_semantics=("arbitrary",), has_side_effects=True,
            vmem_limit_bytes=VMEM_LIMIT),
        name="moe_dispatch",
    )(off, pad_lo, pad_hi, n_active, h2d, g.reshape(1, d), rowp, seg, seg)


def _expert_kernel(be_ref, na_ref, vr_ref, x_ref, w1_ref, w3_ref, w2_ref, y_ref, hid):
    i = pl.program_id(0)
    f = pl.program_id(1)
    tm = y_ref.shape[0]

    def swiglu_rows(rows):
        x = x_ref[0:rows, :].astype(BF16)

        def up(f0):
            return (jnp.dot(x, w1_ref[0, :, f0:f0 + EXPERT_UP_CHUNK], preferred_element_type=F32),
                    jnp.dot(x, w3_ref[0, :, f0:f0 + EXPERT_UP_CHUNK], preferred_element_type=F32))

        starts = list(range(0, w1_ref.shape[2], EXPERT_UP_CHUNK))
        ahead = [up(f0) for f0 in starts[:FFN_LOOKAHEAD]]
        for idx, f0 in enumerate(starts):
            h1, h3 = ahead.pop(0)
            if idx + FFN_LOOKAHEAD < len(starts):
                ahead.append(up(starts[idx + FFN_LOOKAHEAD]))
            hid[0:rows, f0:f0 + EXPERT_UP_CHUNK] = (_silu(h1) * h3).astype(BF16)
        part = jnp.dot(hid[0:rows, :], w2_ref[0], preferred_element_type=F32)

        @pl.when(f == 0)
        def _():
            y_ref[0:rows, :] = part
            if rows < tm:
                y_ref[rows:tm, :] = jnp.zeros((tm - rows, y_ref.shape[1]), F32)

        @pl.when(f > 0)
        def _():
            y_ref[0:rows, :] += part

    active = i < na_ref[0]
    half_full = vr_ref[i] <= tm // 2

    @pl.when(active & jnp.logical_not(half_full))
    def _():
        swiglu_rows(tm)

    @pl.when(active & half_full)
    def _():
        swiglu_rows(tm // 2)

    @pl.when(jnp.logical_not(active) & (f == 0))
    def _():
        y_ref[...] = jnp.zeros(y_ref.shape, F32)


def _experts(xs, blk_expert, n_active, valid_rows, w1, w3, w2, *, tf):
    r, d = xs.shape
    tm = EXPERT_TILE
    ff = w1.shape[2]
    nf = ff // tf

    def row_blk(i, f, be, na, vr):
        return (jnp.maximum(jnp.minimum(i, na[0] - 1), 0), 0)

    def f_blk(i, f, na):
        return jnp.where(i < na[0], f, nf - 1)

    grid_spec = pltpu.PrefetchScalarGridSpec(
        num_scalar_prefetch=3,
        grid=(r // tm, nf),
        in_specs=[
            pl.BlockSpec((tm, d), row_blk),
            pl.BlockSpec((1, d, tf), lambda i, f, be, na, vr: (be[i], 0, f_blk(i, f, na))),
            pl.BlockSpec((1, d, tf), lambda i, f, be, na, vr: (be[i], 0, f_blk(i, f, na))),
            pl.BlockSpec((1, tf, d), lambda i, f, be, na, vr: (be[i], f_blk(i, f, na), 0)),
        ],
        out_specs=pl.BlockSpec((tm, d), lambda i, f, be, na, vr: (i, 0)),
        scratch_shapes=[pltpu.VMEM((tm, tf), BF16)],
    )
    return pl.pallas_call(
        _expert_kernel,
        out_shape=jax.ShapeDtypeStruct((r, d), F32),
        grid_spec=grid_spec,
        compiler_params=pltpu.CompilerParams(
            dimension_semantics=("arbitrary", "arbitrary"),
            vmem_limit_bytes=VMEM_LIMIT),
        name="moe_experts",
    )(blk_expert, n_active, valid_rows, xs, w1, w3, w2)


def _combine_kernel(off_ref, h_ref, col_ref, seg_ref, seg_next_ref, y_ref, gf_ref, o_ref,
                    ysort, sem, *, apply_final_norm):
    tb = ROUTE_TILE
    i = pl.program_id(0)
    slot = i % 2

    def copier(buf):
        def make_copy(tile_row, group_row, rows):
            return pltpu.make_async_copy(y_ref.at[pl.ds(group_row, rows)],
                                         ysort.at[buf, pl.ds(tile_row, rows)], sem.at[buf])
        return make_copy

    @pl.when(i == 0)
    def _():
        ysort[...] = jnp.zeros(ysort.shape, F32)
        _segment_copies(seg_ref, off_ref, copier(0), True)

    @pl.when(i + 1 < pl.num_programs(0))
    def _():
        _segment_copies(seg_next_ref, off_ref, copier(1 - slot), True)

    _segment_copies(seg_ref, off_ref, copier(slot), False)

    col = col_ref[...]
    row = lax.broadcasted_iota(jnp.int32, (tb, SORT_ROWS), 1)
    yb = ysort[slot].astype(BF16)
    weights = (jnp.where(row == col[:, 2:3].astype(jnp.int32), col[:, 0:1], 0.0)
               + jnp.where(row == col[:, 3:4].astype(jnp.int32), col[:, 1:2], 0.0)).astype(BF16)
    out = h_ref[...] + jnp.dot(weights, yb, preferred_element_type=F32)
    o_ref[...] = _rms(out, gf_ref[...]) if apply_final_norm else out


def _combine(h2d, col, seg, y, off, final_g, *, apply_final_norm):
    t, d = h2d.shape
    tb = ROUTE_TILE
    n_tiles = t // tb
    seg_spec = lambda index: pl.BlockSpec((1, SUBLANES, LANES), index, memory_space=pltpu.SMEM)
    grid_spec = pltpu.PrefetchScalarGridSpec(
        num_scalar_prefetch=1,
        grid=(n_tiles,),
        in_specs=[
            pl.BlockSpec((tb, d), lambda i, off: (i, 0)),
            pl.BlockSpec((tb, LANES), lambda i, off: (i, 0)),
            seg_spec(lambda i, off: (i, 0, 0)),
            seg_spec(lambda i, off: (jnp.minimum(i + 1, n_tiles - 1), 0, 0)),
            pl.BlockSpec(memory_space=pl.ANY),
            pl.BlockSpec((1, d), lambda i, off: (0, 0)),
        ],
        out_specs=pl.BlockSpec((tb, d), lambda i, off: (i, 0)),
        scratch_shapes=[pltpu.VMEM((2, SORT_ROWS, d), F32), pltpu.SemaphoreType.DMA((2,))],
    )
    return pl.pallas_call(
        functools.partial(_combine_kernel, apply_final_norm=apply_final_norm),
        out_shape=jax.ShapeDtypeStruct((t, d), F32),
        grid_spec=grid_spec,
        compiler_params=pltpu.CompilerParams(
            dimension_semantics=("arbitrary",), vmem_limit_bytes=VMEM_LIMIT),
        name="moe_combine",
    )(off, h2d, col, seg, seg, y, final_g.reshape(1, d))


def _moe(h2d, g, router_w, w1, w3, w2, layer, final_g, *, apply_final_norm):
    t, d = h2d.shape
    tm = EXPERT_TILE
    n_tiles = t // ROUTE_TILE
    n_rows = 2 * t + n_tiles * N_EXPERTS * SUBLANES + N_EXPERTS * tm
    n_rows = -(-n_rows // tm) * tm
    col, rowp, seg, cnt = _router(h2d, g, router_w)
    counts = cnt[0, :N_EXPERTS].astype(jnp.int32)
    blocks = (counts + tm - 1) // tm
    ends = jnp.cumsum(blocks)
    off = (ends - blocks) * tm
    n_active = ends[-1:]
    blk = jnp.arange(n_rows // tm, dtype=jnp.int32)
    blk_expert = jnp.sum(jnp.minimum(blk, n_active - 1)[:, None] >= ends[None, :], axis=1)
    blk_expert = blk_expert.astype(jnp.int32)
    valid_rows = jnp.clip(counts[blk_expert] - (blk - (ends - blocks)[blk_expert]) * tm, 0, tm)
    blk_expert = blk_expert + layer * N_EXPERTS
    xs = _dispatch(h2d, g, rowp, seg, off, (off + counts) // SUBLANES,
                   (off + blocks * tm) // SUBLANES, n_active, n_rows)
    y = _experts(xs, blk_expert, n_active, valid_rows, w1, w3, w2, tf=1792)
    return _combine(h2d, col, seg, y, off, final_g, apply_final_norm=apply_final_norm)


def kernel(x, e_norm1, e_w_in, e_conv_w, e_conv_b, e_ln_g, e_ln_b, e_pool_w, e_pool_scale,
           e_w_out, e_norm2, e_ff_w1, e_ff_w3, e_ff_w2, o_norm1, o_w_qkv, o_b_qkv, o_sinks,
           o_w_o, o_b_o, o_norm2, o_router, o_exp_w1, o_exp_w3, o_exp_w2, final_norm):
    b, s, d = x.shape
    assert DEPTH % 2 == 0, "the final norm is fused into the last (odd, MoE) layer"
    expert_w = [w.astype(BF16).reshape((-1,) + w.shape[2:]) for w in (o_exp_w1, o_exp_w3, o_exp_w2)]
    h = x
    for layer in range(DEPTH):
        i = layer // 2
        if layer % 2 == 0:
            h = _even_mixer(h, e_norm1[i], e_w_in[i], e_conv_w[i], e_conv_b[i], e_ln_g[i],
                            e_ln_b[i], e_pool_w[i], e_pool_scale[i], e_w_out[i])
            h = _ffn(h.reshape(b * s, d), e_norm2[i], e_ff_w1[i], e_ff_w3[i], e_ff_w2[i],
                     tm=1024, tf=256).reshape(b, s, d)
        else:
            h = _attn_mixer(h, o_norm1[i], o_w_qkv[i], o_b_qkv[i], o_sinks[i], o_w_o[i],
                            o_b_o[i])
            h = _moe(h.reshape(b * s, d), o_norm2[i], o_router[i], *expert_w, i, final_norm,
                     apply_final_norm=(layer == DEPTH - 1)).reshape(b, s, d)
    return h
```
